```python
import jax, jax.numpy as jnp
from jax import lax
import numpy as np

D_MODEL = 2048
BATCH = 4
SEQ = 2048
DEPTH = 2
DEC_BATCH = 128
DEC_SEQ = 1
PAST_LEN = 16384
PAGE_SIZE = 128

N_MIXERS = 2
N_A = (DEPTH + N_MIXERS - 1) // N_MIXERS
N_B = DEPTH // N_MIXERS
HEAD_SIZE = 64
N_HEADS = D_MODEL // HEAD_SIZE
D_DECAY_LORA = max(32, int(round(1.8 * D_MODEL ** 0.5 / 32) * 32))
D_AAA_LORA = max(32, int(round(2.5 * D_MODEL ** 0.5 / 32) * 32))
D_GATE_LORA = max(32, int(round(0.6 * D_MODEL ** 0.8 / 32) * 32))
LNX_EPS = 64e-5
D_RNN = D_MODEL
LRU_BLOCKS = 8
LRU_BW = D_RNN // LRU_BLOCKS
LRU_CONV = 4
LRU_C = 8.0
D_FF = 11 * D_MODEL // 4
FFN_CONV = 3
NORM_EPS = 1e-6
N_MOD = 6

kernel_name = "hybrid_rwkv7_rglru_convffn_step"


def rms_norm(x, g):
    xf = x.astype(jnp.float32)
    y = xf * lax.rsqrt(jnp.mean(xf * xf, axis=-1, keepdims=True) + NORM_EPS)
    return (y * g.astype(jnp.float32)).astype(x.dtype)


def causal_dwconv(x, buf, w, b):
    width = w.shape[0]
    T = x.shape[1]
    xp = jnp.concatenate([buf.astype(x.dtype), x], axis=1)
    y = b + xp[:, 0:T] * w[0]
    for j in range(1, width):
        y = y + xp[:, j:j + T] * w[j]
    return y, xp[:, T:]


def _wkv_step(S, inp):
    r_t, d_t, k_t, v_t, a_t, b_t = inp
    sa = jnp.einsum('bhvk,bhk->bhv', S, a_t)
    S = S * d_t[:, :, None, :] + sa[..., None] * b_t[:, :, None, :] + v_t[..., None] * k_t[:, :, None, :]
    y = jnp.einsum('bhvk,bhk->bhv', S, r_t)
    return S, y


def rwkv7_time_mix(x, shift, s0, mix, w_r, w_k, w_v, w_o, w0, w1, w2, a0, a1, a2,
                   g1, g2, k_k, k_a, r_k, lnx_g, lnx_b):
    B, T, D = x.shape
    f32 = jnp.float32
    x_prev = jnp.concatenate([shift[:, None, :].astype(x.dtype), x[:, :-1]], axis=1)
    xx = x_prev - x
    xr, xw, xk = x + xx * mix[0], x + xx * mix[1], x + xx * mix[2]
    xv, xa, xg = x + xx * mix[3], x + xx * mix[4], x + xx * mix[5]
    r = xr @ w_r
    k = xk @ w_k
    v = xv @ w_v
    w_log = -jax.nn.softplus(-(w0 + jnp.tanh(xw @ w1) @ w2)) - 0.5
    a = jax.nn.sigmoid(a0 + (xa @ a1) @ a2)
    g = jax.nn.sigmoid(xg @ g1) @ g2
    hd = lambda t: t.reshape(B, T, N_HEADS, HEAD_SIZE).astype(f32)
    kk = hd(k * k_k)
    kk = kk / jnp.maximum(jnp.sqrt(jnp.sum(kk * kk, axis=-1, keepdims=True)), 1e-12)
    k = k * (1 + (a - 1) * k_a)
    r_h, k_h, v_h, a_h = hd(r), hd(k), hd(v), hd(a)
    decay = jnp.exp(-jnp.exp(hd(w_log)))
    tm = lambda t: jnp.swapaxes(t, 0, 1)
    xs = (tm(r_h), tm(decay), tm(k_h), tm(v_h), tm(-kk), tm(kk * a_h))
    S_T, ys = lax.scan(_wkv_step, s0.astype(f32), xs)
    y = tm(ys)
    mu = jnp.mean(y, axis=-1, keepdims=True)
    var = jnp.mean(jnp.square(y - mu), axis=-1, keepdims=True)
    y = ((y - mu) * lax.rsqrt(var + LNX_EPS)).reshape(B, T, D) * lnx_g.astype(f32) + lnx_b.astype(f32)
    bonus = (jnp.sum(r_h * k_h * r_k.astype(f32), axis=-1, keepdims=True) * v_h).reshape(B, T, D)
    out = ((y + bonus).astype(x.dtype) * g) @ w_o
    return out, x[:, -1], S_T


def _lin_combine(e1, e2):
    a1, b1 = e1
    a2, b2 = e2
    return a1 * a2, a2 * b1 + b2


def rglru_block(x, h0, conv_buf, reset_first, w_in, b_in, conv_w, conv_b, w_a, b_a, w_x, b_x, lam, w_out):
    B, T, _ = x.shape
    f32 = jnp.float32
    proj = x @ w_in + b_in
    y_br = jax.nn.gelu(proj[..., :D_RNN], approximate=True)
    x_c, new_buf = causal_dwconv(proj[..., D_RNN:], conv_buf, conv_w, conv_b)
    xb = x_c.reshape(B, T, LRU_BLOCKS, LRU_BW)
    gate_a = jax.nn.sigmoid(jnp.einsum('btnk,nkj->btnj', xb, w_a).reshape(B, T, D_RNN) + b_a).astype(f32)
    gate_x = jax.nn.sigmoid(jnp.einsum('btnk,nkj->btnj', xb, w_x).reshape(B, T, D_RNN) + b_x).astype(f32)
    log_a = -LRU_C * gate_a * jax.nn.softplus(-lam.astype(f32))
    a = jnp.exp(log_a)
    mult = jnp.sqrt(-jnp.expm1(2.0 * log_a))
    if reset_first:
        mult = mult.at[:, 0].set(1.0)
    bt = mult * gate_x * x_c.astype(f32)
    bt = bt.at[:, 0].add(a[:, 0] * h0.astype(f32))
    _, h = lax.associative_scan(_lin_combine, (a, bt), axis=1)
    out = (h.astype(x.dtype) * y_br) @ w_out
    return out, h[:, -1], new_buf


def conv_ffn(x, buf, w_gate, w_up, w_down, conv_w, conv_b):
    gt = x @ w_gate
    u = x @ w_up
    gc, new_buf = causal_dwconv(gt, buf, conv_w, conv_b)
    return (jax.nn.gelu(gc, approximate=True) * u) @ w_down, new_buf


def trunk(x, c, st_wkv, st_shift, st_lru_h, st_lru_conv, st_ffn_conv, reset_first, P):
    B = x.shape[0]
    dt = x.dtype
    o_wkv, o_shift, o_h, o_lconv, o_fconv = [], [], [], [], []
    for i in range(DEPTH):
        mod = (jax.nn.silu(c) @ P['w_mod'][i] + P['b_mod'][i]).reshape(B, N_MOD, 1, D_MODEL)
        sh_m, sc_m, gt_m, sh_f, sc_f, gt_f = [mod[:, j] for j in range(N_MOD)]
        h = rms_norm(x, P['norm_g'][i, 0]) * (1 + sc_m) + sh_m
        if i % N_MIXERS == 0:
            j = i // N_MIXERS
            out, shift_new, S = rwkv7_time_mix(
                h, st_shift[j], st_wkv[j], P['rw_mix'][j], P['rw_wr'][j], P['rw_wk'][j], P['rw_wv'][j],
                P['rw_wo'][j], P['rw_w0'][j], P['rw_w1'][j], P['rw_w2'][j], P['rw_a0'][j], P['rw_a1'][j],
                P['rw_a2'][j], P['rw_g1'][j], P['rw_g2'][j], P['rw_kk'][j], P['rw_ka'][j], P['rw_rk'][j],
                P['rw_lnx_g'][j], P['rw_lnx_b'][j])
            o_wkv.append(S.astype(dt))
            o_shift.append(shift_new.astype(dt))
        else:
            j = i // N_MIXERS
            out, h_new, lbuf = rglru_block(
                h, st_lru_h[j], st_lru_conv[j], reset_first, P['lru_w_in'][j], P['lru_b_in'][j],
                P['lru_conv_w'][j], P['lru_conv_b'][j], P['lru_wa'][j], P['lru_ba'][j], P['lru_wx'][j],
                P['lru_bx'][j], P['lru_lambda'][j], P['lru_w_out'][j])
            o_h.append(h_new.astype(dt))
            o_lconv.append(lbuf.astype(dt))
        x = x + gt_m * rms_norm(out, P['norm_g'][i, 1])
        h = rms_norm(x, P['norm_g'][i, 2]) * (1 + sc_f) + sh_f
        f, fbuf = conv_ffn(h, st_ffn_conv[i], P['ffn_w_gate'][i], P['ffn_w_up'][i], P['ffn_w_down'][i],
                           P['ffn_conv_w'][i], P['ffn_conv_b'][i])
        o_fconv.append(fbuf.astype(dt))
        x = x + gt_f * rms_norm(f, P['norm_g'][i, 3])
    return x, jnp.stack(o_wkv), jnp.stack(o_shift), jnp.stack(o_h), jnp.stack(o_lconv), jnp.stack(o_fconv)


def setup_inputs(seed: int = 0) -> dict:
    key = jax.random.key(seed)
    ks = iter(jax.random.split(key, 64))
    f32 = jnp.float32
    nrm = lambda shape, s: jax.random.normal(next(ks), shape, f32) * s
    uni = lambda shape, lo, hi: jax.random.uniform(next(ks), shape, f32, lo, hi)
    D = D_MODEL
    n = jnp.arange(D, dtype=f32) / (D - 1)
    decay_speed = -7.0 + 5.0 * n ** 0.85
    u = uni((N_B, D_RNN), 0.9, 0.999)
    a_base = u ** (1.0 / LRU_C)
    return {
        'x_prompt': nrm((BATCH, SEQ, D), 1.0),
        'x_sample': nrm((DEC_BATCH, DEC_SEQ, D), 1.0),
        'c_prompt': nrm((BATCH, D), 1.0),
        'c_sample': nrm((DEC_BATCH, D), 1.0),
        'state_rwkv_wkv': nrm((N_A, DEC_BATCH, N_HEADS, HEAD_SIZE, HEAD_SIZE), 0.5),
        'state_rwkv_shift': nrm((N_A, DEC_BATCH, D), 1.0),
        'state_lru_h': nrm((N_B, DEC_BATCH, D_RNN), 1.0),
        'state_lru_conv': nrm((N_B, DEC_BATCH, LRU_CONV - 1, D_RNN), 1.0),
        'state_ffn_conv': nrm((DEPTH, DEC_BATCH, FFN_CONV - 1, D_FF), 1.0),
        'w_mod': nrm((DEPTH, D, N_MOD * D), 0.5 * D ** -0.5),
        'b_mod': nrm((DEPTH, N_MOD * D), 0.02),
        'norm_g': 1.0 + nrm((DEPTH, 4, D), 0.05),
        'rw_mix': uni((N_A, 6, D), 0.0, 1.0),
        'rw_wr': nrm((N_A, D, D), D ** -0.5),
        'rw_wk': nrm((N_A, D, D), D ** -0.5),
        'rw_wv': nrm((N_A, D, D), D ** -0.5),
        'rw_wo': nrm((N_A, D, D), D ** -0.5),
        'rw_w0': decay_speed + 0.5 + nrm((N_A, D), 0.1),
        'rw_w1': nrm((N_A, D, D_DECAY_LORA), D ** -0.5),
        'rw_w2': nrm((N_A, D_DECAY_LORA, D), 0.1 * D_DECAY_LORA ** -0.5),
        'rw_a0': nrm((N_A, D), 0.1),
        'rw_a1': nrm((N_A, D, D_AAA_LORA), D ** -0.5),
        'rw_a2': nrm((N_A, D_AAA_LORA, D), 0.5 * D_AAA_LORA ** -0.5),
        'rw_g1': nrm((N_A, D, D_GATE_LORA), D ** -0.5),
        'rw_g2': nrm((N_A, D_GATE_LORA, D), D_GATE_LORA ** -0.5),
        'rw_kk': 0.85 + nrm((N_A, D), 0.05),
        'rw_ka': 1.0 + nrm((N_A, D), 0.05),
        'rw_rk': nrm((N_A, N_HEADS, HEAD_SIZE), 0.1),
        'rw_lnx_g': 1.0 + nrm((N_A, D), 0.05),
        'rw_lnx_b': nrm((N_A, D), 0.02),
        'lru_w_in': nrm((N_B, D, 2 * D_RNN), D ** -0.5),
        'lru_b_in': nrm((N_B, 2 * D_RNN), 0.02),
        'lru_conv_w': nrm((N_B, LRU_CONV, D_RNN), LRU_CONV ** -0.5),
        'lru_conv_b': nrm((N_B, D_RNN), 0.02),
        'lru_wa': nrm((N_B, LRU_BLOCKS, LRU_BW, LRU_BW), LRU_BW ** -0.5),
        'lru_ba': nrm((N_B, D_RNN), 0.02),
        'lru_wx': nrm((N_B, LRU_BLOCKS, LRU_BW, LRU_BW), LRU_BW ** -0.5),
        'lru_bx': nrm((N_B, D_RNN), 0.02),
        'lru_lambda': jnp.log(a_base) - jnp.log1p(-a_base),
        'lru_w_out': nrm((N_B, D_RNN, D), D_RNN ** -0.5),
        'ffn_w_gate': nrm((DEPTH, D, D_FF), D ** -0.5),
        'ffn_w_up': nrm((DEPTH, D, D_FF), D ** -0.5),
        'ffn_w_down': nrm((DEPTH, D_FF, D), D_FF ** -0.5),
        'ffn_conv_w': nrm((DEPTH, FFN_CONV, D_FF), FFN_CONV ** -0.5),
        'ffn_conv_b': nrm((DEPTH, D_FF), 0.02),
    }


def reference(x_prompt, x_sample, c_prompt, c_sample, state_rwkv_wkv, state_rwkv_shift, state_lru_h,
              state_lru_conv, state_ffn_conv, w_mod, b_mod, norm_g, rw_mix, rw_wr, rw_wk, rw_wv, rw_wo,
              rw_w0, rw_w1, rw_w2, rw_a0, rw_a1, rw_a2, rw_g1, rw_g2, rw_kk, rw_ka, rw_rk, rw_lnx_g,
              rw_lnx_b, lru_w_in, lru_b_in, lru_conv_w, lru_conv_b, lru_wa, lru_ba, lru_wx, lru_bx,
              lru_lambda, lru_w_out, ffn_w_gate, ffn_w_up, ffn_w_down, ffn_conv_w, ffn_conv_b):
    P = dict(w_mod=w_mod, b_mod=b_mod, norm_g=norm_g, rw_mix=rw_mix, rw_wr=rw_wr, rw_wk=rw_wk,
             rw_wv=rw_wv, rw_wo=rw_wo, rw_w0=rw_w0, rw_w1=rw_w1, rw_w2=rw_w2, rw_a0=rw_a0, rw_a1=rw_a1,
             rw_a2=rw_a2, rw_g1=rw_g1, rw_g2=rw_g2, rw_kk=rw_kk, rw_ka=rw_ka, rw_rk=rw_rk,
             rw_lnx_g=rw_lnx_g, rw_lnx_b=rw_lnx_b, lru_w_in=lru_w_in, lru_b_in=lru_b_in,
             lru_conv_w=lru_conv_w, lru_conv_b=lru_conv_b, lru_wa=lru_wa, lru_ba=lru_ba, lru_wx=lru_wx,
             lru_bx=lru_bx, lru_lambda=lru_lambda, lru_w_out=lru_w_out, ffn_w_gate=ffn_w_gate,
             ffn_w_up=ffn_w_up, ffn_w_down=ffn_w_down, ffn_conv_w=ffn_conv_w, ffn_conv_b=ffn_conv_b)
    dt = x_prompt.dtype
    z_wkv = jnp.zeros((N_A, BATCH, N_HEADS, HEAD_SIZE, HEAD_SIZE), dt)
    z_shift = jnp.zeros((N_A, BATCH, D_MODEL), dt)
    z_h = jnp.zeros((N_B, BATCH, D_RNN), dt)
    z_lconv = jnp.zeros((N_B, BATCH, LRU_CONV - 1, D_RNN), dt)
    z_fconv = jnp.zeros((DEPTH, BATCH, FFN_CONV - 1, D_FF), dt)
    y_prompt, p_wkv, p_shift, p_h, p_lconv, p_fconv = trunk(
        x_prompt, c_prompt, z_wkv, z_shift, z_h, z_lconv, z_fconv, True, P)
    y_sample, s_wkv, s_shift, s_h, s_lconv, s_fconv = trunk(
        x_sample, c_sample, state_rwkv_wkv, state_rwkv_shift, state_lru_h, state_lru_conv,
        state_ffn_conv, False, P)
    return (y_prompt, y_sample, p_wkv, p_shift, p_h, p_lconv, p_fconv,
            s_wkv, s_shift, s_h, s_lconv, s_fconv)
```

```python
import functools

import jax
import jax.numpy as jnp
from jax import lax
from jax.experimental import pallas as pl
from jax.experimental.pallas import tpu as pltpu

F32 = jnp.float32
BF16 = jnp.bfloat16

NORM_EPS = 1e-6
LNX_EPS = 64e-5
LRU_C = 8.0
N_MOD = 6
HEAD = 64
LANE = 128
SUBLANE = 8
CHUNK = 64
PAIR_UNROLL = 4
VMEM_LIMIT = 52 * 1024 * 1024


def _cp(n_axes):
    return pltpu.CompilerParams(dimension_semantics=("arbitrary",) * n_axes,
                                vmem_limit_bytes=VMEM_LIMIT)


def _pick(n, target, mult):
    best = None
    for t in range(mult, min(n, target) + 1, mult):
        if n % t == 0:
            best = t
    return best if best is not None else n


def _softplus(z):
    return jnp.maximum(z, 0.0) + jnp.log(1.0 + jnp.exp(-jnp.abs(z)))


def _sigmoid(z):
    return 1.0 / (1.0 + jnp.exp(-z))


def _gelu(x):
    return 0.5 * x * (1.0 + jnp.tanh(0.7978845608028654 * (x + 0.044715 * (x * x * x))))


def _act(x, name):
    if name is None:
        return x
    if name == "tanh":
        return jnp.tanh(x)
    if name == "sigmoid":
        return _sigmoid(x)
    if name == "silu":
        return x * _sigmoid(x)
    raise ValueError(name)


def _rms(x, g):
    ms = jnp.mean(x * x, axis=-1, keepdims=True)
    return x * lax.rsqrt(ms + NORM_EPS) * g


def _dot(x, y):
    return jnp.dot(x, y, preferred_element_type=F32)


def _dot_nt(x, y):
    return lax.dot_general(x, y, (((1,), (1,)), ((), ())), preferred_element_type=F32)


def _dot_split(x, m, parts):
    acc = None
    rem = x
    for i in range(parts):
        piece = rem.astype(BF16)
        term = jnp.dot(piece, m, preferred_element_type=F32)
        acc = term if acc is None else acc + term
        if i + 1 < parts:
            rem = rem - piece.astype(F32)
    return acc


def _split_dot_left(m, x, parts):
    acc = None
    rem = x
    for i in range(parts):
        piece = rem.astype(BF16)
        term = jnp.dot(m, piece, preferred_element_type=F32)
        acc = term if acc is None else acc + term
        if i + 1 < parts:
            rem = rem - piece.astype(F32)
    return acc


def _dot_tn(x, y):
    return lax.dot_general(x, y, (((0,), (0,)), ((), ())), preferred_element_type=F32)


def _head_ones():
    r = lax.broadcasted_iota(jnp.int32, (LANE, LANE), 0)
    c = lax.broadcasted_iota(jnp.int32, (LANE, LANE), 1)
    return ((r < HEAD) == (c < HEAD)).astype(BF16)


def _head_sum(x, ones_bd):
    d = x.shape[-1]
    cols = [_dot_split(x[:, s:s + LANE], ones_bd, 2) for s in range(0, d, LANE)]
    return cols[0] if len(cols) == 1 else jnp.concatenate(cols, axis=-1)


def _shift_rows(x, carry8, s):
    n = x.shape[0]
    rolled = pltpu.roll(x, s, 0)
    rolled_c = pltpu.roll(carry8, s, 0)
    row8 = lax.broadcasted_iota(jnp.int32, (SUBLANE, x.shape[1]), 0)
    top = jnp.where(row8 < s, rolled_c, rolled[0:SUBLANE])
    if n == SUBLANE:
        return top
    return jnp.concatenate([top, rolled[SUBLANE:]], axis=0)


def _mm_kernel(*refs, has_bias, pre_act, act):
    if has_bias:
        x_ref, w_ref, b_ref, o_ref, wb_ref = refs
    else:
        x_ref, w_ref, o_ref, wb_ref = refs
        b_ref = None

    @pl.when(pl.program_id(1) == 0)
    def _():
        wb_ref[...] = w_ref[...].astype(BF16)

    x = x_ref[...]
    if pre_act is not None:
        x = _act(x.astype(F32), pre_act)
    acc = _dot(x.astype(BF16), wb_ref[...])
    if has_bias:
        acc = acc + b_ref[...]
    o_ref[...] = _act(acc, act).astype(o_ref.dtype)


def _mm(x, w, bias=None, *, pre_act=None, act=None, out_dtype=F32, tm_target=1024, tn_target=512, name="mm"):
    m, k = x.shape
    n = w.shape[1]
    tm = _pick(m, tm_target, 16)
    tn = _pick(n, tn_target, LANE)
    in_specs = [pl.BlockSpec((tm, k), lambda j, i: (i, 0)),
                pl.BlockSpec((k, tn), lambda j, i: (0, j))]
    args = [x, w]
    if bias is not None:
        in_specs.append(pl.BlockSpec((1, tn), lambda j, i: (0, j)))
        args.append(bias.reshape(1, n))
    return pl.pallas_call(
        functools.partial(_mm_kernel, has_bias=bias is not None, pre_act=pre_act, act=act),
        out_shape=jax.ShapeDtypeStruct((m, n), out_dtype),
        grid=(n // tn, m // tm),
        in_specs=in_specs,
        out_specs=pl.BlockSpec((tm, tn), lambda j, i: (i, j)),
        scratch_shapes=[pltpu.VMEM((k, tn), BF16)],
        compiler_params=_cp(2),
        name=name,
    )(*args)


def _lora_kernel(*refs, has_bias, mid_act, out_act):
    if has_bias:
        x_ref, w1_ref, w2_ref, b_ref, o_ref, w1b, w2b = refs
    else:
        x_ref, w1_ref, w2_ref, o_ref, w1b, w2b = refs
        b_ref = None

    @pl.when(pl.program_id(0) == 0)
    def _():
        w1b[...] = w1_ref[...].astype(BF16)
        w2b[...] = w2_ref[...].astype(BF16)

    mid = _act(_dot(x_ref[...], w1b[...]), mid_act)
    acc = _dot(mid.astype(BF16), w2b[...])
    if has_bias:
        acc = acc + b_ref[...]
    o_ref[...] = _act(acc, out_act)


def _lora(x, w1, w2, bias, *, mid_act, out_act, name):
    m, d = x.shape
    r = w1.shape[1]
    rp = -(-r // LANE) * LANE
    if rp != r:
        w1 = jnp.pad(w1, ((0, 0), (0, rp - r)))
        w2 = jnp.pad(w2, ((0, rp - r), (0, 0)))
    n = w2.shape[1]
    tm = _pick(m, 512, 16)
    in_specs = [pl.BlockSpec((tm, d), lambda i: (i, 0)),
                pl.BlockSpec((d, rp), lambda i: (0, 0)),
                pl.BlockSpec((rp, n), lambda i: (0, 0))]
    args = [x, w1, w2]
    if bias is not None:
        in_specs.append(pl.BlockSpec((1, n), lambda i: (0, 0)))
        args.append(bias.reshape(1, n))
    return pl.pallas_call(
        functools.partial(_lora_kernel, has_bias=bias is not None, mid_act=mid_act, out_act=out_act),
        out_shape=jax.ShapeDtypeStruct((m, n), F32),
        grid=(m // tm,),
        in_specs=in_specs,
        out_specs=pl.BlockSpec((tm, n), lambda i: (i, 0)),
        scratch_shapes=[pltpu.VMEM((d, rp), BF16), pltpu.VMEM((rp, n), BF16)],
        compiler_params=_cp(1),
        name=name,
    )(*args)


def _mod_spec(mod, slot, d, tr):
    rm = mod.shape[1]
    if rm == 1:
        return pl.BlockSpec((1, 1, d), lambda g, t: (g, 0, slot))
    return pl.BlockSpec((1, tr, d), lambda g, t: (g, t, slot))


def _resid_norm_kernel(*refs, with_next):
    if with_next:
        x_ref, o_ref, g1_ref, gate_ref, g2_ref, sc_ref, sh_ref, xn_ref, h_ref = refs
    else:
        x_ref, o_ref, g1_ref, gate_ref, xn_ref = refs
    xn = x_ref[0] + gate_ref[0] * _rms(o_ref[0], g1_ref[...])
    xn_ref[0] = xn
    if with_next:
        h_ref[0] = (_rms(xn, g2_ref[...]) * (1.0 + sc_ref[0]) + sh_ref[0]).astype(h_ref.dtype)


def _resid_norm(x, o, g1, mod, gate_slot, nxt=None, name="resid_norm"):
    g, r, d = x.shape
    tr = _pick(r, 256, 16)
    row = pl.BlockSpec((1, tr, d), lambda gi, t: (gi, t, 0))
    vec = pl.BlockSpec((1, d), lambda gi, t: (0, 0))
    in_specs = [row, row, vec, _mod_spec(mod, gate_slot, d, tr)]
    args = [x, o, g1.reshape(1, d), mod]
    out_shape = [jax.ShapeDtypeStruct((g, r, d), F32)]
    out_specs = [row]
    if nxt is not None:
        g2, mod2, sc_slot, sh_slot = nxt
        in_specs += [vec, _mod_spec(mod2, sc_slot, d, tr), _mod_spec(mod2, sh_slot, d, tr)]
        args += [g2.reshape(1, d), mod2, mod2]
        out_shape.append(jax.ShapeDtypeStruct((g, r, d), BF16))
        out_specs.append(row)
    res = pl.pallas_call(
        functools.partial(_resid_norm_kernel, with_next=nxt is not None),
        out_shape=out_shape,
        grid=(g, r // tr),
        in_specs=in_specs,
        out_specs=out_specs,
        compiler_params=_cp(2),
        name=name,
    )(*args)
    return res if nxt is not None else res[0]


def _rwkv_prep_kernel(x_ref, g_ref, sc_ref, sh_ref, mix_ref, shift_ref, *rest, seq):
    outs = rest[:6]
    shift_out = rest[6]
    h = _rms(x_ref[0], g_ref[...]) * (1.0 + sc_ref[0]) + sh_ref[0]
    tr = h.shape[0]
    if seq:
        carry = rest[7]
        t = pl.program_id(1)

        @pl.when(t == 0)
        def _():
            carry[...] = jnp.broadcast_to(shift_ref[0], carry.shape)

        hprev = _shift_rows(h, carry[...], 1)
        carry[...] = h[tr - SUBLANE:tr]

        @pl.when(t == pl.num_programs(1) - 1)
        def _():
            shift_out[0] = h[tr - 1:tr]
    else:
        hprev = shift_ref[0]
        shift_out[0] = h
    xx = hprev - h
    for j in range(6):
        outs[j][0] = (h + xx * mix_ref[j:j + 1, :]).astype(BF16)


def _rwkv_prep(x, g, mod, mix, shift, seq):
    gg, r, d = x.shape
    tr = _pick(r, 256, 16)
    row = pl.BlockSpec((1, tr, d), lambda gi, t: (gi, t, 0))
    if seq:
        shift_spec = pl.BlockSpec((1, 1, d), lambda gi, t: (gi, 0, 0))
        shift_shape = (gg, 1, d)
        scratch = [pltpu.VMEM((SUBLANE, d), F32)]
    else:
        shift_spec = row
        shift_shape = (gg, r, d)
        scratch = []
    outs = pl.pallas_call(
        functools.partial(_rwkv_prep_kernel, seq=seq),
        out_shape=[jax.ShapeDtypeStruct((gg, r, d), BF16)] * 6 + [jax.ShapeDtypeStruct(shift_shape, F32)],
        grid=(gg, r // tr),
        in_specs=[row, pl.BlockSpec((1, d), lambda gi, t: (0, 0)),
                  _mod_spec(mod, 1, d, tr), _mod_spec(mod, 0, d, tr),
                  pl.BlockSpec((6, d), lambda gi, t: (0, 0)), shift_spec],
        out_specs=[row] * 6 + [shift_spec],
        scratch_shapes=scratch,
        compiler_params=_cp(2),
        name="rwkv_prep",
    )(x, g.reshape(1, d), mod, mod, mix, shift)
    return outs[:6], outs[6]


def _wkv_terms(k, wpre, a, kkp, kap, ones_bd):
    logd = -jnp.exp(-_softplus(-wpre) - 0.5)
    kk = k * kkp
    nrm = jnp.sqrt(_head_sum(kk * kk, ones_bd))
    kk = kk / jnp.maximum(nrm, 1e-12)
    kmod = k * (1.0 + (a - 1.0) * kap)
    return logd, kk, kmod


def _wkv_chunk_kernel(r_ref, k_ref, v_ref, w_ref, a_ref, kkp_ref, kap_ref, y_ref, sout_ref,
                      S, Rt, At, Bt, Kt, Vs, Gc, Ys):
    c = pl.program_id(1)
    n_pairs = S.shape[0]
    C = r_ref.shape[1]

    @pl.when(c == 0)
    def _():
        S[...] = jnp.zeros(S.shape, F32)

    ones_bd = _head_ones()
    a = a_ref[0]
    logd, kk, kmod = _wkv_terms(k_ref[0], w_ref[0], a, kkp_ref[...], kap_ref[...], ones_bd)
    ri = lax.broadcasted_iota(jnp.int32, (C, C), 0)
    ci = lax.broadcasted_iota(jnp.int32, (C, C), 1)
    tri = (ri >= ci).astype(BF16)
    cum = _split_dot_left(tri, logd, 3)
    gam = jnp.exp(cum)
    rt = r_ref[0] * gam
    at = -kk * jnp.exp(cum - logd)
    ginv = jnp.exp(-cum)
    bt = kk * a * ginv
    kt = kmod * ginv
    v = v_ref[0]
    for p in range(n_pairs):
        sl = slice(p * LANE, (p + 1) * LANE)
        Rt[p] = rt[:, sl]
        At[p] = at[:, sl]
        Bt[p] = bt[:, sl]
        Kt[p] = kt[:, sl]
        Vs[p] = v[:, sl]
        Gc[p] = jnp.broadcast_to(gam[C - 1:C, sl], (SUBLANE, LANE))

    lane = lax.broadcasted_iota(jnp.int32, (1, LANE), 1)
    first = lane < HEAD
    r2 = lax.broadcasted_iota(jnp.int32, (2 * C, 2 * C), 0)
    c2 = lax.broadcasted_iota(jnp.int32, (2 * C, 2 * C), 1)
    strict = r2 > c2
    incl = r2 >= c2
    rs = lax.broadcasted_iota(jnp.int32, (LANE, LANE), 0)
    cs = lax.broadcasted_iota(jnp.int32, (LANE, LANE), 1)
    same_head = (rs < HEAD) == (cs < HEAD)

    def stack(x):
        return jnp.concatenate([jnp.where(first, x, 0.0), jnp.where(first, 0.0, x)], axis=0).astype(BF16)

    def pair_step(p):
        rp, ap, bp, kp, vp = Rt[p], At[p], Bt[p], Kt[p], Vs[p]
        am, rm, bm, km = stack(ap), stack(rp), stack(bp), stack(kp)
        lab = jnp.where(strict, _dot_nt(am, bm), 0.0)
        lak = jnp.where(strict, _dot_nt(am, km), 0.0).astype(BF16)
        grb = jnp.where(incl, _dot_nt(rm, bm), 0.0).astype(BF16)
        grk = jnp.where(incl, _dot_nt(rm, km), 0.0).astype(BF16)
        sp = S[p]
        spb = sp.astype(BF16)
        a_s = _dot_nt(ap.astype(BF16), spb)
        r_s = _dot_nt(rp.astype(BF16), spb)
        vst = jnp.concatenate([vp, vp], axis=0).astype(BF16)
        x = jnp.concatenate([a_s, a_s], axis=0) + _dot(lak, vst)
        lp = lab
        n_fac = max(1, (C - 1).bit_length())
        for i in range(n_fac):
            lb = lp.astype(BF16)
            x = x + _dot(lb, x.astype(BF16))
            if i + 1 < n_fac:
                lp = _dot(lb, lb)
        xb = x.astype(BF16)
        yst = jnp.concatenate([r_s, r_s], axis=0) + _dot(grb, xb) + _dot(grk, vst)
        Ys[p] = jnp.where(first, yst[:C], yst[C:])
        u = jnp.where(first, x[:C], x[C:])
        ds = _dot_tn(u.astype(BF16), bp.astype(BF16)) + _dot_tn(vp.astype(BF16), kp.astype(BF16))
        S[p] = jnp.where(same_head, (sp + ds) * Gc[p][0:1, :], 0.0)

    def group(gi, carry):
        for uu in range(PAIR_UNROLL):
            pair_step(gi * PAIR_UNROLL + uu)
        return carry

    if n_pairs % PAIR_UNROLL == 0:
        lax.fori_loop(0, n_pairs // PAIR_UNROLL, group, 0)
    else:
        for p in range(n_pairs):
            pair_step(p)

    for p in range(n_pairs):
        y_ref[0, :, p * LANE:(p + 1) * LANE] = Ys[p]

    @pl.when(c == pl.num_programs(1) - 1)
    def _():
        sout_ref[0] = S[...]


def _wkv_chunked(r, k, v, wpre, a, kkp, kap):
    g, t, d = r.shape
    C = CHUNK
    np_ = d // LANE
    row = pl.BlockSpec((1, C, d), lambda gi, ci: (gi, ci, 0))
    vec = pl.BlockSpec((1, d), lambda gi, ci: (0, 0))
    pair_buf = pltpu.VMEM((np_, C, LANE), F32)
    return pl.pallas_call(
        _wkv_chunk_kernel,
        out_shape=[jax.ShapeDtypeStruct((g, t, d), F32),
                   jax.ShapeDtypeStruct((g, np_, LANE, LANE), F32)],
        grid=(g, t // C),
        in_specs=[row] * 5 + [vec, vec],
        out_specs=[row, pl.BlockSpec((1, np_, LANE, LANE), lambda gi, ci: (gi, 0, 0, 0))],
        scratch_shapes=[pltpu.VMEM((np_, LANE, LANE), F32), pair_buf, pair_buf, pair_buf, pair_buf, pair_buf,
                        pltpu.VMEM((np_, SUBLANE, LANE), F32), pair_buf],
        compiler_params=_cp(2),
        name="wkv_chunk",
    )(r, k, v, wpre, a, kkp.reshape(1, d), kap.reshape(1, d))


def _wkv_vec_kernel(k_ref, w_ref, a_ref, kkp_ref, kap_ref, d_ref, an_ref, b_ref, km_ref):
    a = a_ref[...]
    logd, kk, kmod = _wkv_terms(k_ref[...], w_ref[...], a, kkp_ref[...], kap_ref[...], _head_ones())
    d_ref[...] = jnp.exp(logd)
    an_ref[...] = -kk
    b_ref[...] = kk * a
    km_ref[...] = kmod


def _wkv_vec(k, wpre, a, kkp, kap):
    b, d = k.shape
    full = pl.BlockSpec((b, d), lambda i: (0, 0))
    vec = pl.BlockSpec((1, d), lambda i: (0, 0))
    return pl.pallas_call(
        _wkv_vec_kernel,
        out_shape=[jax.ShapeDtypeStruct((b, d), F32)] * 4,
        grid=(1,),
        in_specs=[full, full, full, vec, vec],
        out_specs=[full] * 4,
        compiler_params=_cp(1),
        name="wkv_vec",
    )(k, wpre, a, kkp.reshape(1, d), kap.reshape(1, d))


def _wkv_step_kernel(s_ref, d_ref, a_ref, b_ref, k_ref, r_ref, v_ref, sn_ref, y_ref):
    bb, h, half, _ = s_ref.shape
    rows = bb * h * half
    ones_bd = _head_ones()
    mi = lax.broadcasted_iota(jnp.int32, (HEAD, LANE), 0)
    li = lax.broadcasted_iota(jnp.int32, (HEAD, LANE), 1)
    spread = (((mi & 1) == 1) == (li >= HEAD)).astype(BF16)
    qj = lax.broadcasted_iota(jnp.int32, (half, HEAD), 0)
    qm = lax.broadcasted_iota(jnp.int32, (half, HEAD), 1)
    qmask = (lax.shift_right_logical(qm, 1) == qj).astype(F32)
    ej = lax.broadcasted_iota(jnp.int32, (half, LANE), 0)
    el = lax.broadcasted_iota(jnp.int32, (half, LANE), 1)
    home = jnp.where(el < HEAD, 2 * ej, 2 * ej + 1 + HEAD)
    emask = (el == home).astype(F32)
    fr = lax.broadcasted_iota(jnp.int32, (LANE, LANE), 0)
    fc = lax.broadcasted_iota(jnp.int32, (LANE, LANE), 1)
    fold = ((fr & (HEAD - 1)) == (fc & (HEAD - 1))).astype(BF16)

    def bsum(x4):
        return _dot_split(x4.reshape(rows, LANE), ones_bd, 2).reshape(bb, h, half, LANE)

    s = s_ref[...]
    sa = bsum(s * a_ref[...])
    vq = v_ref[...] * qmask
    v2 = _dot_split(vq.reshape(rows, HEAD), spread, 2).reshape(bb, h, half, LANE)
    sn = s * d_ref[...] + sa * b_ref[...] + v2 * k_ref[...]
    sn_ref[...] = sn
    y2 = bsum(sn * r_ref[...])
    yh = y2 * emask
    yf = _dot_split(yh.reshape(rows, LANE), fold, 2).reshape(bb, h, half, LANE)
    y_ref[...] = jnp.sum(yf, axis=2, keepdims=True)


def _wkv_step(s, dvec, an, bvec, kmod, r, v):
    b, h, n, _ = s.shape
    half = n // 2
    s2 = s.reshape(b, h, half, LANE)

    def dup(x):
        x = x.reshape(b, h, 1, n)
        return jnp.concatenate([x, x], axis=-1)

    bb = _pick(b, 4, 1)
    sblk = pl.BlockSpec((bb, h, half, LANE), lambda i: (i, 0, 0, 0))
    vblk = pl.BlockSpec((bb, h, 1, LANE), lambda i: (i, 0, 0, 0))
    sn, y = pl.pallas_call(
        _wkv_step_kernel,
        out_shape=[jax.ShapeDtypeStruct((b, h, half, LANE), F32),
                   jax.ShapeDtypeStruct((b, h, 1, LANE), F32)],
        grid=(b // bb,),
        in_specs=[sblk] + [vblk] * 5 + [pl.BlockSpec((bb, h, 1, n), lambda i: (i, 0, 0, 0))],
        out_specs=[sblk, vblk],
        compiler_params=_cp(1),
        name="wkv_step",
    )(s2, dup(dvec), dup(an), dup(bvec), dup(kmod), dup(r), v.reshape(b, h, 1, n))
    return sn.reshape(b, h, n, n), y[..., :n].reshape(b, h * n)


def _wkv_post_kernel(y_ref, r_ref, k_ref, v_ref, a_ref, g_ref, kap_ref, rk_ref, lg_ref, lb_ref, o_ref):
    ones_bd = _head_ones()
    y = y_ref[0]
    inv_n = 1.0 / HEAD
    mu = _head_sum(y, ones_bd) * inv_n
    yc = y - mu
    var = _head_sum(yc * yc, ones_bd) * inv_n
    yn = yc * lax.rsqrt(var + LNX_EPS) * lg_ref[...] + lb_ref[...]
    kmod = k_ref[0] * (1.0 + (a_ref[0] - 1.0) * kap_ref[...])
    bonus = _head_sum(r_ref[0] * kmod * rk_ref[...], ones_bd) * v_ref[0]
    o_ref[0] = ((yn + bonus) * g_ref[0]).astype(BF16)


def _wkv_post(y, r, k, v, a, g, kap, rk, lnx_g, lnx_b):
    gg, t, d = y.shape
    tr = _pick(t, 256, 16)
    row = pl.BlockSpec((1, tr, d), lambda gi, ti: (gi, ti, 0))
    vec = pl.BlockSpec((1, d), lambda gi, ti: (0, 0))
    return pl.pallas_call(
        _wkv_post_kernel,
        out_shape=jax.ShapeDtypeStruct((gg, t, d), BF16),
        grid=(gg, t // tr),
        in_specs=[row] * 6 + [vec] * 4,
        out_specs=row,
        compiler_params=_cp(2),
        name="wkv_post",
    )(y, r, k, v, a, g, kap.reshape(1, d), rk.reshape(1, d), lnx_g.reshape(1, d), lnx_b.reshape(1, d))


def _lru_conv_kernel(x_ref, w_ref, b_ref, buf_ref, xc_ref, *rest, seq, width):
    x = x_ref[0]
    tr = x.shape[0]
    acc = b_ref[...] + x * w_ref[width - 1:width, :]
    if seq:
        nbuf_ref, carry = rest
        t = pl.program_id(1)

        @pl.when(t == 0)
        def _():
            carry[...] = buf_ref[0]

        c8 = carry[...]
        for s in range(1, width):
            acc = acc + _shift_rows(x, c8, s) * w_ref[width - 1 - s:width - s, :]
        carry[...] = x[tr - SUBLANE:tr]
        nbuf_ref[0] = x[tr - SUBLANE:tr]
    else:
        for s in range(1, width):
            acc = acc + buf_ref[0, :, width - 1 - s, :] * w_ref[width - 1 - s:width - s, :]
    xc_ref[0] = acc


def _lru_conv(proj, w, b, buf, seq):
    g, r, d2 = proj.shape
    d = d2 // 2
    width = w.shape[0]
    tr = _pick(r, 256, 16)
    row = pl.BlockSpec((1, tr, d), lambda gi, t: (gi, t, 0))
    in_specs = [pl.BlockSpec((1, tr, d), lambda gi, t: (gi, t, 1)),
                pl.BlockSpec((width, d), lambda gi, t: (0, 0)),
                pl.BlockSpec((1, d), lambda gi, t: (0, 0))]
    if seq:
        in_specs.append(pl.BlockSpec((1, SUBLANE, d), lambda gi, t: (gi, 0, 0)))
        out_shape = [jax.ShapeDtypeStruct((g, r, d), F32), jax.ShapeDtypeStruct((g, SUBLANE, d), F32)]
        out_specs = [row, pl.BlockSpec((1, SUBLANE, d), lambda gi, t: (gi, 0, 0))]
        scratch = [pltpu.VMEM((SUBLANE, d), F32)]
    else:
        in_specs.append(pl.BlockSpec((1, tr, width - 1, d), lambda gi, t: (gi, t, 0, 0)))
        out_shape = [jax.ShapeDtypeStruct((g, r, d), F32)]
        out_specs = [row]
        scratch = []
    return pl.pallas_call(
        functools.partial(_lru_conv_kernel, seq=seq, width=width),
        out_shape=out_shape,
        grid=(g, r // tr),
        in_specs=in_specs,
        out_specs=out_specs,
        scratch_shapes=scratch,
        compiler_params=_cp(2),
        name="lru_conv",
    )(proj, w, b.reshape(1, d), buf)


def _lru_gates_kernel(x_ref, wa_ref, ba_ref, wx_ref, bx_ref, ga_ref, gx_ref):
    x = x_ref[...].astype(BF16)
    ga_ref[...] = _sigmoid(_dot(x, wa_ref[0].astype(BF16)) + ba_ref[...])
    gx_ref[...] = _sigmoid(_dot(x, wx_ref[0].astype(BF16)) + bx_ref[...])


def _lru_gates(xc, wa, ba, wx, bx):
    m, d = xc.shape
    nb, bw, _ = wa.shape
    tm = _pick(m, 1024, 8)
    xs = pl.BlockSpec((tm, bw), lambda n, i: (i, n))
    ws = pl.BlockSpec((1, bw, bw), lambda n, i: (n, 0, 0))
    bs = pl.BlockSpec((1, bw), lambda n, i: (0, n))
    return pl.pallas_call(
        _lru_gates_kernel,
        out_shape=[jax.ShapeDtypeStruct((m, d), F32)] * 2,
        grid=(nb, m // tm),
        in_specs=[xs, ws, bs, ws, bs],
        out_specs=[xs, xs],
        compiler_params=_cp(2),
        name="lru_gates",
    )(xc, wa, ba.reshape(1, d), wx, bx.reshape(1, d))


def _lru_coeffs(ga, gx, xc, lam):
    log_a = -LRU_C * ga * _softplus(-lam)
    a = jnp.exp(log_a)
    mult = jnp.sqrt(1.0 - jnp.exp(2.0 * log_a))
    return a, mult


def _lru_scan_kernel(ga_ref, gx_ref, xc_ref, y_ref, lam_ref, h0_ref, o_ref, hl_ref, carry, abuf, bbuf, hbuf):
    t = pl.program_id(2)
    tr = ga_ref.shape[1]

    @pl.when(t == 0)
    def _():
        carry[...] = jnp.broadcast_to(h0_ref[0], carry.shape)

    xc = xc_ref[0]
    a, mult = _lru_coeffs(ga_ref[0], gx_ref[0], xc, lam_ref[...])
    row = lax.broadcasted_iota(jnp.int32, a.shape, 0)
    mult = jnp.where(jnp.logical_and(t == 0, row == 0), 1.0, mult)
    abuf[...] = a
    bbuf[...] = mult * gx_ref[0] * xc
    r8 = lax.broadcasted_iota(jnp.int32, (SUBLANE, a.shape[1]), 0)

    def body(i, c):
        off = pl.multiple_of(i * SUBLANE, SUBLANE)
        aa = abuf[pl.ds(off, SUBLANE), :]
        bb = bbuf[pl.ds(off, SUBLANE), :]
        for s in (1, 2, 4):
            a_s = jnp.where(r8 >= s, pltpu.roll(aa, s, 0), 1.0)
            b_s = jnp.where(r8 >= s, pltpu.roll(bb, s, 0), 0.0)
            bb = aa * b_s + bb
            aa = aa * a_s
        h = aa * c + bb
        hbuf[pl.ds(off, SUBLANE), :] = h
        return h[SUBLANE - 1:SUBLANE, :]

    c = lax.fori_loop(0, tr // SUBLANE, body, carry[0:1, :])
    carry[...] = jnp.broadcast_to(c, carry.shape)
    o_ref[0] = (hbuf[...] * _gelu(y_ref[0])).astype(BF16)

    @pl.when(t == pl.num_programs(2) - 1)
    def _():
        hl_ref[0] = c


def _lru_scan(ga, gx, xc, proj, lam, h0):
    g, r, d = ga.shape
    tr = _pick(r, 256, 16)
    td = _pick(d, 512, LANE)
    row = pl.BlockSpec((1, tr, td), lambda gi, di, t: (gi, t, di))
    vec = pl.BlockSpec((1, td), lambda gi, di, t: (0, di))
    st = pl.BlockSpec((1, 1, td), lambda gi, di, t: (gi, 0, di))
    tile = pltpu.VMEM((tr, td), F32)
    return pl.pallas_call(
        _lru_scan_kernel,
        out_shape=[jax.ShapeDtypeStruct((g, r, d), BF16), jax.ShapeDtypeStruct((g, 1, d), F32)],
        grid=(g, d // td, r // tr),
        in_specs=[row, row, row, row, vec, st],
        out_specs=[row, st],
        scratch_shapes=[pltpu.VMEM((SUBLANE, td), F32), tile, tile, tile],
        compiler_params=_cp(3),
        name="lru_scan",
    )(ga, gx, xc, proj, lam.reshape(1, d), h0)


def _lru_step_kernel(ga_ref, gx_ref, xc_ref, y_ref, lam_ref, h0_ref, o_ref, h_ref):
    xc = xc_ref[0]
    a, mult = _lru_coeffs(ga_ref[0], gx_ref[0], xc, lam_ref[...])
    h = a * h0_ref[0] + mult * gx_ref[0] * xc
    h_ref[0] = h
    o_ref[0] = (h * _gelu(y_ref[0])).astype(BF16)


def _lru_step(ga, gx, xc, proj, lam, h0):
    g, r, d = ga.shape
    row = pl.BlockSpec((1, r, d), lambda i: (0, 0, 0))
    return pl.pallas_call(
        _lru_step_kernel,
        out_shape=[jax.ShapeDtypeStruct((g, r, d), BF16), jax.ShapeDtypeStruct((g, r, d), F32)],
        grid=(1,),
        in_specs=[row, row, row, row, pl.BlockSpec((1, d), lambda i: (0, 0)), row],
        out_specs=[row, row],
        compiler_params=_cp(1),
        name="lru_step",
    )(ga, gx, xc, proj, lam.reshape(1, d), h0)


def _ffn_up_kernel(*refs, seq, tiles_per_seq):
    if seq:
        x_ref, wg_ref, wu_ref, cw_ref, cb_ref, buf_ref, o_ref, nb_ref, wgb, wub, carry = refs
    else:
        x_ref, wg_ref, wu_ref, cw_ref, cb_ref, p2_ref, p1_ref, o_ref, gt_ref, wgb, wub = refs
    i = pl.program_id(1)

    @pl.when(i == 0)
    def _():
        wgb[...] = wg_ref[...].astype(BF16)
        wub[...] = wu_ref[...].astype(BF16)

    x = x_ref[...]
    gt = _dot(x, wgb[...])
    u = _dot(x, wub[...])
    tm = gt.shape[0]
    if seq:
        @pl.when(i % tiles_per_seq == 0)
        def _():
            carry[...] = buf_ref[0]

        c8 = carry[...]
        p1 = _shift_rows(gt, c8, 1)
        p2 = _shift_rows(gt, c8, 2)
        carry[...] = gt[tm - SUBLANE:tm]
        nb_ref[0] = gt[tm - SUBLANE:tm]
    else:
        p1 = p1_ref[...]
        p2 = p2_ref[...]
        gt_ref[...] = gt
    gc = cb_ref[...] + p2 * cw_ref[0:1, :] + p1 * cw_ref[1:2, :] + gt * cw_ref[2:3, :]
    o_ref[...] = (_gelu(gc) * u).astype(BF16)


def _ffn_up(x, wg, wu, cw, cb, buf, seq, rows_per_seq):
    m, k = x.shape
    f = wg.shape[1]
    tn = _pick(f, 512, LANE)
    tm = _pick(rows_per_seq if seq else m, 512, 16)
    nj = f // tn
    xs = pl.BlockSpec((tm, k), lambda j, i: (i, 0))
    ws = pl.BlockSpec((k, tn), lambda j, i: (0, j))
    os_ = pl.BlockSpec((tm, tn), lambda j, i: (i, j))
    in_specs = [xs, ws, ws, pl.BlockSpec((3, tn), lambda j, i: (0, j)), pl.BlockSpec((1, tn), lambda j, i: (0, j))]
    scratch = [pltpu.VMEM((k, tn), BF16), pltpu.VMEM((k, tn), BF16)]
    if seq:
        tps = rows_per_seq // tm
        g = m // rows_per_seq
        bspec = pl.BlockSpec((1, SUBLANE, tn), lambda j, i: (i // tps, 0, j))
        in_specs.append(bspec)
        out_shape = [jax.ShapeDtypeStruct((m, f), BF16), jax.ShapeDtypeStruct((g, SUBLANE, f), F32)]
        out_specs = [os_, bspec]
        scratch.append(pltpu.VMEM((SUBLANE, tn), F32))
    else:
        tps = 1
        in_specs += [pl.BlockSpec((tm, tn), lambda j, i: (i, j)), pl.BlockSpec((tm, tn), lambda j, i: (i, nj + j))]
        out_shape = [jax.ShapeDtypeStruct((m, f), BF16), jax.ShapeDtypeStruct((m, f), F32)]
        out_specs = [os_, os_]
    args = [x, wg, wu, cw, cb.reshape(1, f), buf] + ([] if seq else [buf])
    return pl.pallas_call(
        functools.partial(_ffn_up_kernel, seq=seq, tiles_per_seq=tps),
        out_shape=out_shape,
        grid=(nj, m // tm),
        in_specs=in_specs,
        out_specs=out_specs,
        scratch_shapes=scratch,
        compiler_params=_cp(2),
        name="ffn_up",
    )(*args)


def _pad_hist(buf):
    return jnp.pad(buf, ((0, 0), (SUBLANE - buf.shape[1], 0), (0, 0)))


def _trunk(x, mods, st, seq, P):
    g, r, d = x.shape
    m = g * r
    depth = P["w_mod"].shape[0]
    o_wkv, o_shift, o_h, o_lconv, o_fconv = [], [], [], [], []
    flat = lambda t: t.reshape(m, t.shape[-1])
    unflat = lambda t: t.reshape(g, r, t.shape[-1])
    h_in = None
    for i in range(depth):
        mod = mods[i]
        j = i // 2
        if i % 2 == 0:
            if seq:
                shift = jnp.zeros((g, 1, d), F32)
            else:
                shift = st["shift"][j].reshape(1, r, d)
            mixes, shift_new = _rwkv_prep(x, P["norm_g"][i, 0], mod, P["rw_mix"][j], shift, seq)
            xr, xw, xk, xv, xa, xg = [flat(t) for t in mixes]
            rr = _mm(xr, P["rw_wr"][j], name="rw_r")
            kk = _mm(xk, P["rw_wk"][j], name="rw_k")
            vv = _mm(xv, P["rw_wv"][j], name="rw_v")
            wpre = _lora(xw, P["rw_w1"][j], P["rw_w2"][j], P["rw_w0"][j], mid_act="tanh", out_act=None, name="rw_w")
            aa = _lora(xa, P["rw_a1"][j], P["rw_a2"][j], P["rw_a0"][j], mid_act=None, out_act="sigmoid", name="rw_a")
            gg = _lora(xg, P["rw_g1"][j], P["rw_g2"][j], None, mid_act="sigmoid", out_act=None, name="rw_g")
            if seq:
                y, s_bd = _wkv_chunked(unflat(rr), unflat(kk), unflat(vv), unflat(wpre), unflat(aa),
                                       P["rw_kk"][j], P["rw_ka"][j])
                npairs = d // LANE
                s_new = jnp.stack([s_bd[:, :, :HEAD, :HEAD], s_bd[:, :, HEAD:, HEAD:]], axis=2)
                s_new = s_new.reshape(g, 2 * npairs, HEAD, HEAD)
                o_shift.append(shift_new.reshape(g, d))
            else:
                dvec, an, bvec, kmod = _wkv_vec(kk, wpre, aa, P["rw_kk"][j], P["rw_ka"][j])
                s_new, y = _wkv_step(st["wkv"][j], dvec, an, bvec, kmod, rr, vv)
                y = y.reshape(1, r, d)
                o_shift.append(shift_new.reshape(r, d))
            o_wkv.append(s_new)
            xo = _wkv_post(y, unflat(rr), unflat(kk), unflat(vv), unflat(aa), unflat(gg),
                           P["rw_ka"][j], P["rw_rk"][j], P["rw_lnx_g"][j], P["rw_lnx_b"][j])
            out = _mm(flat(xo), P["rw_wo"][j], name="rw_o")
        else:
            proj = _mm(flat(h_in), P["lru_w_in"][j], P["lru_b_in"][j], name="lru_in")
            dr = proj.shape[1] // 2
            width = P["lru_conv_w"].shape[1]
            if seq:
                xc, nb8 = _lru_conv(unflat(proj), P["lru_conv_w"][j], P["lru_conv_b"][j],
                                    jnp.zeros((g, SUBLANE, dr), F32), True)
                o_lconv.append(nb8[:, SUBLANE - (width - 1):, :])
            else:
                hist = st["lru_conv"][j]
                (xc,) = _lru_conv(unflat(proj), P["lru_conv_w"][j], P["lru_conv_b"][j],
                                  hist.reshape(1, r, width - 1, dr), False)
                o_lconv.append(jnp.concatenate([hist[:, 1:], proj[:, None, dr:]], axis=1))
            ga, gx = _lru_gates(flat(xc), P["lru_wa"][j], P["lru_ba"][j], P["lru_wx"][j], P["lru_bx"][j])
            if seq:
                hy, hl = _lru_scan(unflat(ga), unflat(gx), xc, unflat(proj), P["lru_lambda"][j],
                                   jnp.zeros((g, 1, dr), F32))
                o_h.append(hl.reshape(g, dr))
            else:
                hy, hl = _lru_step(unflat(ga), unflat(gx), xc, unflat(proj), P["lru_lambda"][j],
                                   st["lru_h"][j].reshape(1, r, dr))
                o_h.append(hl.reshape(r, dr))
            out = _mm(flat(hy), P["lru_w_out"][j], name="lru_out")
        x, hf = _resid_norm(x, unflat(out), P["norm_g"][i, 1], mod, 2, (P["norm_g"][i, 2], mod, 4, 3),
                            name="resid_mix")
        fdim = P["ffn_w_gate"].shape[2]
        if seq:
            hact, nb8 = _ffn_up(flat(hf), P["ffn_w_gate"][i], P["ffn_w_up"][i], P["ffn_conv_w"][i],
                                P["ffn_conv_b"][i], jnp.zeros((g, SUBLANE, fdim), F32), True, r)
            o_fconv.append(nb8[:, SUBLANE - 2:, :])
        else:
            hist = st["ffn_conv"][i]
            hact, gt = _ffn_up(flat(hf), P["ffn_w_gate"][i], P["ffn_w_up"][i], P["ffn_conv_w"][i],
                               P["ffn_conv_b"][i], hist.reshape(r, 2 * fdim), False, r)
            o_fconv.append(jnp.stack([hist[:, 1], gt], axis=1))
        f = _mm(hact, P["ffn_w_down"][i], tm_target=512, name="ffn_down")
        if i + 1 < depth:
            nmod = mods[i + 1]
            if (i + 1) % 2 == 0:
                x = _resid_norm(x, unflat(f), P["norm_g"][i, 3], mod, 5, name="resid_ffn")
            else:
                x, h_in = _resid_norm(x, unflat(f), P["norm_g"][i, 3], mod, 5,
                                      (P["norm_g"][i + 1, 0], nmod, 1, 0), name="resid_ffn")
        else:
            x = _resid_norm(x, unflat(f), P["norm_g"][i, 3], mod, 5, name="resid_ffn")
    return x, jnp.stack(o_wkv), jnp.stack(o_shift), jnp.stack(o_h), jnp.stack(o_lconv), jnp.stack(o_fconv)


def kernel(x_prompt, x_sample, c_prompt, c_sample, state_rwkv_wkv, state_rwkv_shift, state_lru_h,
           state_lru_conv, state_ffn_conv, w_mod, b_mod, norm_g, rw_mix, rw_wr, rw_wk, rw_wv, rw_wo,
           rw_w0, rw_w1, rw_w2, rw_a0, rw_a1, rw_a2, rw_g1, rw_g2, rw_kk, rw_ka, rw_rk, rw_lnx_g,
           rw_lnx_b, lru_w_in, lru_b_in, lru_conv_w, lru_conv_b, lru_wa, lru_ba, lru_wx, lru_bx,
           lru_lambda, lru_w_out, ffn_w_gate, ffn_w_up, ffn_w_down, ffn_conv_w, ffn_conv_b):
    P = dict(w_mod=w_mod, b_mod=b_mod, norm_g=norm_g, rw_mix=rw_mix, rw_wr=rw_wr, rw_wk=rw_wk,
             rw_wv=rw_wv, rw_wo=rw_wo, rw_w0=rw_w0, rw_w1=rw_w1, rw_w2=rw_w2, rw_a0=rw_a0, rw_a1=rw_a1,
             rw_a2=rw_a2, rw_g1=rw_g1, rw_g2=rw_g2, rw_kk=rw_kk, rw_ka=rw_ka, rw_rk=rw_rk,
             rw_lnx_g=rw_lnx_g, rw_lnx_b=rw_lnx_b, lru_w_in=lru_w_in, lru_b_in=lru_b_in,
             lru_conv_w=lru_conv_w, lru_conv_b=lru_conv_b, lru_wa=lru_wa, lru_ba=lru_ba, lru_wx=lru_wx,
             lru_bx=lru_bx, lru_lambda=lru_lambda, lru_w_out=lru_w_out, ffn_w_gate=ffn_w_gate,
             ffn_w_up=ffn_w_up, ffn_w_down=ffn_w_down, ffn_conv_w=ffn_conv_w, ffn_conv_b=ffn_conv_b)
    bp, t, d = x_prompt.shape
    bs = x_sample.shape[0]
    depth = w_mod.shape[0]
    c_all = jnp.concatenate([c_prompt, c_sample], axis=0)
    rows = c_all.shape[0]
    rows_p = -(-rows // 16) * 16
    c_all = jnp.pad(c_all, ((0, rows_p - rows), (0, 0)))
    mods_p, mods_s = [], []
    for i in range(depth):
        mod = _mm(c_all, w_mod[i], b_mod[i], pre_act="silu", name="mod")
        mods_p.append(mod[:bp].reshape(bp, 1, N_MOD * d))
        mods_s.append(mod[bp:bp + bs].reshape(1, bs, N_MOD * d))
    y_p, p_wkv, p_shift, p_h, p_lconv, p_fconv = _trunk(x_prompt, mods_p, None, True, P)
    st = dict(wkv=state_rwkv_wkv, shift=state_rwkv_shift, lru_h=state_lru_h,
              lru_conv=state_lru_conv, ffn_conv=state_ffn_conv)
    y_s, s_wkv, s_shift, s_h, s_lconv, s_fconv = _trunk(x_sample.reshape(1, bs, d), mods_s, st, False, P)
    return (y_p, y_s.reshape(bs, 1, d), p_wkv, p_shift, p_h, p_lconv, p_fconv,
            s_wkv, s_shift, s_h, s_lconv, s_fconv)
```

```python
import functools

import jax
import jax.numpy as jnp
from jax import lax
from jax.experimental import pallas as pl
from jax.experimental.pallas import tpu as pltpu

F32 = jnp.float32
BF16 = jnp.bfloat16

NORM_EPS = 1e-6
LNX_EPS = 64e-5
LRU_C = 8.0
N_MOD = 6
HEAD = 64
LANE = 128
SUBLANE = 8
CHUNK = 64
PAIR_UNROLL = 16
VMEM_LIMIT = 52 * 1024 * 1024


def _cp(n_axes):
    return pltpu.CompilerParams(dimension_semantics=("arbitrary",) * n_axes,
                                vmem_limit_bytes=VMEM_LIMIT)


def _pick(n, target, mult):
    best = None
    for t in range(mult, min(n, target) + 1, mult):
        if n % t == 0:
            best = t
    return best if best is not None else n


def _softplus(z):
    return jnp.maximum(z, 0.0) + jnp.log(1.0 + jnp.exp(-jnp.abs(z)))


def _sigmoid(z):
    return 1.0 / (1.0 + jnp.exp(-z))


def _gelu(x):
    return 0.5 * x * (1.0 + jnp.tanh(0.7978845608028654 * (x + 0.044715 * (x * x * x))))


def _act(x, name):
    if name is None:
        return x
    if name == "tanh":
        return jnp.tanh(x)
    if name == "sigmoid":
        return _sigmoid(x)
    if name == "silu":
        return x * _sigmoid(x)
    raise ValueError(name)


def _rms(x, g):
    ms = jnp.mean(x * x, axis=-1, keepdims=True)
    return x * lax.rsqrt(ms + NORM_EPS) * g


def _dot(x, y):
    return jnp.dot(x, y, preferred_element_type=F32)


def _dot_nt(x, y):
    return lax.dot_general(x, y, (((1,), (1,)), ((), ())), preferred_element_type=F32)


def _dot_split(x, m, parts):
    acc = None
    rem = x
    for i in range(parts):
        piece = rem.astype(BF16)
        term = jnp.dot(piece, m, preferred_element_type=F32)
        acc = term if acc is None else acc + term
        if i + 1 < parts:
            rem = rem - piece.astype(F32)
    return acc


def _split_dot_left(m, x, parts):
    acc = None
    rem = x
    for i in range(parts):
        piece = rem.astype(BF16)
        term = jnp.dot(m, piece, preferred_element_type=F32)
        acc = term if acc is None else acc + term
        if i + 1 < parts:
            rem = rem - piece.astype(F32)
    return acc


def _dot_tn(x, y):
    return lax.dot_general(x, y, (((0,), (0,)), ((), ())), preferred_element_type=F32)


def _head_ones():
    r = lax.broadcasted_iota(jnp.int32, (LANE, LANE), 0)
    c = lax.broadcasted_iota(jnp.int32, (LANE, LANE), 1)
    return ((r < HEAD) == (c < HEAD)).astype(BF16)


def _head_sum(x, ones_bd):
    d = x.shape[-1]
    cols = [_dot_split(x[:, s:s + LANE], ones_bd, 2) for s in range(0, d, LANE)]
    return cols[0] if len(cols) == 1 else jnp.concatenate(cols, axis=-1)


def _shift_rows(x, carry8, s):
    n = x.shape[0]
    rolled = pltpu.roll(x, s, 0)
    rolled_c = pltpu.roll(carry8, s, 0)
    row8 = lax.broadcasted_iota(jnp.int32, (SUBLANE, x.shape[1]), 0)
    top = jnp.where(row8 < s, rolled_c, rolled[0:SUBLANE])
    if n == SUBLANE:
        return top
    return jnp.concatenate([top, rolled[SUBLANE:]], axis=0)


def _mm_kernel(*refs, has_bias, pre_act, act):
    if has_bias:
        x_ref, w_ref, b_ref, o_ref, wb_ref = refs
    else:
        x_ref, w_ref, o_ref, wb_ref = refs
        b_ref = None

    @pl.when(pl.program_id(1) == 0)
    def _():
        wb_ref[...] = w_ref[...].astype(BF16)

    x = x_ref[...]
    if pre_act is not None:
        x = _act(x.astype(F32), pre_act)
    acc = _dot(x.astype(BF16), wb_ref[...])
    if has_bias:
        acc = acc + b_ref[...]
    o_ref[...] = _act(acc, act).astype(o_ref.dtype)


def _wspec(w, layer, rows, cols, col_of):
    assert w.ndim == 3
    return pl.BlockSpec((None, rows, cols), lambda *ids: (layer, 0, col_of(*ids)))


def _mm(x, w, layer, bias=None, *, pre_act=None, act=None, out_dtype=F32, tm_target=1024, tn_target=512,
        name="mm"):
    m, k = x.shape
    n = w.shape[2]
    tm = _pick(m, tm_target, 16)
    tn = _pick(n, tn_target, LANE)
    in_specs = [pl.BlockSpec((tm, k), lambda j, i: (i, 0)),
                _wspec(w, layer, k, tn, lambda j, i: j)]
    args = [x, w]
    if bias is not None:
        in_specs.append(pl.BlockSpec((1, tn), lambda j, i: (0, j)))
        args.append(bias.reshape(1, n))
    return pl.pallas_call(
        functools.partial(_mm_kernel, has_bias=bias is not None, pre_act=pre_act, act=act),
        out_shape=jax.ShapeDtypeStruct((m, n), out_dtype),
        grid=(n // tn, m // tm),
        in_specs=in_specs,
        out_specs=pl.BlockSpec((tm, tn), lambda j, i: (i, j)),
        scratch_shapes=[pltpu.VMEM((k, tn), BF16)],
        compiler_params=_cp(2),
        name=name,
    )(*args)


def _lora_kernel(*refs, has_bias, mid_act, out_act):
    if has_bias:
        x_ref, w1_ref, w2_ref, b_ref, o_ref, w1b, w2b = refs
    else:
        x_ref, w1_ref, w2_ref, o_ref, w1b, w2b = refs
        b_ref = None

    @pl.when(pl.program_id(0) == 0)
    def _():
        w1b[...] = w1_ref[...].astype(BF16)
        w2b[...] = w2_ref[...].astype(BF16)

    mid = _act(_dot(x_ref[...], w1b[...]), mid_act)
    acc = _dot(mid.astype(BF16), w2b[...])
    if has_bias:
        acc = acc + b_ref[...]
    o_ref[...] = _act(acc, out_act)


def _lora(x, w1, w2, bias, *, mid_act, out_act, name):
    m, d = x.shape
    r = w1.shape[1]
    rp = -(-r // LANE) * LANE
    if rp != r:
        w1 = jnp.pad(w1, ((0, 0), (0, rp - r)))
        w2 = jnp.pad(w2, ((0, rp - r), (0, 0)))
    n = w2.shape[1]
    tm = _pick(m, 512, 16)
    in_specs = [pl.BlockSpec((tm, d), lambda i: (i, 0)),
                pl.BlockSpec((d, rp), lambda i: (0, 0)),
                pl.BlockSpec((rp, n), lambda i: (0, 0))]
    args = [x, w1, w2]
    if bias is not None:
        in_specs.append(pl.BlockSpec((1, n), lambda i: (0, 0)))
        args.append(bias.reshape(1, n))
    return pl.pallas_call(
        functools.partial(_lora_kernel, has_bias=bias is not None, mid_act=mid_act, out_act=out_act),
        out_shape=jax.ShapeDtypeStruct((m, n), F32),
        grid=(m // tm,),
        in_specs=in_specs,
        out_specs=pl.BlockSpec((tm, n), lambda i: (i, 0)),
        scratch_shapes=[pltpu.VMEM((d, rp), BF16), pltpu.VMEM((rp, n), BF16)],
        compiler_params=_cp(1),
        name=name,
    )(*args)


def _mod_spec(mod, slot, d, tr):
    rm = mod.shape[1]
    if rm == 1:
        return pl.BlockSpec((1, 1, d), lambda g, t: (g, 0, slot))
    return pl.BlockSpec((1, tr, d), lambda g, t: (g, t, slot))


def _resid_norm_kernel(*refs, with_next):
    if with_next:
        x_ref, o_ref, g1_ref, gate_ref, g2_ref, sc_ref, sh_ref, xn_ref, h_ref = refs
    else:
        x_ref, o_ref, g1_ref, gate_ref, xn_ref = refs
    xn = x_ref[0] + gate_ref[0] * _rms(o_ref[0], g1_ref[...])
    xn_ref[0] = xn
    if with_next:
        h_ref[0] = (_rms(xn, g2_ref[...]) * (1.0 + sc_ref[0]) + sh_ref[0]).astype(h_ref.dtype)


def _resid_norm(x, o, g1, mod, gate_slot, nxt=None, name="resid_norm"):
    g, r, d = x.shape
    tr = _pick(r, 256, 16)
    row = pl.BlockSpec((1, tr, d), lambda gi, t: (gi, t, 0))
    vec = pl.BlockSpec((1, d), lambda gi, t: (0, 0))
    in_specs = [row, row, vec, _mod_spec(mod, gate_slot, d, tr)]
    args = [x, o, g1.reshape(1, d), mod]
    out_shape = [jax.ShapeDtypeStruct((g, r, d), F32)]
    out_specs = [row]
    if nxt is not None:
        g2, mod2, sc_slot, sh_slot = nxt
        in_specs += [vec, _mod_spec(mod2, sc_slot, d, tr), _mod_spec(mod2, sh_slot, d, tr)]
        args += [g2.reshape(1, d), mod2, mod2]
        out_shape.append(jax.ShapeDtypeStruct((g, r, d), BF16))
        out_specs.append(row)
    res = pl.pallas_call(
        functools.partial(_resid_norm_kernel, with_next=nxt is not None),
        out_shape=out_shape,
        grid=(g, r // tr),
        in_specs=in_specs,
        out_specs=out_specs,
        compiler_params=_cp(2),
        name=name,
    )(*args)
    return res if nxt is not None else res[0]


def _rwkv_prep_kernel(x_ref, g_ref, sc_ref, sh_ref, mix_ref, shift_ref, *rest, seq):
    outs = rest[:6]
    shift_out = rest[6]
    h = _rms(x_ref[0], g_ref[...]) * (1.0 + sc_ref[0]) + sh_ref[0]
    tr = h.shape[0]
    if seq:
        carry = rest[7]
        t = pl.program_id(1)

        @pl.when(t == 0)
        def _():
            carry[...] = jnp.broadcast_to(shift_ref[0], carry.shape)

        hprev = _shift_rows(h, carry[...], 1)
        carry[...] = h[tr - SUBLANE:tr]

        @pl.when(t == pl.num_programs(1) - 1)
        def _():
            shift_out[0] = h[tr - 1:tr]
    else:
        hprev = shift_ref[0]
        shift_out[0] = h
    xx = hprev - h
    for j in range(6):
        outs[j][0] = (h + xx * mix_ref[j:j + 1, :]).astype(BF16)


def _rwkv_prep(x, g, mod, mix, shift, seq):
    gg, r, d = x.shape
    tr = _pick(r, 256, 16)
    row = pl.BlockSpec((1, tr, d), lambda gi, t: (gi, t, 0))
    if seq:
        shift_spec = pl.BlockSpec((1, 1, d), lambda gi, t: (gi, 0, 0))
        shift_shape = (gg, 1, d)
        scratch = [pltpu.VMEM((SUBLANE, d), F32)]
    else:
        shift_spec = row
        shift_shape = (gg, r, d)
        scratch = []
    outs = pl.pallas_call(
        functools.partial(_rwkv_prep_kernel, seq=seq),
        out_shape=[jax.ShapeDtypeStruct((gg, r, d), BF16)] * 6 + [jax.ShapeDtypeStruct(shift_shape, F32)],
        grid=(gg, r // tr),
        in_specs=[row, pl.BlockSpec((1, d), lambda gi, t: (0, 0)),
                  _mod_spec(mod, 1, d, tr), _mod_spec(mod, 0, d, tr),
                  pl.BlockSpec((6, d), lambda gi, t: (0, 0)), shift_spec],
        out_specs=[row] * 6 + [shift_spec],
        scratch_shapes=scratch,
        compiler_params=_cp(2),
        name="rwkv_prep",
    )(x, g.reshape(1, d), mod, mod, mix, shift)
    return outs[:6], outs[6]


def _wkv_terms(k, wpre, a, kkp, kap, ones_bd):
    logd = -0.6065306597126334 * _sigmoid(wpre)
    kk = k * kkp
    kk = kk * lax.rsqrt(jnp.maximum(_head_sum(kk * kk, ones_bd), 1e-24))
    kmod = k * (1.0 + (a - 1.0) * kap)
    return logd, kk, kmod


def _wkv_chunk_kernel(r_ref, k_ref, v_ref, w_ref, a_ref, kkp_ref, kap_ref, y_ref, sout_ref,
                      S, Rt, At, Bt, Kt, Vs, Gc, Ys):
    c = pl.program_id(1)
    n_pairs = S.shape[0]
    C = r_ref.shape[1]

    @pl.when(c == 0)
    def _():
        S[...] = jnp.zeros(S.shape, F32)

    ones_bd = _head_ones()
    a = a_ref[0]
    logd, kk, kmod = _wkv_terms(k_ref[0], w_ref[0], a, kkp_ref[...], kap_ref[...], ones_bd)
    ri = lax.broadcasted_iota(jnp.int32, (C, C), 0)
    ci = lax.broadcasted_iota(jnp.int32, (C, C), 1)
    tri = (ri >= ci).astype(BF16)
    cum = _split_dot_left(tri, logd, 3)
    gam = jnp.exp(cum)
    rt = r_ref[0] * gam
    at = -kk * jnp.exp(cum - logd)
    ginv = jnp.exp(-cum)
    bt = kk * a * ginv
    kt = kmod * ginv
    v = v_ref[0]
    for p in range(n_pairs):
        sl = slice(p * LANE, (p + 1) * LANE)
        Rt[p] = rt[:, sl]
        At[p] = at[:, sl]
        Bt[p] = bt[:, sl]
        Kt[p] = kt[:, sl]
        Vs[p] = v[:, sl]
        Gc[p] = jnp.broadcast_to(gam[C - 1:C, sl], (SUBLANE, LANE))

    lane = lax.broadcasted_iota(jnp.int32, (1, LANE), 1)
    first = lane < HEAD
    r2 = lax.broadcasted_iota(jnp.int32, (2 * C, 2 * C), 0)
    c2 = lax.broadcasted_iota(jnp.int32, (2 * C, 2 * C), 1)
    strict = r2 > c2
    incl = r2 >= c2
    rs = lax.broadcasted_iota(jnp.int32, (LANE, LANE), 0)
    cs = lax.broadcasted_iota(jnp.int32, (LANE, LANE), 1)
    same_head = (rs < HEAD) == (cs < HEAD)

    def stack(x):
        return jnp.concatenate([jnp.where(first, x, 0.0), jnp.where(first, 0.0, x)], axis=0).astype(BF16)

    n_fac = max(1, (C - 1).bit_length())

    def pairs_step(ps):
        n = range(len(ps))
        rp, ap, bp = [Rt[p] for p in ps], [At[p] for p in ps], [Bt[p] for p in ps]
        kp, vp = [Kt[p] for p in ps], [Vs[p] for p in ps]
        sp, gc = [S[p] for p in ps], [Gc[p] for p in ps]
        am, rm = [stack(t) for t in ap], [stack(t) for t in rp]
        bm, km = [stack(t) for t in bp], [stack(t) for t in kp]
        lp = [jnp.where(strict, _dot_nt(am[i], bm[i]), 0.0) for i in n]
        lak = [jnp.where(strict, _dot_nt(am[i], km[i]), 0.0).astype(BF16) for i in n]
        grb = [jnp.where(incl, _dot_nt(rm[i], bm[i]), 0.0).astype(BF16) for i in n]
        grk = [jnp.where(incl, _dot_nt(rm[i], km[i]), 0.0).astype(BF16) for i in n]
        spb = [t.astype(BF16) for t in sp]
        a_s = [_dot_nt(ap[i].astype(BF16), spb[i]) for i in n]
        r_s = [_dot_nt(rp[i].astype(BF16), spb[i]) for i in n]
        vst = [jnp.concatenate([t, t], axis=0).astype(BF16) for t in vp]
        x = [jnp.concatenate([a_s[i], a_s[i]], axis=0) + _dot(lak[i], vst[i]) for i in n]
        for f in range(n_fac):
            lb = [t.astype(BF16) for t in lp]
            x = [x[i] + _split_dot_left(lb[i], x[i], 2) for i in n]
            if f + 1 < n_fac:
                lp = [_dot(t, t) for t in lb]
        yst = [jnp.concatenate([r_s[i], r_s[i]], axis=0) + _dot(grb[i], x[i].astype(BF16)) + _dot(grk[i], vst[i])
               for i in n]
        u = [jnp.where(first, t[:C], t[C:]) for t in x]
        ds = [_dot_tn(u[i].astype(BF16), bp[i].astype(BF16)) + _dot_tn(vp[i].astype(BF16), kp[i].astype(BF16))
              for i in n]
        for i, p in enumerate(ps):
            Ys[p] = jnp.where(first, yst[i][:C], yst[i][C:])
            S[p] = jnp.where(same_head, (sp[i] + ds[i]) * gc[i][0:1, :], 0.0)

    def group(gi, carry):
        pairs_step([gi * PAIR_UNROLL + uu for uu in range(PAIR_UNROLL)])
        return carry

    if n_pairs % PAIR_UNROLL == 0:
        lax.fori_loop(0, n_pairs // PAIR_UNROLL, group, 0)
    else:
        pairs_step(list(range(n_pairs)))

    for p in range(n_pairs):
        y_ref[0, :, p * LANE:(p + 1) * LANE] = Ys[p]

    @pl.when(c == pl.num_programs(1) - 1)
    def _():
        sout_ref[0] = S[...]


def _wkv_chunked(r, k, v, wpre, a, kkp, kap):
    g, t, d = r.shape
    C = CHUNK
    np_ = d // LANE
    row = pl.BlockSpec((1, C, d), lambda gi, ci: (gi, ci, 0))
    vec = pl.BlockSpec((1, d), lambda gi, ci: (0, 0))
    pair_buf = pltpu.VMEM((np_, C, LANE), F32)
    return pl.pallas_call(
        _wkv_chunk_kernel,
        out_shape=[jax.ShapeDtypeStruct((g, t, d), F32),
                   jax.ShapeDtypeStruct((g, np_, LANE, LANE), F32)],
        grid=(g, t // C),
        in_specs=[row] * 5 + [vec, vec],
        out_specs=[row, pl.BlockSpec((1, np_, LANE, LANE), lambda gi, ci: (gi, 0, 0, 0))],
        scratch_shapes=[pltpu.VMEM((np_, LANE, LANE), F32), pair_buf, pair_buf, pair_buf, pair_buf, pair_buf,
                        pltpu.VMEM((np_, SUBLANE, LANE), F32), pair_buf],
        compiler_params=_cp(2),
        name="wkv_chunk",
    )(r, k, v, wpre, a, kkp.reshape(1, d), kap.reshape(1, d))


def _wkv_vec_kernel(k_ref, w_ref, a_ref, kkp_ref, kap_ref, d_ref, an_ref, b_ref, km_ref):
    a = a_ref[...]
    logd, kk, kmod = _wkv_terms(k_ref[...], w_ref[...], a, kkp_ref[...], kap_ref[...], _head_ones())
    d_ref[...] = jnp.exp(logd)
    an_ref[...] = -kk
    b_ref[...] = kk * a
    km_ref[...] = kmod


def _wkv_vec(k, wpre, a, kkp, kap):
    b, d = k.shape
    full = pl.BlockSpec((b, d), lambda i: (0, 0))
    vec = pl.BlockSpec((1, d), lambda i: (0, 0))
    return pl.pallas_call(
        _wkv_vec_kernel,
        out_shape=[jax.ShapeDtypeStruct((b, d), F32)] * 4,
        grid=(1,),
        in_specs=[full, full, full, vec, vec],
        out_specs=[full] * 4,
        compiler_params=_cp(1),
        name="wkv_vec",
    )(k, wpre, a, kkp.reshape(1, d), kap.reshape(1, d))


def _wkv_step_kernel(s_ref, d_ref, a_ref, b_ref, k_ref, r_ref, v_ref, sn_ref, y_ref):
    bb, h, n, _ = s_ref.shape
    rows = bb * h * n
    ri = lax.broadcasted_iota(jnp.int32, (n, n), 0)
    ci = lax.broadcasted_iota(jnp.int32, (n, n), 1)
    ones = jnp.ones((n, n), BF16)
    eye = (ri == ci).astype(F32)

    def rsum(x4):
        return _dot_split(x4.reshape(rows, n), ones, 2).reshape(bb, h, n, n)

    s = s_ref[...]
    sa = rsum(s * a_ref[...])
    vcol = rsum(v_ref[...] * eye)
    sn = s * d_ref[...] + sa * b_ref[...] + vcol * k_ref[...]
    sn_ref[...] = sn
    y2 = rsum(sn * r_ref[...])
    y_ref[...] = jnp.sum(y2 * eye, axis=2, keepdims=True)


def _wkv_step(s, dvec, an, bvec, kmod, r, v):
    b, h, n, _ = s.shape
    bb = _pick(b, 4, 1)
    sblk = pl.BlockSpec((bb, h, n, n), lambda i: (i, 0, 0, 0))
    vblk = pl.BlockSpec((bb, h, 1, n), lambda i: (i, 0, 0, 0))
    vec = lambda x: x.reshape(b, h, 1, n)
    sn, y = pl.pallas_call(
        _wkv_step_kernel,
        out_shape=[jax.ShapeDtypeStruct((b, h, n, n), F32), jax.ShapeDtypeStruct((b, h, 1, n), F32)],
        grid=(b // bb,),
        in_specs=[sblk] + [vblk] * 6,
        out_specs=[sblk, vblk],
        compiler_params=_cp(1),
        name="wkv_step",
    )(s, vec(dvec), vec(an), vec(bvec), vec(kmod), vec(r), vec(v))
    return sn, y.reshape(b, h * n)


def _wkv_post_kernel(y_ref, r_ref, k_ref, v_ref, a_ref, g_ref, kap_ref, rk_ref, lg_ref, lb_ref, o_ref):
    ones_bd = _head_ones()
    y = y_ref[0]
    inv_n = 1.0 / HEAD
    mu = _head_sum(y, ones_bd) * inv_n
    yc = y - mu
    var = _head_sum(yc * yc, ones_bd) * inv_n
    yn = yc * lax.rsqrt(var + LNX_EPS) * lg_ref[...] + lb_ref[...]
    kmod = k_ref[0] * (1.0 + (a_ref[0] - 1.0) * kap_ref[...])
    bonus = _head_sum(r_ref[0] * kmod * rk_ref[...], ones_bd) * v_ref[0]
    o_ref[0] = ((yn + bonus) * g_ref[0]).astype(BF16)


def _wkv_post(y, r, k, v, a, g, kap, rk, lnx_g, lnx_b):
    gg, t, d = y.shape
    tr = _pick(t, 256, 16)
    row = pl.BlockSpec((1, tr, d), lambda gi, ti: (gi, ti, 0))
    vec = pl.BlockSpec((1, d), lambda gi, ti: (0, 0))
    return pl.pallas_call(
        _wkv_post_kernel,
        out_shape=jax.ShapeDtypeStruct((gg, t, d), BF16),
        grid=(gg, t // tr),
        in_specs=[row] * 6 + [vec] * 4,
        out_specs=row,
        compiler_params=_cp(2),
        name="wkv_post",
    )(y, r, k, v, a, g, kap.reshape(1, d), rk.reshape(1, d), lnx_g.reshape(1, d), lnx_b.reshape(1, d))


def _lru_conv_kernel(x_ref, w_ref, b_ref, buf_ref, xc_ref, *rest, seq, width):
    x = x_ref[0]
    tr = x.shape[0]
    acc = b_ref[...] + x * w_ref[width - 1:width, :]
    if seq:
        nbuf_ref, carry = rest
        t = pl.program_id(1)

        @pl.when(t == 0)
        def _():
            carry[...] = buf_ref[0]

        c8 = carry[...]
        for s in range(1, width):
            acc = acc + _shift_rows(x, c8, s) * w_ref[width - 1 - s:width - s, :]
        carry[...] = x[tr - SUBLANE:tr]
        nbuf_ref[0] = x[tr - SUBLANE:tr]
    else:
        for s in range(1, width):
            acc = acc + buf_ref[0, :, width - 1 - s, :] * w_ref[width - 1 - s:width - s, :]
    xc_ref[0] = acc


def _lru_conv(proj, w, b, buf, seq):
    g, r, d2 = proj.shape
    d = d2 // 2
    width = w.shape[0]
    tr = _pick(r, 256, 16)
    row = pl.BlockSpec((1, tr, d), lambda gi, t: (gi, t, 0))
    in_specs = [pl.BlockSpec((1, tr, d), lambda gi, t: (gi, t, 1)),
                pl.BlockSpec((width, d), lambda gi, t: (0, 0)),
                pl.BlockSpec((1, d), lambda gi, t: (0, 0))]
    if seq:
        in_specs.append(pl.BlockSpec((1, SUBLANE, d), lambda gi, t: (gi, 0, 0)))
        out_shape = [jax.ShapeDtypeStruct((g, r, d), F32), jax.ShapeDtypeStruct((g, SUBLANE, d), F32)]
        out_specs = [row, pl.BlockSpec((1, SUBLANE, d), lambda gi, t: (gi, 0, 0))]
        scratch = [pltpu.VMEM((SUBLANE, d), F32)]
    else:
        in_specs.append(pl.BlockSpec((1, tr, width - 1, d), lambda gi, t: (gi, t, 0, 0)))
        out_shape = [jax.ShapeDtypeStruct((g, r, d), F32)]
        out_specs = [row]
        scratch = []
    return pl.pallas_call(
        functools.partial(_lru_conv_kernel, seq=seq, width=width),
        out_shape=out_shape,
        grid=(g, r // tr),
        in_specs=in_specs,
        out_specs=out_specs,
        scratch_shapes=scratch,
        compiler_params=_cp(2),
        name="lru_conv",
    )(proj, w, b.reshape(1, d), buf)


def _lru_gates_kernel(x_ref, wa_ref, ba_ref, wx_ref, bx_ref, ga_ref, gx_ref):
    x = x_ref[...].astype(BF16)
    ga_ref[...] = _sigmoid(_dot(x, wa_ref[0].astype(BF16)) + ba_ref[...])
    gx_ref[...] = _sigmoid(_dot(x, wx_ref[0].astype(BF16)) + bx_ref[...])


def _lru_gates(xc, wa, ba, wx, bx):
    m, d = xc.shape
    nb, bw, _ = wa.shape
    tm = _pick(m, 1024, 8)
    xs = pl.BlockSpec((tm, bw), lambda n, i: (i, n))
    ws = pl.BlockSpec((1, bw, bw), lambda n, i: (n, 0, 0))
    bs = pl.BlockSpec((1, bw), lambda n, i: (0, n))
    return pl.pallas_call(
        _lru_gates_kernel,
        out_shape=[jax.ShapeDtypeStruct((m, d), F32)] * 2,
        grid=(nb, m // tm),
        in_specs=[xs, ws, bs, ws, bs],
        out_specs=[xs, xs],
        compiler_params=_cp(2),
        name="lru_gates",
    )(xc, wa, ba.reshape(1, d), wx, bx.reshape(1, d))


def _lru_coeffs(ga, gx, xc, lam):
    log_a = -LRU_C * ga * _softplus(-lam)
    a = jnp.exp(log_a)
    mult = jnp.sqrt(1.0 - jnp.exp(2.0 * log_a))
    return a, mult


def _lru_scan_kernel(ga_ref, gx_ref, xc_ref, y_ref, lam_ref, h0_ref, o_ref, hl_ref, carry, abuf, bbuf, hbuf):
    t = pl.program_id(2)
    tr = ga_ref.shape[1]

    @pl.when(t == 0)
    def _():
        carry[...] = jnp.broadcast_to(h0_ref[0], carry.shape)

    xc = xc_ref[0]
    a, mult = _lru_coeffs(ga_ref[0], gx_ref[0], xc, lam_ref[...])
    row = lax.broadcasted_iota(jnp.int32, a.shape, 0)
    mult = jnp.where(jnp.logical_and(t == 0, row == 0), 1.0, mult)
    abuf[...] = a
    bbuf[...] = mult * gx_ref[0] * xc
    r8 = lax.broadcasted_iota(jnp.int32, (SUBLANE, a.shape[1]), 0)

    def body(i, c):
        off = pl.multiple_of(i * SUBLANE, SUBLANE)
        aa = abuf[pl.ds(off, SUBLANE), :]
        bb = bbuf[pl.ds(off, SUBLANE), :]
        for s in (1, 2, 4):
            a_s = jnp.where(r8 >= s, pltpu.roll(aa, s, 0), 1.0)
            b_s = jnp.where(r8 >= s, pltpu.roll(bb, s, 0), 0.0)
            bb = aa * b_s + bb
            aa = aa * a_s
        h = aa * c + bb
        hbuf[pl.ds(off, SUBLANE), :] = h
        return h[SUBLANE - 1:SUBLANE, :]

    c = lax.fori_loop(0, tr // SUBLANE, body, carry[0:1, :])
    carry[...] = jnp.broadcast_to(c, carry.shape)
    o_ref[0] = (hbuf[...] * _gelu(y_ref[0])).astype(BF16)

    @pl.when(t == pl.num_programs(2) - 1)
    def _():
        hl_ref[0] = c


def _lru_scan(ga, gx, xc, proj, lam, h0):
    g, r, d = ga.shape
    tr = _pick(r, 256, 16)
    td = _pick(d, 512, LANE)
    row = pl.BlockSpec((1, tr, td), lambda gi, di, t: (gi, t, di))
    vec = pl.BlockSpec((1, td), lambda gi, di, t: (0, di))
    st = pl.BlockSpec((1, 1, td), lambda gi, di, t: (gi, 0, di))
    tile = pltpu.VMEM((tr, td), F32)
    return pl.pallas_call(
        _lru_scan_kernel,
        out_shape=[jax.ShapeDtypeStruct((g, r, d), BF16), jax.ShapeDtypeStruct((g, 1, d), F32)],
        grid=(g, d // td, r // tr),
        in_specs=[row, row, row, row, vec, st],
        out_specs=[row, st],
        scratch_shapes=[pltpu.VMEM((SUBLANE, td), F32), tile, tile, tile],
        compiler_params=_cp(3),
        name="lru_scan",
    )(ga, gx, xc, proj, lam.reshape(1, d), h0)


def _lru_step_kernel(ga_ref, gx_ref, xc_ref, y_ref, lam_ref, h0_ref, o_ref, h_ref):
    xc = xc_ref[0]
    a, mult = _lru_coeffs(ga_ref[0], gx_ref[0], xc, lam_ref[...])
    h = a * h0_ref[0] + mult * gx_ref[0] * xc
    h_ref[0] = h
    o_ref[0] = (h * _gelu(y_ref[0])).astype(BF16)


def _lru_step(ga, gx, xc, proj, lam, h0):
    g, r, d = ga.shape
    row = pl.BlockSpec((1, r, d), lambda i: (0, 0, 0))
    return pl.pallas_call(
        _lru_step_kernel,
        out_shape=[jax.ShapeDtypeStruct((g, r, d), BF16), jax.ShapeDtypeStruct((g, r, d), F32)],
        grid=(1,),
        in_specs=[row, row, row, row, pl.BlockSpec((1, d), lambda i: (0, 0)), row],
        out_specs=[row, row],
        compiler_params=_cp(1),
        name="lru_step",
    )(ga, gx, xc, proj, lam.reshape(1, d), h0)


def _ffn_up_kernel(*refs, seq, tiles_per_seq):
    if seq:
        x_ref, wg_ref, wu_ref, cw_ref, cb_ref, buf_ref, o_ref, nb_ref, wgb, wub, carry = refs
    else:
        x_ref, wg_ref, wu_ref, cw_ref, cb_ref, buf_ref, o_ref, nb_ref, wgb, wub = refs
    i = pl.program_id(1)

    @pl.when(i == 0)
    def _():
        wgb[...] = wg_ref[...].astype(BF16)
        wub[...] = wu_ref[...].astype(BF16)

    x = x_ref[...]
    gt = _dot(x, wgb[...])
    u = _dot(x, wub[...])
    tm = gt.shape[0]
    if seq:
        @pl.when(i % tiles_per_seq == 0)
        def _():
            carry[...] = buf_ref[0]

        c8 = carry[...]
        p1 = _shift_rows(gt, c8, 1)
        p2 = _shift_rows(gt, c8, 2)
        carry[...] = gt[tm - SUBLANE:tm]
        nb_ref[0] = gt[tm - SUBLANE:tm]
    else:
        p2 = buf_ref[:, 0, :]
        p1 = buf_ref[:, 1, :]
        nb_ref[:, 0, :] = p1
        nb_ref[:, 1, :] = gt
    gc = cb_ref[...] + p2 * cw_ref[0:1, :] + p1 * cw_ref[1:2, :] + gt * cw_ref[2:3, :]
    o_ref[...] = (_gelu(gc) * u).astype(BF16)


def _ffn_up(x, wg, wu, layer, cw, cb, buf, seq, rows_per_seq):
    m, k = x.shape
    f = wg.shape[2]
    tn = _pick(f, 512, LANE)
    tm = _pick(rows_per_seq if seq else m, 512, 16)
    nj = f // tn
    xs = pl.BlockSpec((tm, k), lambda j, i: (i, 0))
    ws = _wspec(wg, layer, k, tn, lambda j, i: j)
    os_ = pl.BlockSpec((tm, tn), lambda j, i: (i, j))
    in_specs = [xs, ws, ws, pl.BlockSpec((3, tn), lambda j, i: (0, j)), pl.BlockSpec((1, tn), lambda j, i: (0, j))]
    scratch = [pltpu.VMEM((k, tn), BF16), pltpu.VMEM((k, tn), BF16)]
    if seq:
        tps = rows_per_seq // tm
        g = m // rows_per_seq
        bspec = pl.BlockSpec((1, SUBLANE, tn), lambda j, i: (i // tps, 0, j))
        in_specs.append(bspec)
        out_shape = [jax.ShapeDtypeStruct((m, f), BF16), jax.ShapeDtypeStruct((g, SUBLANE, f), F32)]
        out_specs = [os_, bspec]
        scratch.append(pltpu.VMEM((SUBLANE, tn), F32))
    else:
        tps = 1
        hspec = pl.BlockSpec((tm, 2, tn), lambda j, i: (i, 0, j))
        in_specs.append(hspec)
        out_shape = [jax.ShapeDtypeStruct((m, f), BF16), jax.ShapeDtypeStruct((m, 2, f), F32)]
        out_specs = [os_, hspec]
    args = [x, wg, wu, cw, cb.reshape(1, f), buf]
    return pl.pallas_call(
        functools.partial(_ffn_up_kernel, seq=seq, tiles_per_seq=tps),
        out_shape=out_shape,
        grid=(nj, m // tm),
        in_specs=in_specs,
        out_specs=out_specs,
        scratch_shapes=scratch,
        compiler_params=_cp(2),
        name="ffn_up",
    )(*args)


def _trunk(x, mods, st, seq, P):
    g, r, d = x.shape
    m = g * r
    depth = P["w_mod"].shape[0]
    o_wkv, o_shift, o_h, o_lconv, o_fconv = [], [], [], [], []
    flat = lambda t: t.reshape(m, t.shape[-1])
    unflat = lambda t: t.reshape(g, r, t.shape[-1])
    h_in = None
    for i in range(depth):
        mod = mods[i]
        j = i // 2
        if i % 2 == 0:
            if seq:
                shift = jnp.zeros((g, 1, d), F32)
            else:
                shift = st["shift"][j].reshape(1, r, d)
            mixes, shift_new = _rwkv_prep(x, P["norm_g"][i, 0], mod, P["rw_mix"][j], shift, seq)
            xr, xw, xk, xv, xa, xg = [flat(t) for t in mixes]
            rr = _mm(xr, P["rw_wr"], j, name="rw_r")
            kk = _mm(xk, P["rw_wk"], j, name="rw_k")
            vv = _mm(xv, P["rw_wv"], j, name="rw_v")
            wpre = _lora(xw, P["rw_w1"][j], P["rw_w2"][j], P["rw_w0"][j], mid_act="tanh", out_act=None, name="rw_w")
            aa = _lora(xa, P["rw_a1"][j], P["rw_a2"][j], P["rw_a0"][j], mid_act=None, out_act="sigmoid", name="rw_a")
            gg = _lora(xg, P["rw_g1"][j], P["rw_g2"][j], None, mid_act="sigmoid", out_act=None, name="rw_g")
            if seq:
                y, s_bd = _wkv_chunked(unflat(rr), unflat(kk), unflat(vv), unflat(wpre), unflat(aa),
                                       P["rw_kk"][j], P["rw_ka"][j])
                npairs = d // LANE
                s_new = jnp.stack([s_bd[:, :, :HEAD, :HEAD], s_bd[:, :, HEAD:, HEAD:]], axis=2)
                s_new = s_new.reshape(g, 2 * npairs, HEAD, HEAD)
                o_shift.append(shift_new.reshape(g, d))
            else:
                dvec, an, bvec, kmod = _wkv_vec(kk, wpre, aa, P["rw_kk"][j], P["rw_ka"][j])
                s_new, y = _wkv_step(st["wkv"][j], dvec, an, bvec, kmod, rr, vv)
                y = y.reshape(1, r, d)
                o_shift.append(shift_new.reshape(r, d))
            o_wkv.append(s_new)
            xo = _wkv_post(y, unflat(rr), unflat(kk), unflat(vv), unflat(aa), unflat(gg),
                           P["rw_ka"][j], P["rw_rk"][j], P["rw_lnx_g"][j], P["rw_lnx_b"][j])
            out = _mm(flat(xo), P["rw_wo"], j, name="rw_o")
        else:
            proj = _mm(flat(h_in), P["lru_w_in"], j, P["lru_b_in"][j], name="lru_in")
            dr = proj.shape[1] // 2
            width = P["lru_conv_w"].shape[1]
            if seq:
                xc, nb8 = _lru_conv(unflat(proj), P["lru_conv_w"][j], P["lru_conv_b"][j],
                                    jnp.zeros((g, SUBLANE, dr), F32), True)
                o_lconv.append(nb8[:, SUBLANE - (width - 1):, :])
            else:
                hist = st["lru_conv"][j]
                (xc,) = _lru_conv(unflat(proj), P["lru_conv_w"][j], P["lru_conv_b"][j],
                                  hist.reshape(1, r, width - 1, dr), False)
                o_lconv.append(jnp.concatenate([hist[:, 1:], proj[:, None, dr:]], axis=1))
            ga, gx = _lru_gates(flat(xc), P["lru_wa"][j], P["lru_ba"][j], P["lru_wx"][j], P["lru_bx"][j])
            if seq:
                hy, hl = _lru_scan(unflat(ga), unflat(gx), xc, unflat(proj), P["lru_lambda"][j],
                                   jnp.zeros((g, 1, dr), F32))
                o_h.append(hl.reshape(g, dr))
            else:
                hy, hl = _lru_step(unflat(ga), unflat(gx), xc, unflat(proj), P["lru_lambda"][j],
                                   st["lru_h"][j].reshape(1, r, dr))
                o_h.append(hl.reshape(r, dr))
            out = _mm(flat(hy), P["lru_w_out"], j, name="lru_out")
        x, hf = _resid_norm(x, unflat(out), P["norm_g"][i, 1], mod, 2, (P["norm_g"][i, 2], mod, 4, 3),
                            name="resid_mix")
        fdim = P["ffn_w_gate"].shape[2]
        if seq:
            hact, nb8 = _ffn_up(flat(hf), P["ffn_w_gate"], P["ffn_w_up"], i, P["ffn_conv_w"][i],
                                P["ffn_conv_b"][i], jnp.zeros((g, SUBLANE, fdim), F32), True, r)
            o_fconv.append(nb8[:, SUBLANE - 2:, :])
        else:
            hact, nhist = _ffn_up(flat(hf), P["ffn_w_gate"], P["ffn_w_up"], i, P["ffn_conv_w"][i],
                                  P["ffn_conv_b"][i], st["ffn_conv"][i], False, r)
            o_fconv.append(nhist)
        f = _mm(hact, P["ffn_w_down"], i, tm_target=512, name="ffn_down")
        if i + 1 < depth:
            nmod = mods[i + 1]
            if (i + 1) % 2 == 0:
                x = _resid_norm(x, unflat(f), P["norm_g"][i, 3], mod, 5, name="resid_ffn")
            else:
                x, h_in = _resid_norm(x, unflat(f), P["norm_g"][i, 3], mod, 5,
                                      (P["norm_g"][i + 1, 0], nmod, 1, 0), name="resid_ffn")
        else:
            x = _resid_norm(x, unflat(f), P["norm_g"][i, 3], mod, 5, name="resid_ffn")
    return x, jnp.stack(o_wkv), jnp.stack(o_shift), jnp.stack(o_h), jnp.stack(o_lconv), jnp.stack(o_fconv)


def kernel(x_prompt, x_sample, c_prompt, c_sample, state_rwkv_wkv, state_rwkv_shift, state_lru_h,
           state_lru_conv, state_ffn_conv, w_mod, b_mod, norm_g, rw_mix, rw_wr, rw_wk, rw_wv, rw_wo,
           rw_w0, rw_w1, rw_w2, rw_a0, rw_a1, rw_a2, rw_g1, rw_g2, rw_kk, rw_ka, rw_rk, rw_lnx_g,
           rw_lnx_b, lru_w_in, lru_b_in, lru_conv_w, lru_conv_b, lru_wa, lru_ba, lru_wx, lru_bx,
           lru_lambda, lru_w_out, ffn_w_gate, ffn_w_up, ffn_w_down, ffn_conv_w, ffn_conv_b):
    P = dict(w_mod=w_mod, b_mod=b_mod, norm_g=norm_g, rw_mix=rw_mix, rw_wr=rw_wr, rw_wk=rw_wk,
             rw_wv=rw_wv, rw_wo=rw_wo, rw_w0=rw_w0, rw_w1=rw_w1, rw_w2=rw_w2, rw_a0=rw_a0, rw_a1=rw_a1,
             rw_a2=rw_a2, rw_g1=rw_g1, rw_g2=rw_g2, rw_kk=rw_kk, rw_ka=rw_ka, rw_rk=rw_rk,
             rw_lnx_g=rw_lnx_g, rw_lnx_b=rw_lnx_b, lru_w_in=lru_w_in, lru_b_in=lru_b_in,
             lru_conv_w=lru_conv_w, lru_conv_b=lru_conv_b, lru_wa=lru_wa, lru_ba=lru_ba, lru_wx=lru_wx,
             lru_bx=lru_bx, lru_lambda=lru_lambda, lru_w_out=lru_w_out, ffn_w_gate=ffn_w_gate,
             ffn_w_up=ffn_w_up, ffn_w_down=ffn_w_down, ffn_conv_w=ffn_conv_w, ffn_conv_b=ffn_conv_b)
    bp, t, d = x_prompt.shape
    bs = x_sample.shape[0]
    depth = w_mod.shape[0]
    c_all = jnp.concatenate([c_prompt, c_sample], axis=0)
    rows = c_all.shape[0]
    rows_p = -(-rows // 16) * 16
    c_all = jnp.pad(c_all, ((0, rows_p - rows), (0, 0)))
    mods_p, mods_s = [], []
    for i in range(depth):
        mod = _mm(c_all, w_mod, i, b_mod[i], pre_act="silu", name="mod")
        mods_p.append(mod[:bp].reshape(bp, 1, N_MOD * d))
        mods_s.append(mod[bp:bp + bs].reshape(1, bs, N_MOD * d))
    y_p, p_wkv, p_shift, p_h, p_lconv, p_fconv = _trunk(x_prompt, mods_p, None, True, P)
    st = dict(wkv=state_rwkv_wkv, shift=state_rwkv_shift, lru_h=state_lru_h,
              lru_conv=state_lru_conv, ffn_conv=state_ffn_conv)
    y_s, s_wkv, s_shift, s_h, s_lconv, s_fconv = _trunk(x_sample.reshape(1, bs, d), mods_s, st, False, P)
    return (y_p, y_s.reshape(bs, 1, d), p_wkv, p_shift, p_h, p_lconv, p_fconv,
            s_wkv, s_shift, s_h, s_lconv, s_fconv)
```

```python
import functools

import jax
import jax.numpy as jnp
from jax import lax
from jax.experimental import pallas as pl
from jax.experimental.pallas import tpu as pltpu

F32 = jnp.float32
BF16 = jnp.bfloat16

NORM_EPS = 1e-6
LNX_EPS = 64e-5
LRU_C = 8.0
N_MOD = 6
HEAD = 64
LANE = 128
SUBLANE = 8
CHUNK = 64
PAIR_UNROLL = 16
LRU_SCAN_UNROLL = 4
FFN_SUB_ROWS = 128
VMEM_LIMIT = 52 * 1024 * 1024


def _cp(n_axes):
    return pltpu.CompilerParams(dimension_semantics=("arbitrary",) * n_axes,
                                vmem_limit_bytes=VMEM_LIMIT)


def _pick(n, target, mult):
    best = None
    for t in range(mult, min(n, target) + 1, mult):
        if n % t == 0:
            best = t
    return best if best is not None else n


def _softplus(z):
    return jnp.maximum(z, 0.0) + jnp.log(1.0 + jnp.exp(-jnp.abs(z)))


def _sigmoid(z):
    return 1.0 / (1.0 + jnp.exp(-z))


def _gelu(x):
    return 0.5 * x * (1.0 + jnp.tanh(0.7978845608028654 * (x + 0.044715 * (x * x * x))))


def _act(x, name):
    if name is None:
        return x
    if name == "tanh":
        return jnp.tanh(x)
    if name == "sigmoid":
        return _sigmoid(x)
    if name == "silu":
        return x * _sigmoid(x)
    raise ValueError(name)


def _rms(x, g):
    ms = jnp.mean(x * x, axis=-1, keepdims=True)
    return x * lax.rsqrt(ms + NORM_EPS) * g


def _dot(x, y):
    return jnp.dot(x, y, preferred_element_type=F32)


def _dot_nt(x, y):
    return lax.dot_general(x, y, (((1,), (1,)), ((), ())), preferred_element_type=F32)


def _dot_split(x, m, parts):
    acc = None
    rem = x
    for i in range(parts):
        piece = rem.astype(BF16)
        term = jnp.dot(piece, m, preferred_element_type=F32)
        acc = term if acc is None else acc + term
        if i + 1 < parts:
            rem = rem - piece.astype(F32)
    return acc


def _split_dot_left(m, x, parts):
    acc = None
    rem = x
    for i in range(parts):
        piece = rem.astype(BF16)
        term = jnp.dot(m, piece, preferred_element_type=F32)
        acc = term if acc is None else acc + term
        if i + 1 < parts:
            rem = rem - piece.astype(F32)
    return acc


def _dot_tn(x, y):
    return lax.dot_general(x, y, (((0,), (0,)), ((), ())), preferred_element_type=F32)


def _head_ones():
    r = lax.broadcasted_iota(jnp.int32, (LANE, LANE), 0)
    c = lax.broadcasted_iota(jnp.int32, (LANE, LANE), 1)
    return ((r < HEAD) == (c < HEAD)).astype(BF16)


def _head_sum(x, ones_bd):
    d = x.shape[-1]
    cols = [_dot_split(x[:, s:s + LANE], ones_bd, 2) for s in range(0, d, LANE)]
    return cols[0] if len(cols) == 1 else jnp.concatenate(cols, axis=-1)


def _shift_rows(x, carry8, s):
    n = x.shape[0]
    rolled = pltpu.roll(x, s, 0)
    rolled_c = pltpu.roll(carry8, s, 0)
    row8 = lax.broadcasted_iota(jnp.int32, (SUBLANE, x.shape[1]), 0)
    top = jnp.where(row8 < s, rolled_c, rolled[0:SUBLANE])
    if n == SUBLANE:
        return top
    return jnp.concatenate([top, rolled[SUBLANE:]], axis=0)


def _mm_kernel(*refs, has_bias, pre_act, act):
    if has_bias:
        x_ref, w_ref, b_ref, o_ref, wb_ref = refs
    else:
        x_ref, w_ref, o_ref, wb_ref = refs
        b_ref = None

    @pl.when(pl.program_id(1) == 0)
    def _():
        wb_ref[...] = w_ref[...].astype(BF16)

    x = x_ref[...]
    if pre_act is not None:
        x = _act(x.astype(F32), pre_act)
    acc = _dot(x.astype(BF16), wb_ref[...])
    if has_bias:
        acc = acc + b_ref[...]
    o_ref[...] = _act(acc, act).astype(o_ref.dtype)


def _wspec(w, layer, rows, cols, col_of):
    assert w.ndim == 3
    return pl.BlockSpec((None, rows, cols), lambda *ids: (layer, 0, col_of(*ids)))


def _mm(x, w, layer, bias=None, *, pre_act=None, act=None, out_dtype=F32, tm_target=1024, tn_target=1024,
        name="mm"):
    m, k = x.shape
    n = w.shape[2]
    tm = _pick(m, tm_target, 16)
    tn = _pick(n, tn_target, LANE)
    in_specs = [pl.BlockSpec((tm, k), lambda j, i: (i, 0)),
                _wspec(w, layer, k, tn, lambda j, i: j)]
    args = [x, w]
    if bias is not None:
        in_specs.append(pl.BlockSpec((1, tn), lambda j, i: (0, j)))
        args.append(bias.reshape(1, n))
    return pl.pallas_call(
        functools.partial(_mm_kernel, has_bias=bias is not None, pre_act=pre_act, act=act),
        out_shape=jax.ShapeDtypeStruct((m, n), out_dtype),
        grid=(n // tn, m // tm),
        in_specs=in_specs,
        out_specs=pl.BlockSpec((tm, tn), lambda j, i: (i, j)),
        scratch_shapes=[pltpu.VMEM((k, tn), BF16)],
        compiler_params=_cp(2),
        name=name,
    )(*args)


def _lora_kernel(*refs, has_bias, mid_act, out_act):
    if has_bias:
        x_ref, w1_ref, w2_ref, b_ref, o_ref, w1b, w2b = refs
    else:
        x_ref, w1_ref, w2_ref, o_ref, w1b, w2b = refs
        b_ref = None

    @pl.when(pl.program_id(0) == 0)
    def _():
        w1b[...] = w1_ref[...].astype(BF16)
        w2b[...] = w2_ref[...].astype(BF16)

    mid = _act(_dot(x_ref[...], w1b[...]), mid_act)
    acc = _dot(mid.astype(BF16), w2b[...])
    if has_bias:
        acc = acc + b_ref[...]
    o_ref[...] = _act(acc, out_act)


def _lora(x, w1, w2, bias, *, mid_act, out_act, name):
    m, d = x.shape
    r = w1.shape[1]
    rp = -(-r // LANE) * LANE
    if rp != r:
        w1 = jnp.pad(w1, ((0, 0), (0, rp - r)))
        w2 = jnp.pad(w2, ((0, rp - r), (0, 0)))
    n = w2.shape[1]
    tm = _pick(m, 512, 16)
    in_specs = [pl.BlockSpec((tm, d), lambda i: (i, 0)),
                pl.BlockSpec((d, rp), lambda i: (0, 0)),
                pl.BlockSpec((rp, n), lambda i: (0, 0))]
    args = [x, w1, w2]
    if bias is not None:
        in_specs.append(pl.BlockSpec((1, n), lambda i: (0, 0)))
        args.append(bias.reshape(1, n))
    return pl.pallas_call(
        functools.partial(_lora_kernel, has_bias=bias is not None, mid_act=mid_act, out_act=out_act),
        out_shape=jax.ShapeDtypeStruct((m, n), F32),
        grid=(m // tm,),
        in_specs=in_specs,
        out_specs=pl.BlockSpec((tm, n), lambda i: (i, 0)),
        scratch_shapes=[pltpu.VMEM((d, rp), BF16), pltpu.VMEM((rp, n), BF16)],
        compiler_params=_cp(1),
        name=name,
    )(*args)


def _mod_spec(mod, slot, d, tr):
    rm = mod.shape[1]
    if rm == 1:
        return pl.BlockSpec((1, 1, d), lambda g, t: (g, 0, slot))
    return pl.BlockSpec((1, tr, d), lambda g, t: (g, t, slot))


def _resid_norm_kernel(*refs, with_next):
    if with_next:
        x_ref, o_ref, g1_ref, gate_ref, g2_ref, sc_ref, sh_ref, xn_ref, h_ref = refs
    else:
        x_ref, o_ref, g1_ref, gate_ref, xn_ref = refs
    xn = x_ref[0] + gate_ref[0] * _rms(o_ref[0], g1_ref[...])
    xn_ref[0] = xn
    if with_next:
        h_ref[0] = (_rms(xn, g2_ref[...]) * (1.0 + sc_ref[0]) + sh_ref[0]).astype(h_ref.dtype)


def _resid_norm(x, o, g1, mod, gate_slot, nxt=None, name="resid_norm"):
    g, r, d = x.shape
    tr = _pick(r, 256, 16)
    row = pl.BlockSpec((1, tr, d), lambda gi, t: (gi, t, 0))
    vec = pl.BlockSpec((1, d), lambda gi, t: (0, 0))
    in_specs = [row, row, vec, _mod_spec(mod, gate_slot, d, tr)]
    args = [x, o, g1.reshape(1, d), mod]
    out_shape = [jax.ShapeDtypeStruct((g, r, d), F32)]
    out_specs = [row]
    if nxt is not None:
        g2, mod2, sc_slot, sh_slot = nxt
        in_specs += [vec, _mod_spec(mod2, sc_slot, d, tr), _mod_spec(mod2, sh_slot, d, tr)]
        args += [g2.reshape(1, d), mod2, mod2]
        out_shape.append(jax.ShapeDtypeStruct((g, r, d), BF16))
        out_specs.append(row)
    res = pl.pallas_call(
        functools.partial(_resid_norm_kernel, with_next=nxt is not None),
        out_shape=out_shape,
        grid=(g, r // tr),
        in_specs=in_specs,
        out_specs=out_specs,
        compiler_params=_cp(2),
        name=name,
    )(*args)
    return res if nxt is not None else res[0]


def _rwkv_prep_kernel(x_ref, g_ref, sc_ref, sh_ref, mix_ref, shift_ref, *rest, seq):
    outs = rest[:6]
    shift_out = rest[6]
    h = _rms(x_ref[0], g_ref[...]) * (1.0 + sc_ref[0]) + sh_ref[0]
    tr = h.shape[0]
    if seq:
        carry = rest[7]
        t = pl.program_id(1)

        @pl.when(t == 0)
        def _():
            carry[...] = jnp.broadcast_to(shift_ref[0], carry.shape)

        hprev = _shift_rows(h, carry[...], 1)
        carry[...] = h[tr - SUBLANE:tr]

        @pl.when(t == pl.num_programs(1) - 1)
        def _():
            shift_out[0] = h[tr - 1:tr]
    else:
        hprev = shift_ref[0]
        shift_out[0] = h
    xx = hprev - h
    for j in range(6):
        outs[j][0] = (h + xx * mix_ref[j:j + 1, :]).astype(BF16)


def _rwkv_prep(x, g, mod, mix, shift, seq):
    gg, r, d = x.shape
    tr = _pick(r, 256, 16)
    row = pl.BlockSpec((1, tr, d), lambda gi, t: (gi, t, 0))
    if seq:
        shift_spec = pl.BlockSpec((1, 1, d), lambda gi, t: (gi, 0, 0))
        shift_shape = (gg, 1, d)
        scratch = [pltpu.VMEM((SUBLANE, d), F32)]
    else:
        shift_spec = row
        shift_shape = (gg, r, d)
        scratch = []
    outs = pl.pallas_call(
        functools.partial(_rwkv_prep_kernel, seq=seq),
        out_shape=[jax.ShapeDtypeStruct((gg, r, d), BF16)] * 6 + [jax.ShapeDtypeStruct(shift_shape, F32)],
        grid=(gg, r // tr),
        in_specs=[row, pl.BlockSpec((1, d), lambda gi, t: (0, 0)),
                  _mod_spec(mod, 1, d, tr), _mod_spec(mod, 0, d, tr),
                  pl.BlockSpec((6, d), lambda gi, t: (0, 0)), shift_spec],
        out_specs=[row] * 6 + [shift_spec],
        scratch_shapes=scratch,
        compiler_params=_cp(2),
        name="rwkv_prep",
    )(x, g.reshape(1, d), mod, mod, mix, shift)
    return outs[:6], outs[6]


def _wkv_terms(k, wpre, a, kkp, kap, ones_bd):
    logd = -0.6065306597126334 * _sigmoid(wpre)
    kk = k * kkp
    kk = kk * lax.rsqrt(jnp.maximum(_head_sum(kk * kk, ones_bd), 1e-24))
    kmod = k * (1.0 + (a - 1.0) * kap)
    return logd, kk, kmod


def _wkv_chunk_kernel(r_ref, k_ref, v_ref, w_ref, a_ref, kkp_ref, kap_ref, y_ref, sout_ref,
                      S, Rt, At, Bt, Kt, Vs, Gc, Ys):
    c = pl.program_id(1)
    n_pairs = S.shape[0]
    C = r_ref.shape[1]

    @pl.when(c == 0)
    def _():
        S[...] = jnp.zeros(S.shape, F32)

    ones_bd = _head_ones()
    a = a_ref[0]
    logd, kk, kmod = _wkv_terms(k_ref[0], w_ref[0], a, kkp_ref[...], kap_ref[...], ones_bd)
    ri = lax.broadcasted_iota(jnp.int32, (C, C), 0)
    ci = lax.broadcasted_iota(jnp.int32, (C, C), 1)
    tri = (ri >= ci).astype(BF16)
    cum = _split_dot_left(tri, logd, 3)
    gam = jnp.exp(cum)
    rt = r_ref[0] * gam
    at = -kk * jnp.exp(cum - logd)
    ginv = jnp.exp(-cum)
    bt = kk * a * ginv
    kt = kmod * ginv
    v = v_ref[0]
    for p in range(n_pairs):
        sl = slice(p * LANE, (p + 1) * LANE)
        Rt[p] = rt[:, sl]
        At[p] = at[:, sl]
        Bt[p] = bt[:, sl]
        Kt[p] = kt[:, sl]
        Vs[p] = v[:, sl]
        Gc[p] = jnp.broadcast_to(gam[C - 1:C, sl], (SUBLANE, LANE))

    lane = lax.broadcasted_iota(jnp.int32, (1, LANE), 1)
    first = lane < HEAD
    r2 = lax.broadcasted_iota(jnp.int32, (2 * C, 2 * C), 0)
    c2 = lax.broadcasted_iota(jnp.int32, (2 * C, 2 * C), 1)
    strict = r2 > c2
    incl = r2 >= c2
    rs = lax.broadcasted_iota(jnp.int32, (LANE, LANE), 0)
    cs = lax.broadcasted_iota(jnp.int32, (LANE, LANE), 1)
    same_head = (rs < HEAD) == (cs < HEAD)

    def stack(x):
        return jnp.concatenate([jnp.where(first, x, 0.0), jnp.where(first, 0.0, x)], axis=0).astype(BF16)

    n_fac = max(1, (C - 1).bit_length())

    def pairs_step(ps):
        n = range(len(ps))
        rp, ap, bp = [Rt[p] for p in ps], [At[p] for p in ps], [Bt[p] for p in ps]
        kp, vp = [Kt[p] for p in ps], [Vs[p] for p in ps]
        sp, gc = [S[p] for p in ps], [Gc[p] for p in ps]
        am, rm = [stack(t) for t in ap], [stack(t) for t in rp]
        bm, km = [stack(t) for t in bp], [stack(t) for t in kp]
        lp = [jnp.where(strict, _dot_nt(am[i], bm[i]), 0.0) for i in n]
        lak = [jnp.where(strict, _dot_nt(am[i], km[i]), 0.0).astype(BF16) for i in n]
        grb = [jnp.where(incl, _dot_nt(rm[i], bm[i]), 0.0).astype(BF16) for i in n]
        grk = [jnp.where(incl, _dot_nt(rm[i], km[i]), 0.0).astype(BF16) for i in n]
        spb = [t.astype(BF16) for t in sp]
        a_s = [_dot_nt(ap[i].astype(BF16), spb[i]) for i in n]
        r_s = [_dot_nt(rp[i].astype(BF16), spb[i]) for i in n]
        vst = [jnp.concatenate([t, t], axis=0).astype(BF16) for t in vp]
        x = [jnp.concatenate([a_s[i], a_s[i]], axis=0) + _dot(lak[i], vst[i]) for i in n]
        for f in range(n_fac):
            lb = [t.astype(BF16) for t in lp]
            x = [x[i] + _split_dot_left(lb[i], x[i], 2) for i in n]
            if f + 1 < n_fac:
                lp = [_dot(t, t) for t in lb]
        yst = [jnp.concatenate([r_s[i], r_s[i]], axis=0) + _dot(grb[i], x[i].astype(BF16)) + _dot(grk[i], vst[i])
               for i in n]
        u = [jnp.where(first, t[:C], t[C:]) for t in x]
        ds = [_dot_tn(u[i].astype(BF16), bp[i].astype(BF16)) + _dot_tn(vp[i].astype(BF16), kp[i].astype(BF16))
              for i in n]
        for i, p in enumerate(ps):
            Ys[p] = jnp.where(first, yst[i][:C], yst[i][C:])
            S[p] = jnp.where(same_head, (sp[i] + ds[i]) * gc[i][0:1, :], 0.0)

    def group(gi, carry):
        pairs_step([gi * PAIR_UNROLL + uu for uu in range(PAIR_UNROLL)])
        return carry

    if n_pairs % PAIR_UNROLL == 0:
        lax.fori_loop(0, n_pairs // PAIR_UNROLL, group, 0)
    else:
        pairs_step(list(range(n_pairs)))

    for p in range(n_pairs):
        y_ref[0, :, p * LANE:(p + 1) * LANE] = Ys[p]

    @pl.when(c == pl.num_programs(1) - 1)
    def _():
        sout_ref[0] = S[...]


def _wkv_chunked(r, k, v, wpre, a, kkp, kap):
    g, t, d = r.shape
    C = CHUNK
    np_ = d // LANE
    row = pl.BlockSpec((1, C, d), lambda gi, ci: (gi, ci, 0))
    vec = pl.BlockSpec((1, d), lambda gi, ci: (0, 0))
    pair_buf = pltpu.VMEM((np_, C, LANE), F32)
    return pl.pallas_call(
        _wkv_chunk_kernel,
        out_shape=[jax.ShapeDtypeStruct((g, t, d), F32),
                   jax.ShapeDtypeStruct((g, np_, LANE, LANE), F32)],
        grid=(g, t // C),
        in_specs=[row] * 5 + [vec, vec],
        out_specs=[row, pl.BlockSpec((1, np_, LANE, LANE), lambda gi, ci: (gi, 0, 0, 0))],
        scratch_shapes=[pltpu.VMEM((np_, LANE, LANE), F32), pair_buf, pair_buf, pair_buf, pair_buf, pair_buf,
                        pltpu.VMEM((np_, SUBLANE, LANE), F32), pair_buf],
        compiler_params=_cp(2),
        name="wkv_chunk",
    )(r, k, v, wpre, a, kkp.reshape(1, d), kap.reshape(1, d))


def _wkv_vec_kernel(k_ref, w_ref, a_ref, r_ref, v_ref, kkp_ref, kap_ref,
                    d_ref, an_ref, b_ref, km_ref, rt_ref, vt_ref):
    a = a_ref[...]
    logd, kk, kmod = _wkv_terms(k_ref[...], w_ref[...], a, kkp_ref[...], kap_ref[...], _head_ones())
    d_ref[...] = jnp.exp(logd).T
    an_ref[...] = (-kk).T
    b_ref[...] = (kk * a).T
    km_ref[...] = kmod.T
    rt_ref[...] = r_ref[...].T
    vt_ref[...] = v_ref[...].T


def _wkv_vec(k, wpre, a, r, v, kkp, kap):
    b, d = k.shape
    full = pl.BlockSpec((b, d), lambda i: (0, 0))
    fullt = pl.BlockSpec((d, b), lambda i: (0, 0))
    vec = pl.BlockSpec((1, d), lambda i: (0, 0))
    return pl.pallas_call(
        _wkv_vec_kernel,
        out_shape=[jax.ShapeDtypeStruct((d, b), F32)] * 6,
        grid=(1,),
        in_specs=[full] * 5 + [vec, vec],
        out_specs=[fullt] * 6,
        compiler_params=_cp(1),
        name="wkv_vec",
    )(k, wpre, a, r, v, kkp.reshape(1, d), kap.reshape(1, d))


def _wkv_step_kernel(s_ref, d_ref, a_ref, b_ref, k_ref, r_ref, v_ref, sn_ref, y_ref):
    s = s_ref[...]
    key = lambda ref: ref[...][:, None]
    sa = jnp.sum(s * key(a_ref), axis=2, keepdims=True)
    sn = s * key(d_ref) + sa * key(b_ref) + v_ref[...] * key(k_ref)
    sn_ref[...] = sn
    y_ref[...] = jnp.sum(sn * key(r_ref), axis=2, keepdims=True)


def _wkv_step(s, dvec, an, bvec, kmod, r, v):
    h, n, _, b = s.shape
    hb = _pick(h, 2, 1)
    sblk = pl.BlockSpec((hb, n, n, b), lambda i: (i, 0, 0, 0))
    kblk = pl.BlockSpec((hb, n, b), lambda i: (i, 0, 0))
    vblk = pl.BlockSpec((hb, n, 1, b), lambda i: (i, 0, 0, 0))
    kv = lambda x: x.reshape(h, n, b)
    sn, y = pl.pallas_call(
        _wkv_step_kernel,
        out_shape=[jax.ShapeDtypeStruct((h, n, n, b), F32), jax.ShapeDtypeStruct((h, n, 1, b), F32)],
        grid=(h // hb,),
        in_specs=[sblk] + [kblk] * 5 + [vblk],
        out_specs=[sblk, vblk],
        compiler_params=_cp(1),
        name="wkv_step",
    )(s, kv(dvec), kv(an), kv(bvec), kv(kmod), kv(r), v.reshape(h, n, 1, b))
    return sn, y.reshape(h * n, b)


def _wkv_post_kernel(y_ref, r_ref, k_ref, v_ref, a_ref, g_ref, kap_ref, rk_ref, lg_ref, lb_ref, o_ref):
    ones_bd = _head_ones()
    y = y_ref[0]
    inv_n = 1.0 / HEAD
    mu = _head_sum(y, ones_bd) * inv_n
    yc = y - mu
    var = _head_sum(yc * yc, ones_bd) * inv_n
    yn = yc * lax.rsqrt(var + LNX_EPS) * lg_ref[...] + lb_ref[...]
    kmod = k_ref[0] * (1.0 + (a_ref[0] - 1.0) * kap_ref[...])
    bonus = _head_sum(r_ref[0] * kmod * rk_ref[...], ones_bd) * v_ref[0]
    o_ref[0] = ((yn + bonus) * g_ref[0]).astype(BF16)


def _wkv_post(y, r, k, v, a, g, kap, rk, lnx_g, lnx_b):
    gg, t, d = y.shape
    tr = _pick(t, 256, 16)
    row = pl.BlockSpec((1, tr, d), lambda gi, ti: (gi, ti, 0))
    vec = pl.BlockSpec((1, d), lambda gi, ti: (0, 0))
    return pl.pallas_call(
        _wkv_post_kernel,
        out_shape=jax.ShapeDtypeStruct((gg, t, d), BF16),
        grid=(gg, t // tr),
        in_specs=[row] * 6 + [vec] * 4,
        out_specs=row,
        compiler_params=_cp(2),
        name="wkv_post",
    )(y, r, k, v, a, g, kap.reshape(1, d), rk.reshape(1, d), lnx_g.reshape(1, d), lnx_b.reshape(1, d))


def _lru_conv_kernel(x_ref, w_ref, b_ref, buf_ref, xc_ref, *, width):
    acc = b_ref[...] + x_ref[0] * w_ref[width - 1:width, :]
    for s in range(1, width):
        acc = acc + buf_ref[0, :, width - 1 - s, :] * w_ref[width - 1 - s:width - s, :]
    xc_ref[0] = acc


def _lru_conv(proj, w, b, buf):
    g, r, d2 = proj.shape
    d = d2 // 2
    width = w.shape[0]
    tr = _pick(r, 256, 16)
    row = pl.BlockSpec((1, tr, d), lambda gi, t: (gi, t, 0))
    return pl.pallas_call(
        functools.partial(_lru_conv_kernel, width=width),
        out_shape=jax.ShapeDtypeStruct((g, r, d), F32),
        grid=(g, r // tr),
        in_specs=[pl.BlockSpec((1, tr, d), lambda gi, t: (gi, t, 1)),
                  pl.BlockSpec((width, d), lambda gi, t: (0, 0)),
                  pl.BlockSpec((1, d), lambda gi, t: (0, 0)),
                  pl.BlockSpec((1, tr, width - 1, d), lambda gi, t: (gi, t, 0, 0))],
        out_specs=row,
        compiler_params=_cp(2),
        name="lru_conv",
    )(proj, w, b.reshape(1, d), buf)


def _lru_gates_kernel(x_ref, wa_ref, ba_ref, wx_ref, bx_ref, ga_ref, gx_ref):
    x = x_ref[...].astype(BF16)
    ga_ref[...] = _sigmoid(_dot(x, wa_ref[0].astype(BF16)) + ba_ref[...])
    gx_ref[...] = _sigmoid(_dot(x, wx_ref[0].astype(BF16)) + bx_ref[...])


def _lru_gates(xc, wa, ba, wx, bx):
    m, d = xc.shape
    nb, bw, _ = wa.shape
    tm = _pick(m, 1024, 8)
    xs = pl.BlockSpec((tm, bw), lambda n, i: (i, n))
    ws = pl.BlockSpec((1, bw, bw), lambda n, i: (n, 0, 0))
    bs = pl.BlockSpec((1, bw), lambda n, i: (0, n))
    return pl.pallas_call(
        _lru_gates_kernel,
        out_shape=[jax.ShapeDtypeStruct((m, d), F32)] * 2,
        grid=(nb, m // tm),
        in_specs=[xs, ws, bs, ws, bs],
        out_specs=[xs, xs],
        compiler_params=_cp(2),
        name="lru_gates",
    )(xc, wa, ba.reshape(1, d), wx, bx.reshape(1, d))


def _lru_coeffs(ga, gx, xc, lam):
    log_a = -LRU_C * ga * _softplus(-lam)
    a = jnp.exp(log_a)
    mult = jnp.sqrt(1.0 - jnp.exp(2.0 * log_a))
    return a, mult


def _lru_seq_kernel(xin_ref, y_ref, cw_ref, cb_ref, wa_ref, ba_ref, wx_ref, bx_ref, lam_ref, h0_ref, buf_ref,
                    o_ref, hl_ref, nbuf_ref, cconv, carry, abuf, bbuf, hbuf, *, width):
    t = pl.program_id(2)
    tr = xin_ref.shape[1]

    @pl.when(t == 0)
    def _():
        cconv[...] = buf_ref[0]
        carry[...] = jnp.broadcast_to(h0_ref[0], carry.shape)

    x = xin_ref[0]
    c8 = cconv[...]
    xc = cb_ref[...] + x * cw_ref[width - 1:width, :]
    for s in range(1, width):
        xc = xc + _shift_rows(x, c8, s) * cw_ref[width - 1 - s:width - s, :]
    cconv[...] = x[tr - SUBLANE:tr]
    nbuf_ref[0] = x[tr - SUBLANE:tr]
    xb = xc.astype(BF16)
    ga = _sigmoid(_dot(xb, wa_ref[...].astype(BF16)) + ba_ref[...])
    gx = _sigmoid(_dot(xb, wx_ref[...].astype(BF16)) + bx_ref[...])
    a, mult = _lru_coeffs(ga, gx, xc, lam_ref[...])
    row = lax.broadcasted_iota(jnp.int32, a.shape, 0)
    mult = jnp.where(jnp.logical_and(t == 0, row == 0), 1.0, mult)
    abuf[...] = a
    bbuf[...] = mult * gx * xc
    r8 = lax.broadcasted_iota(jnp.int32, (SUBLANE, a.shape[1]), 0)

    def body(i, c):
        off = pl.multiple_of(i * SUBLANE, SUBLANE)
        aa = abuf[pl.ds(off, SUBLANE), :]
        bb = bbuf[pl.ds(off, SUBLANE), :]
        for s in (1, 2, 4):
            a_s = jnp.where(r8 >= s, pltpu.roll(aa, s, 0), 1.0)
            b_s = jnp.where(r8 >= s, pltpu.roll(bb, s, 0), 0.0)
            bb = aa * b_s + bb
            aa = aa * a_s
        h = aa * c + bb
        hbuf[pl.ds(off, SUBLANE), :] = h
        return h[SUBLANE - 1:SUBLANE, :]

    c = lax.fori_loop(0, tr // SUBLANE, body, carry[0:1, :], unroll=LRU_SCAN_UNROLL)
    carry[...] = jnp.broadcast_to(c, carry.shape)
    o_ref[0] = (hbuf[...] * _gelu(y_ref[0])).astype(BF16)

    @pl.when(t == pl.num_programs(2) - 1)
    def _():
        hl_ref[0] = c


def _lru_seq(proj, cw, cb, wa, ba, wx, bx, lam, h0, buf):
    g, r, d2 = proj.shape
    d = d2 // 2
    nb, bw, _ = wa.shape
    width = cw.shape[0]
    tr = _pick(r, 512, 16)
    row = lambda off: pl.BlockSpec((1, tr, bw), lambda gi, n, t: (gi, t, n + off))
    vec = pl.BlockSpec((1, bw), lambda gi, n, t: (0, n))
    wsp = pl.BlockSpec((None, bw, bw), lambda gi, n, t: (n, 0, 0))
    st = pl.BlockSpec((1, 1, bw), lambda gi, n, t: (gi, 0, n))
    st8 = pl.BlockSpec((1, SUBLANE, bw), lambda gi, n, t: (gi, 0, n))
    tile = pltpu.VMEM((tr, bw), F32)
    small = pltpu.VMEM((SUBLANE, bw), F32)
    return pl.pallas_call(
        functools.partial(_lru_seq_kernel, width=width),
        out_shape=[jax.ShapeDtypeStruct((g, r, d), BF16), jax.ShapeDtypeStruct((g, 1, d), F32),
                   jax.ShapeDtypeStruct((g, SUBLANE, d), F32)],
        grid=(g, nb, r // tr),
        in_specs=[row(nb), row(0), pl.BlockSpec((width, bw), lambda gi, n, t: (0, n)), vec,
                  wsp, vec, wsp, vec, vec, st, st8],
        out_specs=[row(0), st, st8],
        scratch_shapes=[small, small, tile, tile, tile],
        compiler_params=_cp(3),
        name="lru_seq",
    )(proj, proj, cw, cb.reshape(1, d), wa, ba.reshape(1, d), wx, bx.reshape(1, d), lam.reshape(1, d), h0, buf)


def _lru_step_kernel(ga_ref, gx_ref, xc_ref, y_ref, lam_ref, h0_ref, o_ref, h_ref):
    xc = xc_ref[0]
    a, mult = _lru_coeffs(ga_ref[0], gx_ref[0], xc, lam_ref[...])
    h = a * h0_ref[0] + mult * gx_ref[0] * xc
    h_ref[0] = h
    o_ref[0] = (h * _gelu(y_ref[0])).astype(BF16)


def _lru_step(ga, gx, xc, proj, lam, h0):
    g, r, d = ga.shape
    row = pl.BlockSpec((1, r, d), lambda i: (0, 0, 0))
    return pl.pallas_call(
        _lru_step_kernel,
        out_shape=[jax.ShapeDtypeStruct((g, r, d), BF16), jax.ShapeDtypeStruct((g, r, d), F32)],
        grid=(1,),
        in_specs=[row, row, row, row, pl.BlockSpec((1, d), lambda i: (0, 0)), row],
        out_specs=[row, row],
        compiler_params=_cp(1),
        name="lru_step",
    )(ga, gx, xc, proj, lam.reshape(1, d), h0)


def _ffn_up_kernel(*refs, seq, tiles_per_seq):
    if seq:
        x_ref, wg_ref, wu_ref, cw_ref, cb_ref, buf_ref, o_ref, nb_ref, wgb, wub, carry = refs
    else:
        x_ref, wg_ref, wu_ref, cw_ref, cb_ref, buf_ref, o_ref, nb_ref, wgb, wub = refs
    i = pl.program_id(1)

    @pl.when(i == 0)
    def _():
        wgb[...] = wg_ref[...].astype(BF16)
        wub[...] = wu_ref[...].astype(BF16)

    tm = x_ref.shape[0]
    ts = _pick(tm, FFN_SUB_ROWS, 16)
    if seq:
        @pl.when(i % tiles_per_seq == 0)
        def _():
            carry[...] = buf_ref[0]

        c8 = carry[...]
    for s0 in range(0, tm, ts):
        rows = slice(s0, s0 + ts)
        x = x_ref[rows, :]
        gt = _dot(x, wgb[...])
        u = _dot(x, wub[...])
        if seq:
            p1 = _shift_rows(gt, c8, 1)
            p2 = _shift_rows(gt, c8, 2)
            c8 = gt[ts - SUBLANE:ts]
        else:
            p2 = buf_ref[rows, 0, :]
            p1 = buf_ref[rows, 1, :]
            nb_ref[rows, 0, :] = p1
            nb_ref[rows, 1, :] = gt
        gc = cb_ref[...] + p2 * cw_ref[0:1, :] + p1 * cw_ref[1:2, :] + gt * cw_ref[2:3, :]
        o_ref[rows, :] = (_gelu(gc) * u).astype(BF16)
    if seq:
        carry[...] = c8
        nb_ref[0] = c8


def _ffn_up(x, wg, wu, layer, cw, cb, buf, seq, rows_per_seq):
    m, k = x.shape
    f = wg.shape[2]
    tn = _pick(f, 512, LANE)
    tm = _pick(rows_per_seq if seq else m, 1024, 16)
    nj = f // tn
    xs = pl.BlockSpec((tm, k), lambda j, i: (i, 0))
    ws = _wspec(wg, layer, k, tn, lambda j, i: j)
    os_ = pl.BlockSpec((tm, tn), lambda j, i: (i, j))
    in_specs = [xs, ws, ws, pl.BlockSpec((3, tn), lambda j, i: (0, j)), pl.BlockSpec((1, tn), lambda j, i: (0, j))]
    scratch = [pltpu.VMEM((k, tn), BF16), pltpu.VMEM((k, tn), BF16)]
    if seq:
        tps = rows_per_seq // tm
        g = m // rows_per_seq
        bspec = pl.BlockSpec((1, SUBLANE, tn), lambda j, i: (i // tps, 0, j))
        in_specs.append(bspec)
        out_shape = [jax.ShapeDtypeStruct((m, f), BF16), jax.ShapeDtypeStruct((g, SUBLANE, f), F32)]
        out_specs = [os_, bspec]
        scratch.append(pltpu.VMEM((SUBLANE, tn), F32))
    else:
        tps = 1
        hspec = pl.BlockSpec((tm, 2, tn), lambda j, i: (i, 0, j))
        in_specs.append(hspec)
        out_shape = [jax.ShapeDtypeStruct((m, f), BF16), jax.ShapeDtypeStruct((m, 2, f), F32)]
        out_specs = [os_, hspec]
    args = [x, wg, wu, cw, cb.reshape(1, f), buf]
    return pl.pallas_call(
        functools.partial(_ffn_up_kernel, seq=seq, tiles_per_seq=tps),
        out_shape=out_shape,
        grid=(nj, m // tm),
        in_specs=in_specs,
        out_specs=out_specs,
        scratch_shapes=scratch,
        compiler_params=_cp(2),
        name="ffn_up",
    )(*args)


def _trunk(x, mods, st, seq, P):
    g, r, d = x.shape
    m = g * r
    depth = P["w_mod"].shape[0]
    o_wkv, o_shift, o_h, o_lconv, o_fconv = [], [], [], [], []
    flat = lambda t: t.reshape(m, t.shape[-1])
    unflat = lambda t: t.reshape(g, r, t.shape[-1])
    h_in = None
    for i in range(depth):
        mod = mods[i]
        j = i // 2
        if i % 2 == 0:
            if seq:
                shift = jnp.zeros((g, 1, d), F32)
            else:
                shift = st["shift"][j].reshape(1, r, d)
            mixes, shift_new = _rwkv_prep(x, P["norm_g"][i, 0], mod, P["rw_mix"][j], shift, seq)
            xr, xw, xk, xv, xa, xg = [flat(t) for t in mixes]
            rr = _mm(xr, P["rw_wr"], j, name="rw_r")
            kk = _mm(xk, P["rw_wk"], j, name="rw_k")
            vv = _mm(xv, P["rw_wv"], j, name="rw_v")
            wpre = _lora(xw, P["rw_w1"][j], P["rw_w2"][j], P["rw_w0"][j], mid_act="tanh", out_act=None, name="rw_w")
            aa = _lora(xa, P["rw_a1"][j], P["rw_a2"][j], P["rw_a0"][j], mid_act=None, out_act="sigmoid", name="rw_a")
            gg = _lora(xg, P["rw_g1"][j], P["rw_g2"][j], None, mid_act="sigmoid", out_act=None, name="rw_g")
            if seq:
                y, s_bd = _wkv_chunked(unflat(rr), unflat(kk), unflat(vv), unflat(wpre), unflat(aa),
                                       P["rw_kk"][j], P["rw_ka"][j])
                npairs = d // LANE
                s_new = jnp.stack([s_bd[:, :, :HEAD, :HEAD], s_bd[:, :, HEAD:, HEAD:]], axis=2)
                s_new = s_new.reshape(g, 2 * npairs, HEAD, HEAD)
                o_shift.append(shift_new.reshape(g, d))
            else:
                s_t = jnp.transpose(st["wkv"][j], (1, 2, 3, 0))
                vecs = _wkv_vec(kk, wpre, aa, rr, vv, P["rw_kk"][j], P["rw_ka"][j])
                s_new, y = _wkv_step(s_t, *vecs)
                s_new = jnp.transpose(s_new, (3, 0, 1, 2))
                y = y.T.reshape(1, r, d)
                o_shift.append(shift_new.reshape(r, d))
            o_wkv.append(s_new)
            xo = _wkv_post(y, unflat(rr), unflat(kk), unflat(vv), unflat(aa), unflat(gg),
                           P["rw_ka"][j], P["rw_rk"][j], P["rw_lnx_g"][j], P["rw_lnx_b"][j])
            out = _mm(flat(xo), P["rw_wo"], j, name="rw_o")
        else:
            proj = _mm(flat(h_in), P["lru_w_in"], j, P["lru_b_in"][j], name="lru_in")
            dr = proj.shape[1] // 2
            width = P["lru_conv_w"].shape[1]
            if seq:
                hy, hl, nb8 = _lru_seq(unflat(proj), P["lru_conv_w"][j], P["lru_conv_b"][j],
                                       P["lru_wa"][j], P["lru_ba"][j], P["lru_wx"][j], P["lru_bx"][j],
                                       P["lru_lambda"][j], jnp.zeros((g, 1, dr), F32),
                                       jnp.zeros((g, SUBLANE, dr), F32))
                o_lconv.append(nb8[:, SUBLANE - (width - 1):, :])
                o_h.append(hl.reshape(g, dr))
            else:
                hist = st["lru_conv"][j]
                xc = _lru_conv(unflat(proj), P["lru_conv_w"][j], P["lru_conv_b"][j],
                               hist.reshape(1, r, width - 1, dr))
                o_lconv.append(jnp.concatenate([hist[:, 1:], proj[:, None, dr:]], axis=1))
                ga, gx = _lru_gates(flat(xc), P["lru_wa"][j], P["lru_ba"][j], P["lru_wx"][j], P["lru_bx"][j])
                hy, hl = _lru_step(unflat(ga), unflat(gx), xc, unflat(proj), P["lru_lambda"][j],
                                   st["lru_h"][j].reshape(1, r, dr))
                o_h.append(hl.reshape(r, dr))
            out = _mm(flat(hy), P["lru_w_out"], j, name="lru_out")
        x, hf = _resid_norm(x, unflat(out), P["norm_g"][i, 1], mod, 2, (P["norm_g"][i, 2], mod, 4, 3),
                            name="resid_mix")
        fdim = P["ffn_w_gate"].shape[2]
        if seq:
            hact, nb8 = _ffn_up(flat(hf), P["ffn_w_gate"], P["ffn_w_up"], i, P["ffn_conv_w"][i],
                                P["ffn_conv_b"][i], jnp.zeros((g, SUBLANE, fdim), F32), True, r)
            o_fconv.append(nb8[:, SUBLANE - 2:, :])
        else:
            hact, nhist = _ffn_up(flat(hf), P["ffn_w_gate"], P["ffn_w_up"], i, P["ffn_conv_w"][i],
                                  P["ffn_conv_b"][i], st["ffn_conv"][i], False, r)
            o_fconv.append(nhist)
        f = _mm(hact, P["ffn_w_down"], i, tm_target=512, tn_target=512, name="ffn_down")
        if i + 1 < depth:
            nmod = mods[i + 1]
            if (i + 1) % 2 == 0:
                x = _resid_norm(x, unflat(f), P["norm_g"][i, 3], mod, 5, name="resid_ffn")
            else:
                x, h_in = _resid_norm(x, unflat(f), P["norm_g"][i, 3], mod, 5,
                                      (P["norm_g"][i + 1, 0], nmod, 1, 0), name="resid_ffn")
        else:
            x = _resid_norm(x, unflat(f), P["norm_g"][i, 3], mod, 5, name="resid_ffn")
    return x, jnp.stack(o_wkv), jnp.stack(o_shift), jnp.stack(o_h), jnp.stack(o_lconv), jnp.stack(o_fconv)


def kernel(x_prompt, x_sample, c_prompt, c_sample, state_rwkv_wkv, state_rwkv_shift, state_lru_h,
           state_lru_conv, state_ffn_conv, w_mod, b_mod, norm_g, rw_mix, rw_wr, rw_wk, rw_wv, rw_wo,
           rw_w0, rw_w1, rw_w2, rw_a0, rw_a1, rw_a2, rw_g1, rw_g2, rw_kk, rw_ka, rw_rk, rw_lnx_g,
           rw_lnx_b, lru_w_in, lru_b_in, lru_conv_w, lru_conv_b, lru_wa, lru_ba, lru_wx, lru_bx,
           lru_lambda, lru_w_out, ffn_w_gate, ffn_w_up, ffn_w_down, ffn_conv_w, ffn_conv_b):
    P = dict(w_mod=w_mod, b_mod=b_mod, norm_g=norm_g, rw_mix=rw_mix, rw_wr=rw_wr, rw_wk=rw_wk,
             rw_wv=rw_wv, rw_wo=rw_wo, rw_w0=rw_w0, rw_w1=rw_w1, rw_w2=rw_w2, rw_a0=rw_a0, rw_a1=rw_a1,
             rw_a2=rw_a2, rw_g1=rw_g1, rw_g2=rw_g2, rw_kk=rw_kk, rw_ka=rw_ka, rw_rk=rw_rk,
             rw_lnx_g=rw_lnx_g, rw_lnx_b=rw_lnx_b, lru_w_in=lru_w_in, lru_b_in=lru_b_in,
             lru_conv_w=lru_conv_w, lru_conv_b=lru_conv_b, lru_wa=lru_wa, lru_ba=lru_ba, lru_wx=lru_wx,
             lru_bx=lru_bx, lru_lambda=lru_lambda, lru_w_out=lru_w_out, ffn_w_gate=ffn_w_gate,
             ffn_w_up=ffn_w_up, ffn_w_down=ffn_w_down, ffn_conv_w=ffn_conv_w, ffn_conv_b=ffn_conv_b)
    bp, t, d = x_prompt.shape
    bs = x_sample.shape[0]
    depth = w_mod.shape[0]
    c_all = jnp.concatenate([c_prompt, c_sample], axis=0)
    rows = c_all.shape[0]
    rows_p = -(-rows // 16) * 16
    c_all = jnp.pad(c_all, ((0, rows_p - rows), (0, 0)))
    mods_p, mods_s = [], []
    for i in range(depth):
        mod = _mm(c_all, w_mod, i, b_mod[i], pre_act="silu", name="mod")
        mods_p.append(mod[:bp].reshape(bp, 1, N_MOD * d))
        mods_s.append(mod[bp:bp + bs].reshape(1, bs, N_MOD * d))
    y_p, p_wkv, p_shift, p_h, p_lconv, p_fconv = _trunk(x_prompt, mods_p, None, True, P)
    st = dict(wkv=state_rwkv_wkv, shift=state_rwkv_shift, lru_h=state_lru_h,
              lru_conv=state_lru_conv, ffn_conv=state_ffn_conv)
    y_s, s_wkv, s_shift, s_h, s_lconv, s_fconv = _trunk(x_sample.reshape(1, bs, d), mods_s, st, False, P)
    return (y_p, y_s.reshape(bs, 1, d), p_wkv, p_shift, p_h, p_lconv, p_fconv,
            s_wkv, s_shift, s_h, s_lconv, s_fconv)
```

```python
import functools

import jax
import jax.numpy as jnp
from jax import lax
from jax.experimental import pallas as pl
from jax.experimental.pallas import tpu as pltpu

F32 = jnp.float32
BF16 = jnp.bfloat16

NORM_EPS = 1e-6
LNX_EPS = 64e-5
LRU_C = 8.0
N_MOD = 6
HEAD = 64
LANE = 128
SUBLANE = 8
CHUNK = 64
PAIR_UNROLL = 16
LRU_SCAN_UNROLL = 4
MM_RESID_SUB_ROWS = 128
FFN_SUB_ROWS = 128
VMEM_LIMIT = 52 * 1024 * 1024


def _cp(n_axes):
    return pltpu.CompilerParams(dimension_semantics=("arbitrary",) * n_axes,
                                vmem_limit_bytes=VMEM_LIMIT)


def _pick(n, target, mult):
    best = None
    for t in range(mult, min(n, target) + 1, mult):
        if n % t == 0:
            best = t
    return best if best is not None else n


def _softplus(z):
    return jnp.maximum(z, 0.0) + jnp.log(1.0 + jnp.exp(-jnp.abs(z)))


def _sigmoid(z):
    return 1.0 / (1.0 + jnp.exp(-z))


def _gelu(x):
    return 0.5 * x * (1.0 + jnp.tanh(0.7978845608028654 * (x + 0.044715 * (x * x * x))))


def _act(x, name):
    if name is None:
        return x
    if name == "tanh":
        return jnp.tanh(x)
    if name == "sigmoid":
        return _sigmoid(x)
    if name == "silu":
        return x * _sigmoid(x)
    raise ValueError(name)


def _rms(x, g):
    ms = jnp.mean(x * x, axis=-1, keepdims=True)
    return x * lax.rsqrt(ms + NORM_EPS) * g


def _dot(x, y):
    return jnp.dot(x, y, preferred_element_type=F32)


def _dot_nt(x, y):
    return lax.dot_general(x, y, (((1,), (1,)), ((), ())), preferred_element_type=F32)


def _dot_split(x, m, parts):
    acc = None
    rem = x
    for i in range(parts):
        piece = rem.astype(BF16)
        term = jnp.dot(piece, m, preferred_element_type=F32)
        acc = term if acc is None else acc + term
        if i + 1 < parts:
            rem = rem - piece.astype(F32)
    return acc


def _split_dot_left(m, x, parts):
    acc = None
    rem = x
    for i in range(parts):
        piece = rem.astype(BF16)
        term = jnp.dot(m, piece, preferred_element_type=F32)
        acc = term if acc is None else acc + term
        if i + 1 < parts:
            rem = rem - piece.astype(F32)
    return acc


def _dot_tn(x, y):
    return lax.dot_general(x, y, (((0,), (0,)), ((), ())), preferred_element_type=F32)


def _head_ones():
    r = lax.broadcasted_iota(jnp.int32, (LANE, LANE), 0)
    c = lax.broadcasted_iota(jnp.int32, (LANE, LANE), 1)
    return ((r < HEAD) == (c < HEAD)).astype(BF16)


def _head_sum(x, ones_bd):
    d = x.shape[-1]
    cols = [_dot_split(x[:, s:s + LANE], ones_bd, 2) for s in range(0, d, LANE)]
    return cols[0] if len(cols) == 1 else jnp.concatenate(cols, axis=-1)


def _shift_rows(x, carry8, s):
    n = x.shape[0]
    rolled = pltpu.roll(x, s, 0)
    rolled_c = pltpu.roll(carry8, s, 0)
    row8 = lax.broadcasted_iota(jnp.int32, (SUBLANE, x.shape[1]), 0)
    top = jnp.where(row8 < s, rolled_c, rolled[0:SUBLANE])
    if n == SUBLANE:
        return top
    return jnp.concatenate([top, rolled[SUBLANE:]], axis=0)


def _mm_kernel(*refs, has_bias, pre_act, act):
    if has_bias:
        x_ref, w_ref, b_ref, o_ref, wb_ref = refs
    else:
        x_ref, w_ref, o_ref, wb_ref = refs
        b_ref = None

    @pl.when(pl.program_id(1) == 0)
    def _():
        wb_ref[...] = w_ref[...].astype(BF16)

    x = x_ref[...]
    if pre_act is not None:
        x = _act(x.astype(F32), pre_act)
    acc = _dot(x.astype(BF16), wb_ref[...])
    if has_bias:
        acc = acc + b_ref[...]
    o_ref[...] = _act(acc, act).astype(o_ref.dtype)


def _wspec(w, layer, rows, cols, col_of):
    assert w.ndim == 3
    return pl.BlockSpec((None, rows, cols), lambda *ids: (layer, 0, col_of(*ids)))


def _mm(x, w, layer, bias=None, *, pre_act=None, act=None, out_dtype=F32, tm_target=1024, tn_target=1024,
        name="mm"):
    m, k = x.shape
    n = w.shape[2]
    tm = _pick(m, tm_target, 16)
    tn = _pick(n, tn_target, LANE)
    in_specs = [pl.BlockSpec((tm, k), lambda j, i: (i, 0)),
                _wspec(w, layer, k, tn, lambda j, i: j)]
    args = [x, w]
    if bias is not None:
        in_specs.append(pl.BlockSpec((1, tn), lambda j, i: (0, j)))
        args.append(bias.reshape(1, n))
    return pl.pallas_call(
        functools.partial(_mm_kernel, has_bias=bias is not None, pre_act=pre_act, act=act),
        out_shape=jax.ShapeDtypeStruct((m, n), out_dtype),
        grid=(n // tn, m // tm),
        in_specs=in_specs,
        out_specs=pl.BlockSpec((tm, tn), lambda j, i: (i, j)),
        scratch_shapes=[pltpu.VMEM((k, tn), BF16)],
        compiler_params=_cp(2),
        name=name,
    )(*args)


def _mod_spec(mod, slot, d, tr):
    rm = mod.shape[1]
    if rm == 1:
        return pl.BlockSpec((1, 1, d), lambda g, t: (g, 0, slot))
    return pl.BlockSpec((1, tr, d), lambda g, t: (g, t, slot))


def _resid_norm_kernel(*refs, with_next):
    if with_next:
        x_ref, o_ref, g1_ref, gate_ref, g2_ref, sc_ref, sh_ref, xn_ref, h_ref = refs
    else:
        x_ref, o_ref, g1_ref, gate_ref, xn_ref = refs
    xn = x_ref[0] + gate_ref[0] * _rms(o_ref[0], g1_ref[...])
    xn_ref[0] = xn
    if with_next:
        h_ref[0] = (_rms(xn, g2_ref[...]) * (1.0 + sc_ref[0]) + sh_ref[0]).astype(h_ref.dtype)


def _resid_norm(x, o, g1, mod, gate_slot, nxt=None, name="resid_norm"):
    g, r, d = x.shape
    tr = _pick(r, 256, 16)
    row = pl.BlockSpec((1, tr, d), lambda gi, t: (gi, t, 0))
    vec = pl.BlockSpec((1, d), lambda gi, t: (0, 0))
    in_specs = [row, row, vec, _mod_spec(mod, gate_slot, d, tr)]
    args = [x, o, g1.reshape(1, d), mod]
    out_shape = [jax.ShapeDtypeStruct((g, r, d), F32)]
    out_specs = [row]
    if nxt is not None:
        g2, mod2, sc_slot, sh_slot = nxt
        in_specs += [vec, _mod_spec(mod2, sc_slot, d, tr), _mod_spec(mod2, sh_slot, d, tr)]
        args += [g2.reshape(1, d), mod2, mod2]
        out_shape.append(jax.ShapeDtypeStruct((g, r, d), BF16))
        out_specs.append(row)
    res = pl.pallas_call(
        functools.partial(_resid_norm_kernel, with_next=nxt is not None),
        out_shape=out_shape,
        grid=(g, r // tr),
        in_specs=in_specs,
        out_specs=out_specs,
        compiler_params=_cp(2),
        name=name,
    )(*args)
    return res if nxt is not None else res[0]


def _mm_resid_kernel(a_ref, w_ref, x_ref, g1_ref, gate_ref, g2_ref, sc_ref, sh_ref, xn_ref, h_ref):
    tm = a_ref.shape[1]
    ts = _pick(tm, MM_RESID_SUB_ROWS, 16)

    def rows_of(ref, rows):
        return ref[0] if ref.shape[1] == 1 else ref[0, rows, :]

    for s0 in range(0, tm, ts):
        rows = slice(s0, s0 + ts)
        o = _dot(a_ref[0, rows, :], w_ref[...])
        xn = x_ref[0, rows, :] + rows_of(gate_ref, rows) * _rms(o, g1_ref[...])
        xn_ref[0, rows, :] = xn
        h_ref[0, rows, :] = (_rms(xn, g2_ref[...]) * (1.0 + rows_of(sc_ref, rows))
                             + rows_of(sh_ref, rows)).astype(BF16)


def _mm_resid(a, w, layer, x, g1, mod, gate_slot, g2, sc_slot, sh_slot, name):
    g, r, k = a.shape
    n = w.shape[2]
    wb = w[layer].astype(BF16)
    tm = _pick(r, 256, 16)
    row = lambda c: pl.BlockSpec((1, tm, c), lambda gi, t: (gi, t, 0))
    vec = pl.BlockSpec((1, n), lambda gi, t: (0, 0))
    return pl.pallas_call(
        _mm_resid_kernel,
        out_shape=[jax.ShapeDtypeStruct((g, r, n), F32), jax.ShapeDtypeStruct((g, r, n), BF16)],
        grid=(g, r // tm),
        in_specs=[row(k), pl.BlockSpec((k, n), lambda gi, t: (0, 0)), row(n), vec,
                  _mod_spec(mod, gate_slot, n, tm), vec, _mod_spec(mod, sc_slot, n, tm),
                  _mod_spec(mod, sh_slot, n, tm)],
        out_specs=[row(n), row(n)],
        compiler_params=_cp(2),
        name=name,
    )(a, wb, x, g1.reshape(1, n), mod, g2.reshape(1, n), mod, mod)


def _rwkv_prep_kernel(x_ref, g_ref, sc_ref, sh_ref, mix_ref, shift_ref,
                      w1_ref, w2_ref, w0_ref, a1_ref, a2_ref, a0_ref, g1_ref, g2_ref,
                      xr_ref, xk_ref, xv_ref, wpre_ref, a_ref, gate_ref, shift_out, *scratch, seq):
    h = _rms(x_ref[0], g_ref[...]) * (1.0 + sc_ref[0]) + sh_ref[0]
    tr = h.shape[0]
    if seq:
        carry = scratch[0]
        t = pl.program_id(1)

        @pl.when(t == 0)
        def _():
            carry[...] = jnp.broadcast_to(shift_ref[0], carry.shape)

        hprev = _shift_rows(h, carry[...], 1)
        carry[...] = h[tr - SUBLANE:tr]

        @pl.when(t == pl.num_programs(1) - 1)
        def _():
            shift_out[0] = h[tr - 1:tr]
    else:
        hprev = shift_ref[0]
        shift_out[0] = h
    xx = hprev - h

    def mixed(j):
        return (h + xx * mix_ref[j:j + 1, :]).astype(BF16)

    xr_ref[0] = mixed(0)
    xk_ref[0] = mixed(2)
    xv_ref[0] = mixed(3)
    wpre_ref[0] = w0_ref[...] + _dot(jnp.tanh(_dot(mixed(1), w1_ref[...])).astype(BF16), w2_ref[...])
    a_ref[0] = _sigmoid(a0_ref[...] + _dot(_dot(mixed(4), a1_ref[...]).astype(BF16), a2_ref[...]))
    gate_ref[0] = _dot(_sigmoid(_dot(mixed(5), g1_ref[...])).astype(BF16), g2_ref[...])


def _lora_pair(w1, w2):
    r = w1.shape[1]
    rp = -(-r // LANE) * LANE
    if rp != r:
        w1 = jnp.pad(w1, ((0, 0), (0, rp - r)))
        w2 = jnp.pad(w2, ((0, rp - r), (0, 0)))
    return w1.astype(BF16), w2.astype(BF16)


def _rwkv_prep(x, g, mod, mix, shift, seq, lora):
    gg, r, d = x.shape
    tr = _pick(r, 256, 16)
    w1, w2, w0, a1, a2, a0, g1, g2 = lora
    w1, w2 = _lora_pair(w1, w2)
    a1, a2 = _lora_pair(a1, a2)
    g1, g2 = _lora_pair(g1, g2)
    row = pl.BlockSpec((1, tr, d), lambda gi, t: (gi, t, 0))
    vec = pl.BlockSpec((1, d), lambda gi, t: (0, 0))
    full = lambda w: pl.BlockSpec(w.shape, lambda gi, t: (0, 0))
    if seq:
        shift_spec = pl.BlockSpec((1, 1, d), lambda gi, t: (gi, 0, 0))
        shift_shape = (gg, 1, d)
        scratch = [pltpu.VMEM((SUBLANE, d), F32)]
    else:
        shift_spec = row
        shift_shape = (gg, r, d)
        scratch = []
    act = lambda dt: jax.ShapeDtypeStruct((gg, r, d), dt)
    outs = pl.pallas_call(
        functools.partial(_rwkv_prep_kernel, seq=seq),
        out_shape=[act(BF16)] * 3 + [act(F32)] * 3 + [jax.ShapeDtypeStruct(shift_shape, F32)],
        grid=(gg, r // tr),
        in_specs=[row, vec, _mod_spec(mod, 1, d, tr), _mod_spec(mod, 0, d, tr),
                  pl.BlockSpec((6, d), lambda gi, t: (0, 0)), shift_spec,
                  full(w1), full(w2), vec, full(a1), full(a2), vec, full(g1), full(g2)],
        out_specs=[row] * 6 + [shift_spec],
        scratch_shapes=scratch,
        compiler_params=_cp(2),
        name="rwkv_prep",
    )(x, g.reshape(1, d), mod, mod, mix, shift, w1, w2, w0.reshape(1, d), a1, a2, a0.reshape(1, d), g1, g2)
    return outs[:6], outs[6]


def _wkv_terms(k, wpre, a, kkp, kap, ones_bd):
    logd = -0.6065306597126334 * _sigmoid(wpre)
    kk = k * kkp
    kk = kk * lax.rsqrt(jnp.maximum(_head_sum(kk * kk, ones_bd), 1e-24))
    kmod = k * (1.0 + (a - 1.0) * kap)
    return logd, kk, kmod


def _wkv_chunk_kernel(r_ref, k_ref, v_ref, w_ref, a_ref, kkp_ref, kap_ref, y_ref, sout_ref,
                      S, Rt, At, Bt, Kt, Vs, Gc, Ys):
    c = pl.program_id(1)
    n_pairs = S.shape[0]
    C = r_ref.shape[1]

    @pl.when(c == 0)
    def _():
        S[...] = jnp.zeros(S.shape, F32)

    ones_bd = _head_ones()
    a = a_ref[0]
    logd, kk, kmod = _wkv_terms(k_ref[0], w_ref[0], a, kkp_ref[...], kap_ref[...], ones_bd)
    ri = lax.broadcasted_iota(jnp.int32, (C, C), 0)
    ci = lax.broadcasted_iota(jnp.int32, (C, C), 1)
    tri = (ri >= ci).astype(BF16)
    cum = _split_dot_left(tri, logd, 3)
    gam = jnp.exp(cum)
    rt = r_ref[0] * gam
    at = -kk * jnp.exp(cum - logd)
    ginv = jnp.exp(-cum)
    bt = kk * a * ginv
    kt = kmod * ginv
    v = v_ref[0]
    for p in range(n_pairs):
        sl = slice(p * LANE, (p + 1) * LANE)
        Rt[p] = rt[:, sl]
        At[p] = at[:, sl]
        Bt[p] = bt[:, sl]
        Kt[p] = kt[:, sl]
        Vs[p] = v[:, sl]
        Gc[p] = jnp.broadcast_to(gam[C - 1:C, sl], (SUBLANE, LANE))

    lane = lax.broadcasted_iota(jnp.int32, (1, LANE), 1)
    first = lane < HEAD
    r2 = lax.broadcasted_iota(jnp.int32, (2 * C, 2 * C), 0)
    c2 = lax.broadcasted_iota(jnp.int32, (2 * C, 2 * C), 1)
    strict = r2 > c2
    incl = r2 >= c2
    rs = lax.broadcasted_iota(jnp.int32, (LANE, LANE), 0)
    cs = lax.broadcasted_iota(jnp.int32, (LANE, LANE), 1)
    same_head = (rs < HEAD) == (cs < HEAD)

    def stack(x):
        return jnp.concatenate([jnp.where(first, x, 0.0), jnp.where(first, 0.0, x)], axis=0).astype(BF16)

    n_fac = max(1, (C - 1).bit_length())

    def pairs_step(ps):
        n = range(len(ps))
        rp, ap, bp = [Rt[p] for p in ps], [At[p] for p in ps], [Bt[p] for p in ps]
        kp, vp = [Kt[p] for p in ps], [Vs[p] for p in ps]
        sp, gc = [S[p] for p in ps], [Gc[p] for p in ps]
        am, rm = [stack(t) for t in ap], [stack(t) for t in rp]
        bm, km = [stack(t) for t in bp], [stack(t) for t in kp]
        ar = [jnp.concatenate([am[i], rm[i]], axis=0) for i in n]
        gb = [_dot_nt(ar[i], bm[i]) for i in n]
        gk = [_dot_nt(ar[i], km[i]) for i in n]
        lp = [jnp.where(strict, t[:2 * C], 0.0) for t in gb]
        lak = [jnp.where(strict, t[:2 * C], 0.0).astype(BF16) for t in gk]
        grbk = [jnp.concatenate([jnp.where(incl, gb[i][2 * C:], 0.0), jnp.where(incl, gk[i][2 * C:], 0.0)],
                                axis=1).astype(BF16) for i in n]
        spb = [t.astype(BF16) for t in sp]
        ars = [_dot_nt(jnp.concatenate([ap[i], rp[i]], axis=0).astype(BF16), spb[i]) for i in n]
        a_s, r_s = [t[:C] for t in ars], [t[C:] for t in ars]
        vst = [jnp.concatenate([t, t], axis=0).astype(BF16) for t in vp]
        x = [jnp.concatenate([a_s[i], a_s[i]], axis=0) + _dot(lak[i], vst[i]) for i in n]
        for f in range(n_fac):
            lh = [t.astype(BF16) for t in lp]
            xh = [t.astype(BF16) for t in x]
            xl = [(x[i] - xh[i].astype(F32)).astype(BF16) for i in n]
            lhs = [jnp.concatenate([t, t], axis=1) for t in lh]
            if f + 1 < n_fac:
                ll = [(lp[i] - lh[i].astype(F32)).astype(BF16) for i in n]
                rhs = [jnp.concatenate([jnp.concatenate([xh[i], lh[i]], axis=1),
                                        jnp.concatenate([xl[i], ll[i]], axis=1)], axis=0) for i in n]
                prod = [_dot(lhs[i], rhs[i]) for i in n]
                x = [x[i] + prod[i][:, :LANE] for i in n]
                lp = [t[:, LANE:] for t in prod]
            else:
                x = [x[i] + _dot(lhs[i], jnp.concatenate([xh[i], xl[i]], axis=0)) for i in n]
        yst = [jnp.concatenate([r_s[i], r_s[i]], axis=0)
               + _dot(grbk[i], jnp.concatenate([x[i].astype(BF16), vst[i]], axis=0)) for i in n]
        u = [jnp.where(first, t[:C], t[C:]) for t in x]
        ds = [_dot_tn(jnp.concatenate([u[i], vp[i]], axis=0).astype(BF16),
                      jnp.concatenate([bp[i], kp[i]], axis=0).astype(BF16)) for i in n]
        for i, p in enumerate(ps):
            Ys[p] = jnp.where(first, yst[i][:C], yst[i][C:])
            S[p] = jnp.where(same_head, (sp[i] + ds[i]) * gc[i][0:1, :], 0.0)

    for p0 in range(0, n_pairs, PAIR_UNROLL):
        pairs_step(list(range(p0, min(p0 + PAIR_UNROLL, n_pairs))))

    for p in range(n_pairs):
        y_ref[0, :, p * LANE:(p + 1) * LANE] = Ys[p]

    @pl.when(c == pl.num_programs(1) - 1)
    def _():
        sout_ref[0] = S[...]


def _wkv_chunked(r, k, v, wpre, a, kkp, kap):
    g, t, d = r.shape
    C = CHUNK
    np_ = d // LANE
    row = pl.BlockSpec((1, C, d), lambda gi, ci: (gi, ci, 0))
    vec = pl.BlockSpec((1, d), lambda gi, ci: (0, 0))
    pair_buf = pltpu.VMEM((np_, C, LANE), F32)
    return pl.pallas_call(
        _wkv_chunk_kernel,
        out_shape=[jax.ShapeDtypeStruct((g, t, d), F32),
                   jax.ShapeDtypeStruct((g, np_, LANE, LANE), F32)],
        grid=(g, t // C),
        in_specs=[row] * 5 + [vec, vec],
        out_specs=[row, pl.BlockSpec((1, np_, LANE, LANE), lambda gi, ci: (gi, 0, 0, 0))],
        scratch_shapes=[pltpu.VMEM((np_, LANE, LANE), F32), pair_buf, pair_buf, pair_buf, pair_buf, pair_buf,
                        pltpu.VMEM((np_, SUBLANE, LANE), F32), pair_buf],
        compiler_params=_cp(2),
        name="wkv_chunk",
    )(r, k, v, wpre, a, kkp.reshape(1, d), kap.reshape(1, d))


def _wkv_vec_kernel(k_ref, w_ref, a_ref, r_ref, v_ref, kkp_ref, kap_ref,
                    d_ref, an_ref, b_ref, km_ref, rt_ref, vt_ref):
    a = a_ref[...]
    logd, kk, kmod = _wkv_terms(k_ref[...], w_ref[...], a, kkp_ref[...], kap_ref[...], _head_ones())
    d_ref[...] = jnp.exp(logd).T
    an_ref[...] = (-kk).T
    b_ref[...] = (kk * a).T
    km_ref[...] = kmod.T
    rt_ref[...] = r_ref[...].T
    vt_ref[...] = v_ref[...].T


def _wkv_vec(k, wpre, a, r, v, kkp, kap):
    b, d = k.shape
    full = pl.BlockSpec((b, d), lambda i: (0, 0))
    fullt = pl.BlockSpec((d, b), lambda i: (0, 0))
    vec = pl.BlockSpec((1, d), lambda i: (0, 0))
    return pl.pallas_call(
        _wkv_vec_kernel,
        out_shape=[jax.ShapeDtypeStruct((d, b), F32)] * 6,
        grid=(1,),
        in_specs=[full] * 5 + [vec, vec],
        out_specs=[fullt] * 6,
        compiler_params=_cp(1),
        name="wkv_vec",
    )(k, wpre, a, r, v, kkp.reshape(1, d), kap.reshape(1, d))


def _wkv_step_kernel(s_ref, d_ref, a_ref, b_ref, k_ref, r_ref, v_ref, sn_ref, y_ref):
    s = s_ref[...]
    key = lambda ref: ref[...][:, None]
    sa = jnp.sum(s * key(a_ref), axis=2, keepdims=True)
    sn = s * key(d_ref) + sa * key(b_ref) + v_ref[...] * key(k_ref)
    sn_ref[...] = sn
    y_ref[...] = jnp.sum(sn * key(r_ref), axis=2, keepdims=True)


def _wkv_step(s, dvec, an, bvec, kmod, r, v):
    h, n, _, b = s.shape
    hb = _pick(h, 2, 1)
    sblk = pl.BlockSpec((hb, n, n, b), lambda i: (i, 0, 0, 0))
    kblk = pl.BlockSpec((hb, n, b), lambda i: (i, 0, 0))
    vblk = pl.BlockSpec((hb, n, 1, b), lambda i: (i, 0, 0, 0))
    kv = lambda x: x.reshape(h, n, b)
    sn, y = pl.pallas_call(
        _wkv_step_kernel,
        out_shape=[jax.ShapeDtypeStruct((h, n, n, b), F32), jax.ShapeDtypeStruct((h, n, 1, b), F32)],
        grid=(h // hb,),
        in_specs=[sblk] + [kblk] * 5 + [vblk],
        out_specs=[sblk, vblk],
        compiler_params=_cp(1),
        name="wkv_step",
    )(s, kv(dvec), kv(an), kv(bvec), kv(kmod), kv(r), v.reshape(h, n, 1, b))
    return sn, y.reshape(h * n, b)


def _wkv_post_kernel(y_ref, r_ref, k_ref, v_ref, a_ref, g_ref, kap_ref, rk_ref, lg_ref, lb_ref, o_ref):
    ones_bd = _head_ones()
    y = y_ref[0]
    inv_n = 1.0 / HEAD
    mu = _head_sum(y, ones_bd) * inv_n
    yc = y - mu
    var = _head_sum(yc * yc, ones_bd) * inv_n
    yn = yc * lax.rsqrt(var + LNX_EPS) * lg_ref[...] + lb_ref[...]
    kmod = k_ref[0] * (1.0 + (a_ref[0] - 1.0) * kap_ref[...])
    bonus = _head_sum(r_ref[0] * kmod * rk_ref[...], ones_bd) * v_ref[0]
    o_ref[0] = ((yn + bonus) * g_ref[0]).astype(BF16)


def _wkv_post(y, r, k, v, a, g, kap, rk, lnx_g, lnx_b):
    gg, t, d = y.shape
    tr = _pick(t, 256, 16)
    row = pl.BlockSpec((1, tr, d), lambda gi, ti: (gi, ti, 0))
    vec = pl.BlockSpec((1, d), lambda gi, ti: (0, 0))
    return pl.pallas_call(
        _wkv_post_kernel,
        out_shape=jax.ShapeDtypeStruct((gg, t, d), BF16),
        grid=(gg, t // tr),
        in_specs=[row] * 6 + [vec] * 4,
        out_specs=row,
        compiler_params=_cp(2),
        name="wkv_post",
    )(y, r, k, v, a, g, kap.reshape(1, d), rk.reshape(1, d), lnx_g.reshape(1, d), lnx_b.reshape(1, d))


def _lru_conv_kernel(x_ref, w_ref, b_ref, buf_ref, xc_ref, *, width):
    acc = b_ref[...] + x_ref[0] * w_ref[width - 1:width, :]
    for s in range(1, width):
        acc = acc + buf_ref[0, :, width - 1 - s, :] * w_ref[width - 1 - s:width - s, :]
    xc_ref[0] = acc


def _lru_conv(proj, w, b, buf):
    g, r, d2 = proj.shape
    d = d2 // 2
    width = w.shape[0]
    tr = _pick(r, 256, 16)
    row = pl.BlockSpec((1, tr, d), lambda gi, t: (gi, t, 0))
    return pl.pallas_call(
        functools.partial(_lru_conv_kernel, width=width),
        out_shape=jax.ShapeDtypeStruct((g, r, d), F32),
        grid=(g, r // tr),
        in_specs=[pl.BlockSpec((1, tr, d), lambda gi, t: (gi, t, 1)),
                  pl.BlockSpec((width, d), lambda gi, t: (0, 0)),
                  pl.BlockSpec((1, d), lambda gi, t: (0, 0)),
                  pl.BlockSpec((1, tr, width - 1, d), lambda gi, t: (gi, t, 0, 0))],
        out_specs=row,
        compiler_params=_cp(2),
        name="lru_conv",
    )(proj, w, b.reshape(1, d), buf)


def _lru_gates_kernel(x_ref, wa_ref, ba_ref, wx_ref, bx_ref, ga_ref, gx_ref):
    x = x_ref[...].astype(BF16)
    ga_ref[...] = _sigmoid(_dot(x, wa_ref[0].astype(BF16)) + ba_ref[...])
    gx_ref[...] = _sigmoid(_dot(x, wx_ref[0].astype(BF16)) + bx_ref[...])


def _lru_gates(xc, wa, ba, wx, bx):
    m, d = xc.shape
    nb, bw, _ = wa.shape
    tm = _pick(m, 1024, 8)
    xs = pl.BlockSpec((tm, bw), lambda n, i: (i, n))
    ws = pl.BlockSpec((1, bw, bw), lambda n, i: (n, 0, 0))
    bs = pl.BlockSpec((1, bw), lambda n, i: (0, n))
    return pl.pallas_call(
        _lru_gates_kernel,
        out_shape=[jax.ShapeDtypeStruct((m, d), F32)] * 2,
        grid=(nb, m // tm),
        in_specs=[xs, ws, bs, ws, bs],
        out_specs=[xs, xs],
        compiler_params=_cp(2),
        name="lru_gates",
    )(xc, wa, ba.reshape(1, d), wx, bx.reshape(1, d))


def _lru_coeffs(ga, gx, xc, lam):
    log_a = -LRU_C * ga * _softplus(-lam)
    a = jnp.exp(log_a)
    mult = jnp.sqrt(1.0 - jnp.exp(2.0 * log_a))
    return a, mult


def _lru_seq_kernel(xin_ref, y_ref, cw_ref, cb_ref, wa_ref, ba_ref, wx_ref, bx_ref, lam_ref, h0_ref, buf_ref,
                    o_ref, hl_ref, nbuf_ref, cconv, carry, abuf, bbuf, hbuf, *, width):
    t = pl.program_id(2)
    tr = xin_ref.shape[1]

    @pl.when(t == 0)
    def _():
        cconv[...] = buf_ref[0]
        carry[...] = jnp.broadcast_to(h0_ref[0], carry.shape)

    x = xin_ref[0]
    c8 = cconv[...]
    xc = cb_ref[...] + x * cw_ref[width - 1:width, :]
    for s in range(1, width):
        xc = xc + _shift_rows(x, c8, s) * cw_ref[width - 1 - s:width - s, :]
    cconv[...] = x[tr - SUBLANE:tr]
    nbuf_ref[0] = x[tr - SUBLANE:tr]
    xb = xc.astype(BF16)
    ga = _sigmoid(_dot(xb, wa_ref[...].astype(BF16)) + ba_ref[...])
    gx = _sigmoid(_dot(xb, wx_ref[...].astype(BF16)) + bx_ref[...])
    a, mult = _lru_coeffs(ga, gx, xc, lam_ref[...])
    row = lax.broadcasted_iota(jnp.int32, a.shape, 0)
    mult = jnp.where(jnp.logical_and(t == 0, row == 0), 1.0, mult)
    abuf[...] = a
    bbuf[...] = mult * gx * xc
    r8 = lax.broadcasted_iota(jnp.int32, (SUBLANE, a.shape[1]), 0)

    def body(i, c):
        off = pl.multiple_of(i * SUBLANE, SUBLANE)
        aa = abuf[pl.ds(off, SUBLANE), :]
        bb = bbuf[pl.ds(off, SUBLANE), :]
        for s in (1, 2, 4):
            a_s = jnp.where(r8 >= s, pltpu.roll(aa, s, 0), 1.0)
            b_s = jnp.where(r8 >= s, pltpu.roll(bb, s, 0), 0.0)
            bb = aa * b_s + bb
            aa = aa * a_s
        h = aa * c + bb
        hbuf[pl.ds(off, SUBLANE), :] = h
        return h[SUBLANE - 1:SUBLANE, :]

    c = lax.fori_loop(0, tr // SUBLANE, body, carry[0:1, :], unroll=LRU_SCAN_UNROLL)
    carry[...] = jnp.broadcast_to(c, carry.shape)
    o_ref[0] = (hbuf[...] * _gelu(y_ref[0])).astype(BF16)

    @pl.when(t == pl.num_programs(2) - 1)
    def _():
        hl_ref[0] = c


def _lru_seq(proj, cw, cb, wa, ba, wx, bx, lam, h0, buf):
    g, r, d2 = proj.shape
    d = d2 // 2
    nb, bw, _ = wa.shape
    width = cw.shape[0]
    tr = _pick(r, 512, 16)
    row = lambda off: pl.BlockSpec((1, tr, bw), lambda gi, n, t: (gi, t, n + off))
    vec = pl.BlockSpec((1, bw), lambda gi, n, t: (0, n))
    wsp = pl.BlockSpec((None, bw, bw), lambda gi, n, t: (n, 0, 0))
    st = pl.BlockSpec((1, 1, bw), lambda gi, n, t: (gi, 0, n))
    st8 = pl.BlockSpec((1, SUBLANE, bw), lambda gi, n, t: (gi, 0, n))
    tile = pltpu.VMEM((tr, bw), F32)
    small = pltpu.VMEM((SUBLANE, bw), F32)
    return pl.pallas_call(
        functools.partial(_lru_seq_kernel, width=width),
        out_shape=[jax.ShapeDtypeStruct((g, r, d), BF16), jax.ShapeDtypeStruct((g, 1, d), F32),
                   jax.ShapeDtypeStruct((g, SUBLANE, d), F32)],
        grid=(g, nb, r // tr),
        in_specs=[row(nb), row(0), pl.BlockSpec((width, bw), lambda gi, n, t: (0, n)), vec,
                  wsp, vec, wsp, vec, vec, st, st8],
        out_specs=[row(0), st, st8],
        scratch_shapes=[small, small, tile, tile, tile],
        compiler_params=_cp(3),
        name="lru_seq",
    )(proj, proj, cw, cb.reshape(1, d), wa, ba.reshape(1, d), wx, bx.reshape(1, d), lam.reshape(1, d), h0, buf)


def _lru_step_kernel(ga_ref, gx_ref, xc_ref, y_ref, lam_ref, h0_ref, o_ref, h_ref):
    xc = xc_ref[0]
    a, mult = _lru_coeffs(ga_ref[0], gx_ref[0], xc, lam_ref[...])
    h = a * h0_ref[0] + mult * gx_ref[0] * xc
    h_ref[0] = h
    o_ref[0] = (h * _gelu(y_ref[0])).astype(BF16)


def _lru_step(ga, gx, xc, proj, lam, h0):
    g, r, d = ga.shape
    row = pl.BlockSpec((1, r, d), lambda i: (0, 0, 0))
    return pl.pallas_call(
        _lru_step_kernel,
        out_shape=[jax.ShapeDtypeStruct((g, r, d), BF16), jax.ShapeDtypeStruct((g, r, d), F32)],
        grid=(1,),
        in_specs=[row, row, row, row, pl.BlockSpec((1, d), lambda i: (0, 0)), row],
        out_specs=[row, row],
        compiler_params=_cp(1),
        name="lru_step",
    )(ga, gx, xc, proj, lam.reshape(1, d), h0)


def _ffn_up_kernel(*refs, seq, tiles_per_seq):
    if seq:
        x_ref, wg_ref, wu_ref, cw_ref, cb_ref, buf_ref, o_ref, nb_ref, wgb, wub, carry = refs
    else:
        x_ref, wg_ref, wu_ref, cw_ref, cb_ref, buf_ref, o_ref, nb_ref, wgb, wub = refs
    i = pl.program_id(1)

    @pl.when(i == 0)
    def _():
        wgb[...] = wg_ref[...].astype(BF16)
        wub[...] = wu_ref[...].astype(BF16)

    tm = x_ref.shape[0]
    ts = _pick(tm, FFN_SUB_ROWS, 16)
    if seq:
        @pl.when(i % tiles_per_seq == 0)
        def _():
            carry[...] = buf_ref[0]

        c8 = carry[...]
    for s0 in range(0, tm, ts):
        rows = slice(s0, s0 + ts)
        x = x_ref[rows, :]
        gt = _dot(x, wgb[...])
        u = _dot(x, wub[...])
        if seq:
            p1 = _shift_rows(gt, c8, 1)
            p2 = _shift_rows(gt, c8, 2)
            c8 = gt[ts - SUBLANE:ts]
        else:
            p2 = buf_ref[rows, 0, :]
            p1 = buf_ref[rows, 1, :]
            nb_ref[rows, 0, :] = p1
            nb_ref[rows, 1, :] = gt
        gc = cb_ref[...] + p2 * cw_ref[0:1, :] + p1 * cw_ref[1:2, :] + gt * cw_ref[2:3, :]
        o_ref[rows, :] = (_gelu(gc) * u).astype(BF16)
    if seq:
        carry[...] = c8
        nb_ref[0] = c8


def _ffn_up(x, wg, wu, layer, cw, cb, buf, seq, rows_per_seq):
    m, k = x.shape
    f = wg.shape[2]
    tn = _pick(f, 512, LANE)
    tm = _pick(rows_per_seq if seq else m, 1024, 16)
    nj = f // tn
    xs = pl.BlockSpec((tm, k), lambda j, i: (i, 0))
    ws = _wspec(wg, layer, k, tn, lambda j, i: j)
    os_ = pl.BlockSpec((tm, tn), lambda j, i: (i, j))
    in_specs = [xs, ws, ws, pl.BlockSpec((3, tn), lambda j, i: (0, j)), pl.BlockSpec((1, tn), lambda j, i: (0, j))]
    scratch = [pltpu.VMEM((k, tn), BF16), pltpu.VMEM((k, tn), BF16)]
    if seq:
        tps = rows_per_seq // tm
        g = m // rows_per_seq
        bspec = pl.BlockSpec((1, SUBLANE, tn), lambda j, i: (i // tps, 0, j))
        in_specs.append(bspec)
        out_shape = [jax.ShapeDtypeStruct((m, f), BF16), jax.ShapeDtypeStruct((g, SUBLANE, f), F32)]
        out_specs = [os_, bspec]
        scratch.append(pltpu.VMEM((SUBLANE, tn), F32))
    else:
        tps = 1
        hspec = pl.BlockSpec((tm, 2, tn), lambda j, i: (i, 0, j))
        in_specs.append(hspec)
        out_shape = [jax.ShapeDtypeStruct((m, f), BF16), jax.ShapeDtypeStruct((m, 2, f), F32)]
        out_specs = [os_, hspec]
    args = [x, wg, wu, cw, cb.reshape(1, f), buf]
    return pl.pallas_call(
        functools.partial(_ffn_up_kernel, seq=seq, tiles_per_seq=tps),
        out_shape=out_shape,
        grid=(nj, m // tm),
        in_specs=in_specs,
        out_specs=out_specs,
        scratch_shapes=scratch,
        compiler_params=_cp(2),
        name="ffn_up",
    )(*args)


def _trunk(x, mods, st, seq, P):
    g, r, d = x.shape
    m = g * r
    depth = P["w_mod"].shape[0]
    o_wkv, o_shift, o_h, o_lconv, o_fconv = [], [], [], [], []
    flat = lambda t: t.reshape(m, t.shape[-1])
    unflat = lambda t: t.reshape(g, r, t.shape[-1])
    h_in = None
    for i in range(depth):
        mod = mods[i]
        j = i // 2
        if i % 2 == 0:
            if seq:
                shift = jnp.zeros((g, 1, d), F32)
            else:
                shift = st["shift"][j].reshape(1, r, d)
            lora = tuple(P[nm][j] for nm in ("rw_w1", "rw_w2", "rw_w0", "rw_a1", "rw_a2", "rw_a0", "rw_g1", "rw_g2"))
            pre, shift_new = _rwkv_prep(x, P["norm_g"][i, 0], mod, P["rw_mix"][j], shift, seq, lora)
            xr, xk, xv, wpre, aa, gg = [flat(t) for t in pre]
            rr = _mm(xr, P["rw_wr"], j, name="rw_r")
            kk = _mm(xk, P["rw_wk"], j, name="rw_k")
            vv = _mm(xv, P["rw_wv"], j, name="rw_v")
            if seq:
                y, s_bd = _wkv_chunked(unflat(rr), unflat(kk), unflat(vv), unflat(wpre), unflat(aa),
                                       P["rw_kk"][j], P["rw_ka"][j])
                npairs = d // LANE
                s_new = jnp.stack([s_bd[:, :, :HEAD, :HEAD], s_bd[:, :, HEAD:, HEAD:]], axis=2)
                s_new = s_new.reshape(g, 2 * npairs, HEAD, HEAD)
                o_shift.append(shift_new.reshape(g, d))
            else:
                s_t = jnp.transpose(st["wkv"][j], (1, 2, 3, 0))
                vecs = _wkv_vec(kk, wpre, aa, rr, vv, P["rw_kk"][j], P["rw_ka"][j])
                s_new, y = _wkv_step(s_t, *vecs)
                s_new = jnp.transpose(s_new, (3, 0, 1, 2))
                y = y.T.reshape(1, r, d)
                o_shift.append(shift_new.reshape(r, d))
            o_wkv.append(s_new)
            xo = _wkv_post(y, unflat(rr), unflat(kk), unflat(vv), unflat(aa), unflat(gg),
                           P["rw_ka"][j], P["rw_rk"][j], P["rw_lnx_g"][j], P["rw_lnx_b"][j])
            mix_out, w_out = xo, P["rw_wo"]
        else:
            proj = _mm(flat(h_in), P["lru_w_in"], j, P["lru_b_in"][j], name="lru_in")
            dr = proj.shape[1] // 2
            width = P["lru_conv_w"].shape[1]
            if seq:
                hy, hl, nb8 = _lru_seq(unflat(proj), P["lru_conv_w"][j], P["lru_conv_b"][j],
                                       P["lru_wa"][j], P["lru_ba"][j], P["lru_wx"][j], P["lru_bx"][j],
                                       P["lru_lambda"][j], jnp.zeros((g, 1, dr), F32),
                                       jnp.zeros((g, SUBLANE, dr), F32))
                o_lconv.append(nb8[:, SUBLANE - (width - 1):, :])
                o_h.append(hl.reshape(g, dr))
            else:
                hist = st["lru_conv"][j]
                xc = _lru_conv(unflat(proj), P["lru_conv_w"][j], P["lru_conv_b"][j],
                               hist.reshape(1, r, width - 1, dr))
                o_lconv.append(jnp.concatenate([hist[:, 1:], proj[:, None, dr:]], axis=1))
                ga, gx = _lru_gates(flat(xc), P["lru_wa"][j], P["lru_ba"][j], P["lru_wx"][j], P["lru_bx"][j])
                hy, hl = _lru_step(unflat(ga), unflat(gx), xc, unflat(proj), P["lru_lambda"][j],
                                   st["lru_h"][j].reshape(1, r, dr))
                o_h.append(hl.reshape(r, dr))
            mix_out, w_out = hy, P["lru_w_out"]
        x, hf = _mm_resid(mix_out, w_out, j, x, P["norm_g"][i, 1], mod, 2, P["norm_g"][i, 2], 4, 3,
                          name="mix_out")
        fdim = P["ffn_w_gate"].shape[2]
        if seq:
            hact, nb8 = _ffn_up(flat(hf), P["ffn_w_gate"], P["ffn_w_up"], i, P["ffn_conv_w"][i],
                                P["ffn_conv_b"][i], jnp.zeros((g, SUBLANE, fdim), F32), True, r)
            o_fconv.append(nb8[:, SUBLANE - 2:, :])
        else:
            hact, nhist = _ffn_up(flat(hf), P["ffn_w_gate"], P["ffn_w_up"], i, P["ffn_conv_w"][i],
                                  P["ffn_conv_b"][i], st["ffn_conv"][i], False, r)
            o_fconv.append(nhist)
        f = _mm(hact, P["ffn_w_down"], i, tm_target=512, tn_target=512, name="ffn_down")
        if i + 1 < depth:
            nmod = mods[i + 1]
            if (i + 1) % 2 == 0:
                x = _resid_norm(x, unflat(f), P["norm_g"][i, 3], mod, 5, name="resid_ffn")
            else:
                x, h_in = _resid_norm(x, unflat(f), P["norm_g"][i, 3], mod, 5,
                                      (P["norm_g"][i + 1, 0], nmod, 1, 0), name="resid_ffn")
        else:
            x = _resid_norm(x, unflat(f), P["norm_g"][i, 3], mod, 5, name="resid_ffn")
    return x, jnp.stack(o_wkv), jnp.stack(o_shift), jnp.stack(o_h), jnp.stack(o_lconv), jnp.stack(o_fconv)


def kernel(x_prompt, x_sample, c_prompt, c_sample, state_rwkv_wkv, state_rwkv_shift, state_lru_h,
           state_lru_conv, state_ffn_conv, w_mod, b_mod, norm_g, rw_mix, rw_wr, rw_wk, rw_wv, rw_wo,
           rw_w0, rw_w1, rw_w2, rw_a0, rw_a1, rw_a2, rw_g1, rw_g2, rw_kk, rw_ka, rw_rk, rw_lnx_g,
           rw_lnx_b, lru_w_in, lru_b_in, lru_conv_w, lru_conv_b, lru_wa, lru_ba, lru_wx, lru_bx,
           lru_lambda, lru_w_out, ffn_w_gate, ffn_w_up, ffn_w_down, ffn_conv_w, ffn_conv_b):
    P = dict(w_mod=w_mod, b_mod=b_mod, norm_g=norm_g, rw_mix=rw_mix, rw_wr=rw_wr, rw_wk=rw_wk,
             rw_wv=rw_wv, rw_wo=rw_wo, rw_w0=rw_w0, rw_w1=rw_w1, rw_w2=rw_w2, rw_a0=rw_a0, rw_a1=rw_a1,
             rw_a2=rw_a2, rw_g1=rw_g1, rw_g2=rw_g2, rw_kk=rw_kk, rw_ka=rw_ka, rw_rk=rw_rk,
             rw_lnx_g=rw_lnx_g, rw_lnx_b=rw_lnx_b, lru_w_in=lru_w_in, lru_b_in=lru_b_in,
             lru_conv_w=lru_conv_w, lru_conv_b=lru_conv_b, lru_wa=lru_wa, lru_ba=lru_ba, lru_wx=lru_wx,
             lru_bx=lru_bx, lru_lambda=lru_lambda, lru_w_out=lru_w_out, ffn_w_gate=ffn_w_gate,
             ffn_w_up=ffn_w_up, ffn_w_down=ffn_w_down, ffn_conv_w=ffn_conv_w, ffn_conv_b=ffn_conv_b)
    bp, t, d = x_prompt.shape
    bs = x_sample.shape[0]
    depth = w_mod.shape[0]
    c_all = jnp.concatenate([c_prompt, c_sample], axis=0)
    rows = c_all.shape[0]
    rows_p = -(-rows // 16) * 16
    c_all = jnp.pad(c_all, ((0, rows_p - rows), (0, 0)))
    mods_p, mods_s = [], []
    for i in range(depth):
        mod = _mm(c_all, w_mod, i, b_mod[i], pre_act="silu", name="mod")
        mods_p.append(mod[:bp].reshape(bp, 1, N_MOD * d))
        mods_s.append(mod[bp:bp + bs].reshape(1, bs, N_MOD * d))
    y_p, p_wkv, p_shift, p_h, p_lconv, p_fconv = _trunk(x_prompt, mods_p, None, True, P)
    st = dict(wkv=state_rwkv_wkv, shift=state_rwkv_shift, lru_h=state_lru_h,
              lru_conv=state_lru_conv, ffn_conv=state_ffn_conv)
    y_s, s_wkv, s_shift, s_h, s_lconv, s_fconv = _trunk(x_sample.reshape(1, bs, d), mods_s, st, False, P)
    return (y_p, y_s.reshape(bs, 1, d), p_wkv, p_shift, p_h, p_lconv, p_fconv,
            s_wkv, s_shift, s_h, s_lconv, s_fconv)
```

```python
import functools

import jax
import jax.numpy as jnp
from jax import lax
from jax.experimental import pallas as pl
from jax.experimental.pallas import tpu as pltpu

F32 = jnp.float32
BF16 = jnp.bfloat16

NORM_EPS = 1e-6
LNX_EPS = 64e-5
LRU_C = 8.0
N_MOD = 6
HEAD = 64
LANE = 128
SUBLANE = 8
CHUNK = 64
PAIR_UNROLL = 16
LRU_SCAN_UNROLL = 4
MM_RESID_SUB_ROWS = 128
FFN_SUB_ROWS = 128
VMEM_LIMIT = 52 * 1024 * 1024


def _cp(n_axes):
    return pltpu.CompilerParams(dimension_semantics=("arbitrary",) * n_axes,
                                vmem_limit_bytes=VMEM_LIMIT)


def _pick(n, target, mult):
    best = None
    for t in range(mult, min(n, target) + 1, mult):
        if n % t == 0:
            best = t
    return best if best is not None else n


def _softplus(z):
    return jnp.maximum(z, 0.0) + jnp.log(1.0 + jnp.exp(-jnp.abs(z)))


def _sigmoid(z):
    return 1.0 / (1.0 + jnp.exp(-z))


def _gelu(x):
    return 0.5 * x * (1.0 + jnp.tanh(0.7978845608028654 * (x + 0.044715 * (x * x * x))))


def _act(x, name):
    if name is None:
        return x
    if name == "tanh":
        return jnp.tanh(x)
    if name == "sigmoid":
        return _sigmoid(x)
    if name == "silu":
        return x * _sigmoid(x)
    raise ValueError(name)


def _rms(x, g):
    ms = jnp.mean(x * x, axis=-1, keepdims=True)
    return x * lax.rsqrt(ms + NORM_EPS) * g


def _dot(x, y):
    return jnp.dot(x, y, preferred_element_type=F32)


def _dot_nt(x, y):
    return lax.dot_general(x, y, (((1,), (1,)), ((), ())), preferred_element_type=F32)


def _split_dot_left(m, x, parts):
    acc = None
    rem = x
    for i in range(parts):
        piece = rem.astype(BF16)
        term = jnp.dot(m, piece, preferred_element_type=F32)
        acc = term if acc is None else acc + term
        if i + 1 < parts:
            rem = rem - piece.astype(F32)
    return acc


def _dot_tn(x, y):
    return lax.dot_general(x, y, (((0,), (0,)), ((), ())), preferred_element_type=F32)


def _head_sum(x):
    d = x.shape[-1]
    first = lax.broadcasted_iota(jnp.int32, (1, LANE), 1) < HEAD
    cols = []
    for s in range(0, d, LANE):
        xs = x[:, s:s + LANE]
        lo = jnp.sum(jnp.where(first, xs, 0.0), axis=-1, keepdims=True)
        hi = jnp.sum(jnp.where(first, 0.0, xs), axis=-1, keepdims=True)
        cols.append(jnp.where(first, lo, hi))
    return cols[0] if len(cols) == 1 else jnp.concatenate(cols, axis=-1)


def _shift_rows(x, carry8, s):
    n = x.shape[0]
    rolled = pltpu.roll(x, s, 0)
    rolled_c = pltpu.roll(carry8, s, 0)
    row8 = lax.broadcasted_iota(jnp.int32, (SUBLANE, x.shape[1]), 0)
    top = jnp.where(row8 < s, rolled_c, rolled[0:SUBLANE])
    if n == SUBLANE:
        return top
    return jnp.concatenate([top, rolled[SUBLANE:]], axis=0)


def _mm_kernel(*refs, has_bias, has_x2, pre_act, act):
    refs = list(refs)
    x_ref, w_ref = refs[:2]
    x2_ref = refs[2] if has_x2 else None
    b_ref = refs[2 + has_x2] if has_bias else None
    o_ref = refs[2 + has_x2 + has_bias]
    o2_ref = refs[3 + has_x2 + has_bias] if has_x2 else None
    wb_ref = refs[-1]

    @pl.when(pl.program_id(1) == 0)
    def _():
        wb_ref[...] = w_ref[...].astype(BF16)

    def project(src, dst):
        x = src[...]
        if pre_act is not None:
            x = _act(x.astype(F32), pre_act)
        acc = _dot(x.astype(BF16), wb_ref[...])
        if has_bias:
            acc = acc + b_ref[...]
        dst[...] = _act(acc, act).astype(dst.dtype)

    project(x_ref, o_ref)
    if has_x2:
        @pl.when(pl.program_id(1) == pl.num_programs(1) - 1)
        def _():
            project(x2_ref, o2_ref)


def _wspec(w, layer, rows, cols, col_of):
    assert w.ndim == 3
    return pl.BlockSpec((None, rows, cols), lambda *ids: (layer, 0, col_of(*ids)))


def _mm(x, w, layer, bias=None, *, x2=None, pre_act=None, act=None, out_dtype=F32, tm_target=1024,
        tn_target=1024, name="mm"):
    m, k = x.shape
    n = w.shape[2]
    tm = _pick(m, tm_target, 16)
    tn = _pick(n, tn_target, LANE)
    in_specs = [pl.BlockSpec((tm, k), lambda j, i: (i, 0)),
                _wspec(w, layer, k, tn, lambda j, i: j)]
    args = [x, w]
    out_shape = [jax.ShapeDtypeStruct((m, n), out_dtype)]
    out_specs = [pl.BlockSpec((tm, tn), lambda j, i: (i, j))]
    if x2 is not None:
        m2 = x2.shape[0]
        in_specs.append(pl.BlockSpec((m2, k), lambda j, i: (0, 0)))
        args.append(x2)
        out_shape.append(jax.ShapeDtypeStruct((m2, n), out_dtype))
        out_specs.append(pl.BlockSpec((m2, tn), lambda j, i: (0, j)))
    if bias is not None:
        in_specs.append(pl.BlockSpec((1, tn), lambda j, i: (0, j)))
        args.append(bias.reshape(1, n))
    res = pl.pallas_call(
        functools.partial(_mm_kernel, has_bias=bias is not None, has_x2=x2 is not None, pre_act=pre_act, act=act),
        out_shape=out_shape,
        grid=(n // tn, m // tm),
        in_specs=in_specs,
        out_specs=out_specs,
        scratch_shapes=[pltpu.VMEM((k, tn), BF16)],
        compiler_params=_cp(2),
        name=name,
    )(*args)
    return res if x2 is not None else res[0]


def _mod_spec(mod, slot, d, tr):
    rm = mod.shape[1]
    if rm == 1:
        return pl.BlockSpec((1, 1, d), lambda g, t: (g, 0, slot))
    return pl.BlockSpec((1, tr, d), lambda g, t: (g, t, slot))


def _resid_norm_kernel(*refs, with_next):
    if with_next:
        x_ref, o_ref, g1_ref, gate_ref, g2_ref, sc_ref, sh_ref, xn_ref, h_ref = refs
    else:
        x_ref, o_ref, g1_ref, gate_ref, xn_ref = refs
    xn = x_ref[0] + gate_ref[0] * _rms(o_ref[0], g1_ref[...])
    xn_ref[0] = xn
    if with_next:
        h_ref[0] = (_rms(xn, g2_ref[...]) * (1.0 + sc_ref[0]) + sh_ref[0]).astype(h_ref.dtype)


def _resid_norm(x, o, g1, mod, gate_slot, nxt=None, name="resid_norm"):
    g, r, d = x.shape
    tr = _pick(r, 256, 16)
    row = pl.BlockSpec((1, tr, d), lambda gi, t: (gi, t, 0))
    vec = pl.BlockSpec((1, d), lambda gi, t: (0, 0))
    in_specs = [row, row, vec, _mod_spec(mod, gate_slot, d, tr)]
    args = [x, o, g1.reshape(1, d), mod]
    out_shape = [jax.ShapeDtypeStruct((g, r, d), F32)]
    out_specs = [row]
    if nxt is not None:
        g2, mod2, sc_slot, sh_slot = nxt
        in_specs += [vec, _mod_spec(mod2, sc_slot, d, tr), _mod_spec(mod2, sh_slot, d, tr)]
        args += [g2.reshape(1, d), mod2, mod2]
        out_shape.append(jax.ShapeDtypeStruct((g, r, d), BF16))
        out_specs.append(row)
    res = pl.pallas_call(
        functools.partial(_resid_norm_kernel, with_next=nxt is not None),
        out_shape=out_shape,
        grid=(g, r // tr),
        in_specs=in_specs,
        out_specs=out_specs,
        compiler_params=_cp(2),
        name=name,
    )(*args)
    return res if nxt is not None else res[0]


def _mm_resid_kernel(a_ref, w_ref, x_ref, g1_ref, gate_ref, g2_ref, sc_ref, sh_ref, xn_ref, h_ref):
    tm = a_ref.shape[1]
    ts = _pick(tm, MM_RESID_SUB_ROWS, 16)

    def rows_of(ref, rows):
        return ref[0] if ref.shape[1] == 1 else ref[0, rows, :]

    for s0 in range(0, tm, ts):
        rows = slice(s0, s0 + ts)
        o = _dot(a_ref[0, rows, :], w_ref[...])
        xn = x_ref[0, rows, :] + rows_of(gate_ref, rows) * _rms(o, g1_ref[...])
        xn_ref[0, rows, :] = xn
        h_ref[0, rows, :] = (_rms(xn, g2_ref[...]) * (1.0 + rows_of(sc_ref, rows))
                             + rows_of(sh_ref, rows)).astype(BF16)


def _mm_resid(a, w, layer, x, g1, mod, gate_slot, g2, sc_slot, sh_slot, name):
    g, r, k = a.shape
    n = w.shape[2]
    wb = w[layer].astype(BF16)
    tm = _pick(r, 256, 16)
    row = lambda c: pl.BlockSpec((1, tm, c), lambda gi, t: (gi, t, 0))
    vec = pl.BlockSpec((1, n), lambda gi, t: (0, 0))
    return pl.pallas_call(
        _mm_resid_kernel,
        out_shape=[jax.ShapeDtypeStruct((g, r, n), F32), jax.ShapeDtypeStruct((g, r, n), BF16)],
        grid=(g, r // tm),
        in_specs=[row(k), pl.BlockSpec((k, n), lambda gi, t: (0, 0)), row(n), vec,
                  _mod_spec(mod, gate_slot, n, tm), vec, _mod_spec(mod, sc_slot, n, tm),
                  _mod_spec(mod, sh_slot, n, tm)],
        out_specs=[row(n), row(n)],
        compiler_params=_cp(2),
        name=name,
    )(a, wb, x, g1.reshape(1, n), mod, g2.reshape(1, n), mod, mod)


def _rwkv_prep_kernel(x_ref, g_ref, sc_ref, sh_ref, mix_ref, shift_ref,
                      w1_ref, w2_ref, w0_ref, a1_ref, a2_ref, a0_ref, g1_ref, g2_ref,
                      xr_ref, xk_ref, xv_ref, wpre_ref, a_ref, gate_ref, shift_out, *scratch, seq):
    h = _rms(x_ref[0], g_ref[...]) * (1.0 + sc_ref[0]) + sh_ref[0]
    tr = h.shape[0]
    if seq:
        carry = scratch[0]
        t = pl.program_id(1)

        @pl.when(t == 0)
        def _():
            carry[...] = jnp.broadcast_to(shift_ref[0], carry.shape)

        hprev = _shift_rows(h, carry[...], 1)
        carry[...] = h[tr - SUBLANE:tr]

        @pl.when(t == pl.num_programs(1) - 1)
        def _():
            shift_out[0] = h[tr - 1:tr]
    else:
        hprev = shift_ref[0]
        shift_out[0] = h
    xx = hprev - h

    def mixed(j):
        return (h + xx * mix_ref[j:j + 1, :]).astype(BF16)

    xr_ref[0] = mixed(0)
    xk_ref[0] = mixed(2)
    xv_ref[0] = mixed(3)
    wpre_ref[0] = w0_ref[...] + _dot(jnp.tanh(_dot(mixed(1), w1_ref[...])).astype(BF16), w2_ref[...])
    a_ref[0] = _sigmoid(a0_ref[...] + _dot(_dot(mixed(4), a1_ref[...]).astype(BF16), a2_ref[...]))
    gate_ref[0] = _dot(_sigmoid(_dot(mixed(5), g1_ref[...])).astype(BF16), g2_ref[...])


def _lora_pair(w1, w2):
    r = w1.shape[1]
    rp = -(-r // LANE) * LANE
    if rp != r:
        w1 = jnp.pad(w1, ((0, 0), (0, rp - r)))
        w2 = jnp.pad(w2, ((0, rp - r), (0, 0)))
    return w1.astype(BF16), w2.astype(BF16)


def _rwkv_prep(x, g, mod, mix, shift, seq, lora):
    gg, r, d = x.shape
    tr = _pick(r, 256, 16)
    w1, w2, w0, a1, a2, a0, g1, g2 = lora
    w1, w2 = _lora_pair(w1, w2)
    a1, a2 = _lora_pair(a1, a2)
    g1, g2 = _lora_pair(g1, g2)
    row = pl.BlockSpec((1, tr, d), lambda gi, t: (gi, t, 0))
    vec = pl.BlockSpec((1, d), lambda gi, t: (0, 0))
    full = lambda w: pl.BlockSpec(w.shape, lambda gi, t: (0, 0))
    if seq:
        shift_spec = pl.BlockSpec((1, 1, d), lambda gi, t: (gi, 0, 0))
        shift_shape = (gg, 1, d)
        scratch = [pltpu.VMEM((SUBLANE, d), F32)]
    else:
        shift_spec = row
        shift_shape = (gg, r, d)
        scratch = []
    act = lambda dt: jax.ShapeDtypeStruct((gg, r, d), dt)
    outs = pl.pallas_call(
        functools.partial(_rwkv_prep_kernel, seq=seq),
        out_shape=[act(BF16)] * 3 + [act(F32)] * 3 + [jax.ShapeDtypeStruct(shift_shape, F32)],
        grid=(gg, r // tr),
        in_specs=[row, vec, _mod_spec(mod, 1, d, tr), _mod_spec(mod, 0, d, tr),
                  pl.BlockSpec((6, d), lambda gi, t: (0, 0)), shift_spec,
                  full(w1), full(w2), vec, full(a1), full(a2), vec, full(g1), full(g2)],
        out_specs=[row] * 6 + [shift_spec],
        scratch_shapes=scratch,
        compiler_params=_cp(2),
        name="rwkv_prep",
    )(x, g.reshape(1, d), mod, mod, mix, shift, w1, w2, w0.reshape(1, d), a1, a2, a0.reshape(1, d), g1, g2)
    return outs[:6], outs[6]


def _wkv_terms(k, wpre, a, kkp, kap):
    logd = -0.6065306597126334 * _sigmoid(wpre)
    kk = k * kkp
    kk = kk * lax.rsqrt(jnp.maximum(_head_sum(kk * kk), 1e-24))
    kmod = k * (1.0 + (a - 1.0) * kap)
    return logd, kk, kmod


def _wkv_readout(y, bonus, gate, lnx_g, lnx_b):
    inv_n = 1.0 / HEAD
    yc = y - _head_sum(y) * inv_n
    var = _head_sum(yc * yc) * inv_n
    return ((yc * lax.rsqrt(var + LNX_EPS) * lnx_g + lnx_b + bonus) * gate).astype(BF16)


def _wkv_chunk_kernel(r_ref, k_ref, v_ref, w_ref, a_ref, gate_ref, kkp_ref, kap_ref, rk_ref, lg_ref, lb_ref,
                      xo_ref, sout_ref, S, Rt, At, Bt, Kt, Vs, Gc, Ys, Bonus):
    c = pl.program_id(1)
    n_pairs = S.shape[0]
    C = r_ref.shape[1]

    @pl.when(c == 0)
    def _():
        S[...] = jnp.zeros(S.shape, F32)

    a = a_ref[0]
    logd, kk, kmod = _wkv_terms(k_ref[0], w_ref[0], a, kkp_ref[...], kap_ref[...])
    ri = lax.broadcasted_iota(jnp.int32, (C, C), 0)
    ci = lax.broadcasted_iota(jnp.int32, (C, C), 1)
    tri = (ri >= ci).astype(BF16)
    cum = _split_dot_left(tri, logd, 3)
    gam = jnp.exp(cum)
    rt = r_ref[0] * gam
    at = -kk * jnp.exp(cum - logd)
    ginv = jnp.exp(-cum)
    bt = kk * a * ginv
    kt = kmod * ginv
    v = v_ref[0]
    Bonus[...] = _head_sum(r_ref[0] * kmod * rk_ref[...]) * v
    for p in range(n_pairs):
        sl = slice(p * LANE, (p + 1) * LANE)
        Rt[p] = rt[:, sl]
        At[p] = at[:, sl]
        Bt[p] = bt[:, sl]
        Kt[p] = kt[:, sl]
        Vs[p] = v[:, sl]
        Gc[p] = jnp.broadcast_to(gam[C - 1:C, sl], (SUBLANE, LANE))

    lane = lax.broadcasted_iota(jnp.int32, (1, LANE), 1)
    first = lane < HEAD
    r2 = lax.broadcasted_iota(jnp.int32, (2 * C, 2 * C), 0)
    c2 = lax.broadcasted_iota(jnp.int32, (2 * C, 2 * C), 1)
    strict = r2 > c2
    incl = r2 >= c2
    rs = lax.broadcasted_iota(jnp.int32, (LANE, LANE), 0)
    cs = lax.broadcasted_iota(jnp.int32, (LANE, LANE), 1)
    same_head = (rs < HEAD) == (cs < HEAD)

    def stack(x):
        return jnp.concatenate([jnp.where(first, x, 0.0), jnp.where(first, 0.0, x)], axis=0).astype(BF16)

    n_fac = max(1, (C - 1).bit_length())

    def pairs_step(ps):
        n = range(len(ps))
        rp, ap, bp = [Rt[p] for p in ps], [At[p] for p in ps], [Bt[p] for p in ps]
        kp, vp = [Kt[p] for p in ps], [Vs[p] for p in ps]
        sp, gc = [S[p] for p in ps], [Gc[p] for p in ps]
        am, rm = [stack(t) for t in ap], [stack(t) for t in rp]
        bm, km = [stack(t) for t in bp], [stack(t) for t in kp]
        ar = [jnp.concatenate([am[i], rm[i]], axis=0) for i in n]
        gb = [_dot_nt(ar[i], bm[i]) for i in n]
        gk = [_dot_nt(ar[i], km[i]) for i in n]
        lp = [jnp.where(strict, t[:2 * C], 0.0) for t in gb]
        lak = [jnp.where(strict, t[:2 * C], 0.0).astype(BF16) for t in gk]
        grbk = [jnp.concatenate([jnp.where(incl, gb[i][2 * C:], 0.0), jnp.where(incl, gk[i][2 * C:], 0.0)],
                                axis=1).astype(BF16) for i in n]
        spb = [t.astype(BF16) for t in sp]
        ars = [_dot_nt(jnp.concatenate([ap[i], rp[i]], axis=0).astype(BF16), spb[i]) for i in n]
        a_s, r_s = [t[:C] for t in ars], [t[C:] for t in ars]
        vst = [jnp.concatenate([t, t], axis=0).astype(BF16) for t in vp]
        x = [jnp.concatenate([a_s[i], a_s[i]], axis=0) + _dot(lak[i], vst[i]) for i in n]
        for f in range(n_fac):
            lh = [t.astype(BF16) for t in lp]
            xh = [t.astype(BF16) for t in x]
            xl = [(x[i] - xh[i].astype(F32)).astype(BF16) for i in n]
            lhs = [jnp.concatenate([t, t], axis=1) for t in lh]
            if f + 1 < n_fac:
                ll = [(lp[i] - lh[i].astype(F32)).astype(BF16) for i in n]
                rhs = [jnp.concatenate([jnp.concatenate([xh[i], lh[i]], axis=1),
                                        jnp.concatenate([xl[i], ll[i]], axis=1)], axis=0) for i in n]
                prod = [_dot(lhs[i], rhs[i]) for i in n]
                x = [x[i] + prod[i][:, :LANE] for i in n]
                lp = [t[:, LANE:] for t in prod]
            else:
                x = [x[i] + _dot(lhs[i], jnp.concatenate([xh[i], xl[i]], axis=0)) for i in n]
        yst = [jnp.concatenate([r_s[i], r_s[i]], axis=0)
               + _dot(grbk[i], jnp.concatenate([x[i].astype(BF16), vst[i]], axis=0)) for i in n]
        u = [jnp.where(first, t[:C], t[C:]) for t in x]
        ds = [_dot_tn(jnp.concatenate([u[i], vp[i]], axis=0).astype(BF16),
                      jnp.concatenate([bp[i], kp[i]], axis=0).astype(BF16)) for i in n]
        for i, p in enumerate(ps):
            Ys[p] = jnp.where(first, yst[i][:C], yst[i][C:])
            S[p] = jnp.where(same_head, (sp[i] + ds[i]) * gc[i][0:1, :], 0.0)

    for p0 in range(0, n_pairs, PAIR_UNROLL):
        pairs_step(list(range(p0, min(p0 + PAIR_UNROLL, n_pairs))))

    y = jnp.concatenate([Ys[p] for p in range(n_pairs)], axis=-1) if n_pairs > 1 else Ys[0]
    xo_ref[0] = _wkv_readout(y, Bonus[...], gate_ref[0], lg_ref[...], lb_ref[...])

    @pl.when(c == pl.num_programs(1) - 1)
    def _():
        sout_ref[0] = S[...]


def _wkv_chunked(r, k, v, wpre, a, gate, kkp, kap, rk, lnx_g, lnx_b):
    g, t, d = r.shape
    C = CHUNK
    np_ = d // LANE
    row = pl.BlockSpec((1, C, d), lambda gi, ci: (gi, ci, 0))
    vec = pl.BlockSpec((1, d), lambda gi, ci: (0, 0))
    pair_buf = pltpu.VMEM((np_, C, LANE), F32)
    return pl.pallas_call(
        _wkv_chunk_kernel,
        out_shape=[jax.ShapeDtypeStruct((g, t, d), BF16),
                   jax.ShapeDtypeStruct((g, np_, LANE, LANE), F32)],
        grid=(g, t // C),
        in_specs=[row] * 6 + [vec] * 5,
        out_specs=[row, pl.BlockSpec((1, np_, LANE, LANE), lambda gi, ci: (gi, 0, 0, 0))],
        scratch_shapes=[pltpu.VMEM((np_, LANE, LANE), F32), pair_buf, pair_buf, pair_buf, pair_buf, pair_buf,
                        pltpu.VMEM((np_, SUBLANE, LANE), F32), pair_buf, pltpu.VMEM((C, d), F32)],
        compiler_params=_cp(2),
        name="wkv_chunk",
    )(r, k, v, wpre, a, gate, *[p.reshape(1, d) for p in (kkp, kap, rk, lnx_g, lnx_b)])


def _wkv_vec_kernel(k_ref, w_ref, a_ref, r_ref, v_ref, kkp_ref, kap_ref,
                    d_ref, an_ref, b_ref, km_ref, rt_ref, vt_ref):
    a = a_ref[...]
    logd, kk, kmod = _wkv_terms(k_ref[...], w_ref[...], a, kkp_ref[...], kap_ref[...])
    d_ref[...] = jnp.exp(logd).T
    an_ref[...] = (-kk).T
    b_ref[...] = (kk * a).T
    km_ref[...] = kmod.T
    rt_ref[...] = r_ref[...].T
    vt_ref[...] = v_ref[...].T


def _wkv_vec(k, wpre, a, r, v, kkp, kap):
    b, d = k.shape
    full = pl.BlockSpec((b, d), lambda i: (0, 0))
    fullt = pl.BlockSpec((d, b), lambda i: (0, 0))
    vec = pl.BlockSpec((1, d), lambda i: (0, 0))
    return pl.pallas_call(
        _wkv_vec_kernel,
        out_shape=[jax.ShapeDtypeStruct((d, b), F32)] * 6,
        grid=(1,),
        in_specs=[full] * 5 + [vec, vec],
        out_specs=[fullt] * 6,
        compiler_params=_cp(1),
        name="wkv_vec",
    )(k, wpre, a, r, v, kkp.reshape(1, d), kap.reshape(1, d))


def _wkv_step_kernel(s_ref, d_ref, a_ref, b_ref, k_ref, r_ref, v_ref, sn_ref, y_ref):
    s = s_ref[...]
    key = lambda ref: ref[...][:, None]
    sa = jnp.sum(s * key(a_ref), axis=2, keepdims=True)
    sn = s * key(d_ref) + sa * key(b_ref) + v_ref[...] * key(k_ref)
    sn_ref[...] = sn
    y_ref[...] = jnp.sum(sn * key(r_ref), axis=2, keepdims=True)


def _wkv_step(s, dvec, an, bvec, kmod, r, v):
    h, n, _, b = s.shape
    hb = _pick(h, 2, 1)
    sblk = pl.BlockSpec((hb, n, n, b), lambda i: (i, 0, 0, 0))
    kblk = pl.BlockSpec((hb, n, b), lambda i: (i, 0, 0))
    vblk = pl.BlockSpec((hb, n, 1, b), lambda i: (i, 0, 0, 0))
    kv = lambda x: x.reshape(h, n, b)
    sn, y = pl.pallas_call(
        _wkv_step_kernel,
        out_shape=[jax.ShapeDtypeStruct((h, n, n, b), F32), jax.ShapeDtypeStruct((h, n, 1, b), F32)],
        grid=(h // hb,),
        in_specs=[sblk] + [kblk] * 5 + [vblk],
        out_specs=[sblk, vblk],
        compiler_params=_cp(1),
        name="wkv_step",
    )(s, kv(dvec), kv(an), kv(bvec), kv(kmod), kv(r), v.reshape(h, n, 1, b))
    return sn, y.reshape(h * n, b)


def _wkv_post_kernel(y_ref, r_ref, k_ref, v_ref, a_ref, g_ref, kap_ref, rk_ref, lg_ref, lb_ref, o_ref):
    kmod = k_ref[0] * (1.0 + (a_ref[0] - 1.0) * kap_ref[...])
    bonus = _head_sum(r_ref[0] * kmod * rk_ref[...]) * v_ref[0]
    o_ref[0] = _wkv_readout(y_ref[0], bonus, g_ref[0], lg_ref[...], lb_ref[...])


def _wkv_post(y, r, k, v, a, g, kap, rk, lnx_g, lnx_b):
    gg, t, d = y.shape
    tr = _pick(t, 256, 16)
    row = pl.BlockSpec((1, tr, d), lambda gi, ti: (gi, ti, 0))
    vec = pl.BlockSpec((1, d), lambda gi, ti: (0, 0))
    return pl.pallas_call(
        _wkv_post_kernel,
        out_shape=jax.ShapeDtypeStruct((gg, t, d), BF16),
        grid=(gg, t // tr),
        in_specs=[row] * 6 + [vec] * 4,
        out_specs=row,
        compiler_params=_cp(2),
        name="wkv_post",
    )(y, r, k, v, a, g, kap.reshape(1, d), rk.reshape(1, d), lnx_g.reshape(1, d), lnx_b.reshape(1, d))


def _lru_conv_kernel(x_ref, w_ref, b_ref, buf_ref, xc_ref, *, width):
    acc = b_ref[...] + x_ref[0] * w_ref[width - 1:width, :]
    for s in range(1, width):
        acc = acc + buf_ref[0, :, width - 1 - s, :] * w_ref[width - 1 - s:width - s, :]
    xc_ref[0] = acc


def _lru_conv(proj, w, b, buf):
    g, r, d2 = proj.shape
    d = d2 // 2
    width = w.shape[0]
    tr = _pick(r, 256, 16)
    row = pl.BlockSpec((1, tr, d), lambda gi, t: (gi, t, 0))
    return pl.pallas_call(
        functools.partial(_lru_conv_kernel, width=width),
        out_shape=jax.ShapeDtypeStruct((g, r, d), F32),
        grid=(g, r // tr),
        in_specs=[pl.BlockSpec((1, tr, d), lambda gi, t: (gi, t, 1)),
                  pl.BlockSpec((width, d), lambda gi, t: (0, 0)),
                  pl.BlockSpec((1, d), lambda gi, t: (0, 0)),
                  pl.BlockSpec((1, tr, width - 1, d), lambda gi, t: (gi, t, 0, 0))],
        out_specs=row,
        compiler_params=_cp(2),
        name="lru_conv",
    )(proj, w, b.reshape(1, d), buf)


def _lru_gates_kernel(x_ref, wa_ref, ba_ref, wx_ref, bx_ref, ga_ref, gx_ref):
    x = x_ref[...].astype(BF16)
    ga_ref[...] = _sigmoid(_dot(x, wa_ref[0].astype(BF16)) + ba_ref[...])
    gx_ref[...] = _sigmoid(_dot(x, wx_ref[0].astype(BF16)) + bx_ref[...])


def _lru_gates(xc, wa, ba, wx, bx):
    m, d = xc.shape
    nb, bw, _ = wa.shape
    tm = _pick(m, 1024, 8)
    xs = pl.BlockSpec((tm, bw), lambda n, i: (i, n))
    ws = pl.BlockSpec((1, bw, bw), lambda n, i: (n, 0, 0))
    bs = pl.BlockSpec((1, bw), lambda n, i: (0, n))
    return pl.pallas_call(
        _lru_gates_kernel,
        out_shape=[jax.ShapeDtypeStruct((m, d), F32)] * 2,
        grid=(nb, m // tm),
        in_specs=[xs, ws, bs, ws, bs],
        out_specs=[xs, xs],
        compiler_params=_cp(2),
        name="lru_gates",
    )(xc, wa, ba.reshape(1, d), wx, bx.reshape(1, d))


def _lru_coeffs(ga, gx, xc, lam):
    log_a = -LRU_C * ga * _softplus(-lam)
    a = jnp.exp(log_a)
    mult = jnp.sqrt(1.0 - jnp.exp(2.0 * log_a))
    return a, mult


def _lru_seq_kernel(xin_ref, y_ref, cw_ref, cb_ref, wa_ref, ba_ref, wx_ref, bx_ref, lam_ref, h0_ref, buf_ref,
                    o_ref, hl_ref, nbuf_ref, cconv, carry, abuf, bbuf, hbuf, *, width):
    t = pl.program_id(2)
    tr = xin_ref.shape[1]

    @pl.when(t == 0)
    def _():
        cconv[...] = buf_ref[0]
        carry[...] = jnp.broadcast_to(h0_ref[0], carry.shape)

    x = xin_ref[0]
    c8 = cconv[...]
    xc = cb_ref[...] + x * cw_ref[width - 1:width, :]
    for s in range(1, width):
        xc = xc + _shift_rows(x, c8, s) * cw_ref[width - 1 - s:width - s, :]
    cconv[...] = x[tr - SUBLANE:tr]
    nbuf_ref[0] = x[tr - SUBLANE:tr]
    xb = xc.astype(BF16)
    ga = _sigmoid(_dot(xb, wa_ref[...].astype(BF16)) + ba_ref[...])
    gx = _sigmoid(_dot(xb, wx_ref[...].astype(BF16)) + bx_ref[...])
    a, mult = _lru_coeffs(ga, gx, xc, lam_ref[...])
    row = lax.broadcasted_iota(jnp.int32, a.shape, 0)
    mult = jnp.where(jnp.logical_and(t == 0, row == 0), 1.0, mult)
    abuf[...] = a
    bbuf[...] = mult * gx * xc
    r8 = lax.broadcasted_iota(jnp.int32, (SUBLANE, a.shape[1]), 0)

    def body(i, c):
        off = pl.multiple_of(i * SUBLANE, SUBLANE)
        aa = abuf[pl.ds(off, SUBLANE), :]
        bb = bbuf[pl.ds(off, SUBLANE), :]
        for s in (1, 2, 4):
            a_s = jnp.where(r8 >= s, pltpu.roll(aa, s, 0), 1.0)
            b_s = jnp.where(r8 >= s, pltpu.roll(bb, s, 0), 0.0)
            bb = aa * b_s + bb
            aa = aa * a_s
        h = aa * c + bb
        hbuf[pl.ds(off, SUBLANE), :] = h
        return h[SUBLANE - 1:SUBLANE, :]

    c = lax.fori_loop(0, tr // SUBLANE, body, carry[0:1, :], unroll=LRU_SCAN_UNROLL)
    carry[...] = jnp.broadcast_to(c, carry.shape)
    o_ref[0] = (hbuf[...] * _gelu(y_ref[0])).astype(BF16)

    @pl.when(t == pl.num_programs(2) - 1)
    def _():
        hl_ref[0] = c


def _lru_seq(proj, cw, cb, wa, ba, wx, bx, lam, h0, buf):
    g, r, d2 = proj.shape
    d = d2 // 2
    nb, bw, _ = wa.shape
    width = cw.shape[0]
    tr = _pick(r, 512, 16)
    row = lambda off: pl.BlockSpec((1, tr, bw), lambda gi, n, t: (gi, t, n + off))
    vec = pl.BlockSpec((1, bw), lambda gi, n, t: (0, n))
    wsp = pl.BlockSpec((None, bw, bw), lambda gi, n, t: (n, 0, 0))
    st = pl.BlockSpec((1, 1, bw), lambda gi, n, t: (gi, 0, n))
    st8 = pl.BlockSpec((1, SUBLANE, bw), lambda gi, n, t: (gi, 0, n))
    tile = pltpu.VMEM((tr, bw), F32)
    small = pltpu.VMEM((SUBLANE, bw), F32)
    return pl.pallas_call(
        functools.partial(_lru_seq_kernel, width=width),
        out_shape=[jax.ShapeDtypeStruct((g, r, d), BF16), jax.ShapeDtypeStruct((g, 1, d), F32),
                   jax.ShapeDtypeStruct((g, SUBLANE, d), F32)],
        grid=(g, nb, r // tr),
        in_specs=[row(nb), row(0), pl.BlockSpec((width, bw), lambda gi, n, t: (0, n)), vec,
                  wsp, vec, wsp, vec, vec, st, st8],
        out_specs=[row(0), st, st8],
        scratch_shapes=[small, small, tile, tile, tile],
        compiler_params=_cp(3),
        name="lru_seq",
    )(proj, proj, cw, cb.reshape(1, d), wa, ba.reshape(1, d), wx, bx.reshape(1, d), lam.reshape(1, d), h0, buf)


def _lru_step_kernel(ga_ref, gx_ref, xc_ref, y_ref, lam_ref, h0_ref, o_ref, h_ref):
    xc = xc_ref[0]
    a, mult = _lru_coeffs(ga_ref[0], gx_ref[0], xc, lam_ref[...])
    h = a * h0_ref[0] + mult * gx_ref[0] * xc
    h_ref[0] = h
    o_ref[0] = (h * _gelu(y_ref[0])).astype(BF16)


def _lru_step(ga, gx, xc, proj, lam, h0):
    g, r, d = ga.shape
    row = pl.BlockSpec((1, r, d), lambda i: (0, 0, 0))
    return pl.pallas_call(
        _lru_step_kernel,
        out_shape=[jax.ShapeDtypeStruct((g, r, d), BF16), jax.ShapeDtypeStruct((g, r, d), F32)],
        grid=(1,),
        in_specs=[row, row, row, row, pl.BlockSpec((1, d), lambda i: (0, 0)), row],
        out_specs=[row, row],
        compiler_params=_cp(1),
        name="lru_step",
    )(ga, gx, xc, proj, lam.reshape(1, d), h0)


def _ffn_up_kernel(*refs, seq, tiles_per_seq):
    if seq:
        x_ref, wg_ref, wu_ref, cw_ref, cb_ref, buf_ref, o_ref, nb_ref, wgb, wub, carry = refs
    else:
        x_ref, wg_ref, wu_ref, cw_ref, cb_ref, buf_ref, o_ref, nb_ref, wgb, wub = refs
    i = pl.program_id(1)

    @pl.when(i == 0)
    def _():
        wgb[...] = wg_ref[...].astype(BF16)
        wub[...] = wu_ref[...].astype(BF16)

    tm = x_ref.shape[0]
    ts = _pick(tm, FFN_SUB_ROWS, 16)
    if seq:
        @pl.when(i % tiles_per_seq == 0)
        def _():
            carry[...] = buf_ref[0]

        c8 = carry[...]
    for s0 in range(0, tm, ts):
        rows = slice(s0, s0 + ts)
        x = x_ref[rows, :]
        gt = _dot(x, wgb[...])
        u = _dot(x, wub[...])
        if seq:
            p1 = _shift_rows(gt, c8, 1)
            p2 = _shift_rows(gt, c8, 2)
            c8 = gt[ts - SUBLANE:ts]
        else:
            p2 = buf_ref[rows, 0, :]
            p1 = buf_ref[rows, 1, :]
            nb_ref[rows, 0, :] = p1
            nb_ref[rows, 1, :] = gt
        gc = cb_ref[...] + p2 * cw_ref[0:1, :] + p1 * cw_ref[1:2, :] + gt * cw_ref[2:3, :]
        o_ref[rows, :] = (_gelu(gc) * u).astype(BF16)
    if seq:
        carry[...] = c8
        nb_ref[0] = c8


def _ffn_up(x, wg, wu, layer, cw, cb, buf, seq, rows_per_seq):
    m, k = x.shape
    f = wg.shape[2]
    tn = _pick(f, 512, LANE)
    tm = _pick(rows_per_seq if seq else m, 1024, 16)
    nj = f // tn
    xs = pl.BlockSpec((tm, k), lambda j, i: (i, 0))
    ws = _wspec(wg, layer, k, tn, lambda j, i: j)
    os_ = pl.BlockSpec((tm, tn), lambda j, i: (i, j))
    in_specs = [xs, ws, ws, pl.BlockSpec((3, tn), lambda j, i: (0, j)), pl.BlockSpec((1, tn), lambda j, i: (0, j))]
    scratch = [pltpu.VMEM((k, tn), BF16), pltpu.VMEM((k, tn), BF16)]
    if seq:
        tps = rows_per_seq // tm
        g = m // rows_per_seq
        bspec = pl.BlockSpec((1, SUBLANE, tn), lambda j, i: (i // tps, 0, j))
        in_specs.append(bspec)
        out_shape = [jax.ShapeDtypeStruct((m, f), BF16), jax.ShapeDtypeStruct((g, SUBLANE, f), F32)]
        out_specs = [os_, bspec]
        scratch.append(pltpu.VMEM((SUBLANE, tn), F32))
    else:
        tps = 1
        hspec = pl.BlockSpec((tm, 2, tn), lambda j, i: (i, 0, j))
        in_specs.append(hspec)
        out_shape = [jax.ShapeDtypeStruct((m, f), BF16), jax.ShapeDtypeStruct((m, 2, f), F32)]
        out_specs = [os_, hspec]
    args = [x, wg, wu, cw, cb.reshape(1, f), buf]
    return pl.pallas_call(
        functools.partial(_ffn_up_kernel, seq=seq, tiles_per_seq=tps),
        out_shape=out_shape,
        grid=(nj, m // tm),
        in_specs=in_specs,
        out_specs=out_specs,
        scratch_shapes=scratch,
        compiler_params=_cp(2),
        name="ffn_up",
    )(*args)


def _trunk(xp, xs, mods_p, mods_s, st, P):
    gp, rp, d = xp.shape
    rs = xs.shape[1]
    depth = P["w_mod"].shape[0]
    keys = ("wkv", "shift", "h", "lconv", "fconv")
    out_p = {k: [] for k in keys}
    out_s = {k: [] for k in keys}
    flat = lambda t: t.reshape(-1, t.shape[-1])
    unp = lambda t: t.reshape(gp, rp, t.shape[-1])
    uns = lambda t: t.reshape(1, rs, t.shape[-1])
    hp = hs = None
    for i in range(depth):
        mp, ms = mods_p[i], mods_s[i]
        j = i // 2
        if i % 2 == 0:
            lora = tuple(P[nm][j] for nm in ("rw_w1", "rw_w2", "rw_w0", "rw_a1", "rw_a2", "rw_a0", "rw_g1", "rw_g2"))
            g0, mix = P["norm_g"][i, 0], P["rw_mix"][j]
            pre_p, shift_p = _rwkv_prep(xp, g0, mp, mix, jnp.zeros((gp, 1, d), F32), True, lora)
            pre_s, shift_s = _rwkv_prep(xs, g0, ms, mix, st["shift"][j].reshape(1, rs, d), False, lora)
            xr_p, xk_p, xv_p, w_p, a_p, g_p = pre_p
            xr_s, xk_s, xv_s, w_s, a_s, g_s = pre_s
            rr_p, rr_s = _mm(flat(xr_p), P["rw_wr"], j, x2=flat(xr_s), name="rw_r")
            kk_p, kk_s = _mm(flat(xk_p), P["rw_wk"], j, x2=flat(xk_s), name="rw_k")
            vv_p, vv_s = _mm(flat(xv_p), P["rw_wv"], j, x2=flat(xv_s), name="rw_v")
            heads = (P["rw_kk"][j], P["rw_ka"][j], P["rw_rk"][j], P["rw_lnx_g"][j], P["rw_lnx_b"][j])
            xo_p, s_bd = _wkv_chunked(unp(rr_p), unp(kk_p), unp(vv_p), w_p, a_p, g_p, *heads)
            npairs = d // LANE
            s_new = jnp.stack([s_bd[:, :, :HEAD, :HEAD], s_bd[:, :, HEAD:, HEAD:]], axis=2)
            out_p["wkv"].append(s_new.reshape(gp, 2 * npairs, HEAD, HEAD))
            out_p["shift"].append(shift_p.reshape(gp, d))
            s_t = jnp.transpose(st["wkv"][j], (1, 2, 3, 0))
            vecs = _wkv_vec(kk_s, flat(w_s), flat(a_s), rr_s, vv_s, heads[0], heads[1])
            s_t, y_s = _wkv_step(s_t, *vecs)
            out_s["wkv"].append(jnp.transpose(s_t, (3, 0, 1, 2)))
            out_s["shift"].append(shift_s.reshape(rs, d))
            xo_s = _wkv_post(y_s.T.reshape(1, rs, d), uns(rr_s), uns(kk_s), uns(vv_s), a_s, g_s, *heads[1:])
            mo_p, mo_s, w_out = xo_p, xo_s, P["rw_wo"]
        else:
            proj_p, proj_s = _mm(flat(hp), P["lru_w_in"], j, P["lru_b_in"][j], x2=flat(hs), name="lru_in")
            dr = proj_p.shape[1] // 2
            width = P["lru_conv_w"].shape[1]
            cw, cb, lam = P["lru_conv_w"][j], P["lru_conv_b"][j], P["lru_lambda"][j]
            gates = (P["lru_wa"][j], P["lru_ba"][j], P["lru_wx"][j], P["lru_bx"][j])
            hy_p, hl_p, nb8 = _lru_seq(unp(proj_p), cw, cb, *gates, lam, jnp.zeros((gp, 1, dr), F32),
                                       jnp.zeros((gp, SUBLANE, dr), F32))
            out_p["lconv"].append(nb8[:, SUBLANE - (width - 1):, :])
            out_p["h"].append(hl_p.reshape(gp, dr))
            hist = st["lru_conv"][j]
            xc = _lru_conv(uns(proj_s), cw, cb, hist.reshape(1, rs, width - 1, dr))
            out_s["lconv"].append(jnp.concatenate([hist[:, 1:], proj_s[:, None, dr:]], axis=1))
            ga, gx = _lru_gates(flat(xc), *gates)
            hy_s, hl_s = _lru_step(uns(ga), uns(gx), xc, uns(proj_s), lam, st["lru_h"][j].reshape(1, rs, dr))
            out_s["h"].append(hl_s.reshape(rs, dr))
            mo_p, mo_s, w_out = hy_p, hy_s, P["lru_w_out"]
        g1, g2 = P["norm_g"][i, 1], P["norm_g"][i, 2]
        xp, hf_p = _mm_resid(mo_p, w_out, j, xp, g1, mp, 2, g2, 4, 3, name="mix_out")
        xs, hf_s = _mm_resid(mo_s, w_out, j, xs, g1, ms, 2, g2, 4, 3, name="mix_out")
        fdim = P["ffn_w_gate"].shape[2]
        ffn = (P["ffn_w_gate"], P["ffn_w_up"], i, P["ffn_conv_w"][i], P["ffn_conv_b"][i])
        hact_p, nb8 = _ffn_up(flat(hf_p), *ffn, jnp.zeros((gp, SUBLANE, fdim), F32), True, rp)
        out_p["fconv"].append(nb8[:, SUBLANE - 2:, :])
        hact_s, nhist = _ffn_up(flat(hf_s), *ffn, st["ffn_conv"][i], False, rs)
        out_s["fconv"].append(nhist)
        f_p, f_s = _mm(hact_p, P["ffn_w_down"], i, x2=hact_s, tm_target=512, tn_target=512, name="ffn_down")
        g3 = P["norm_g"][i, 3]
        if i + 1 < depth and (i + 1) % 2 == 1:
            nxt = lambda mod: (P["norm_g"][i + 1, 0], mod, 1, 0)
            xp, hp = _resid_norm(xp, unp(f_p), g3, mp, 5, nxt(mods_p[i + 1]), name="resid_ffn")
            xs, hs = _resid_norm(xs, uns(f_s), g3, ms, 5, nxt(mods_s[i + 1]), name="resid_ffn")
        else:
            xp = _resid_norm(xp, unp(f_p), g3, mp, 5, name="resid_ffn")
            xs = _resid_norm(xs, uns(f_s), g3, ms, 5, name="resid_ffn")
    stack = lambda o: tuple(jnp.stack(o[k]) for k in keys)
    return (xp,) + stack(out_p), (xs,) + stack(out_s)


def kernel(x_prompt, x_sample, c_prompt, c_sample, state_rwkv_wkv, state_rwkv_shift, state_lru_h,
           state_lru_conv, state_ffn_conv, w_mod, b_mod, norm_g, rw_mix, rw_wr, rw_wk, rw_wv, rw_wo,
           rw_w0, rw_w1, rw_w2, rw_a0, rw_a1, rw_a2, rw_g1, rw_g2, rw_kk, rw_ka, rw_rk, rw_lnx_g,
           rw_lnx_b, lru_w_in, lru_b_in, lru_conv_w, lru_conv_b, lru_wa, lru_ba, lru_wx, lru_bx,
           lru_lambda, lru_w_out, ffn_w_gate, ffn_w_up, ffn_w_down, ffn_conv_w, ffn_conv_b):
    P = dict(w_mod=w_mod, b_mod=b_mod, norm_g=norm_g, rw_mix=rw_mix, rw_wr=rw_wr, rw_wk=rw_wk,
             rw_wv=rw_wv, rw_wo=rw_wo, rw_w0=rw_w0, rw_w1=rw_w1, rw_w2=rw_w2, rw_a0=rw_a0, rw_a1=rw_a1,
             rw_a2=rw_a2, rw_g1=rw_g1, rw_g2=rw_g2, rw_kk=rw_kk, rw_ka=rw_ka, rw_rk=rw_rk,
             rw_lnx_g=rw_lnx_g, rw_lnx_b=rw_lnx_b, lru_w_in=lru_w_in, lru_b_in=lru_b_in,
             lru_conv_w=lru_conv_w, lru_conv_b=lru_conv_b, lru_wa=lru_wa, lru_ba=lru_ba, lru_wx=lru_wx,
             lru_bx=lru_bx, lru_lambda=lru_lambda, lru_w_out=lru_w_out, ffn_w_gate=ffn_w_gate,
             ffn_w_up=ffn_w_up, ffn_w_down=ffn_w_down, ffn_conv_w=ffn_conv_w, ffn_conv_b=ffn_conv_b)
    bp, t, d = x_prompt.shape
    bs = x_sample.shape[0]
    depth = w_mod.shape[0]
    c_all = jnp.concatenate([c_prompt, c_sample], axis=0)
    rows = c_all.shape[0]
    rows_p = -(-rows // 16) * 16
    c_all = jnp.pad(c_all, ((0, rows_p - rows), (0, 0)))
    mods_p, mods_s = [], []
    for i in range(depth):
        mod = _mm(c_all, w_mod, i, b_mod[i], pre_act="silu", name="mod")
        mods_p.append(mod[:bp].reshape(bp, 1, N_MOD * d))
        mods_s.append(mod[bp:bp + bs].reshape(1, bs, N_MOD * d))
    st = dict(wkv=state_rwkv_wkv, shift=state_rwkv_shift, lru_h=state_lru_h,
              lru_conv=state_lru_conv, ffn_conv=state_ffn_conv)
    (y_p, *state_p), (y_s, *state_s) = _trunk(x_prompt, x_sample.reshape(1, bs, d), mods_p, mods_s, st, P)
    return (y_p, y_s.reshape(bs, 1, d), *state_p, *state_s)
```

```python
import functools

import jax
import jax.numpy as jnp
from jax import lax
from jax.experimental import pallas as pl
from jax.experimental.pallas import tpu as pltpu

F32 = jnp.float32
BF16 = jnp.bfloat16
ACT_DTYPE = BF16

NORM_EPS = 1e-6
LNX_EPS = 64e-5
LRU_C = 8.0
N_MOD = 6
HEAD = 64
LANE = 128
SUBLANE = 8
CHUNK = 64
PAIR_UNROLL = 16
LRU_SCAN_UNROLL = 4
MM_RESID_SUB_ROWS = 128
FFN_SUB_ROWS = 128
VMEM_LIMIT = 52 * 1024 * 1024


def _cp(n_axes):
    return pltpu.CompilerParams(dimension_semantics=("arbitrary",) * n_axes,
                                vmem_limit_bytes=VMEM_LIMIT)


def _pick(n, target, mult):
    best = None
    for t in range(mult, min(n, target) + 1, mult):
        if n % t == 0:
            best = t
    return best if best is not None else n


def _softplus(z):
    return jnp.maximum(z, 0.0) + jnp.log(1.0 + jnp.exp(-jnp.abs(z)))


def _sigmoid(z):
    return 1.0 / (1.0 + jnp.exp(-z))


def _gelu(x):
    return 0.5 * x * (1.0 + jnp.tanh(0.7978845608028654 * (x + 0.044715 * (x * x * x))))


def _act(x, name):
    if name is None:
        return x
    if name == "tanh":
        return jnp.tanh(x)
    if name == "sigmoid":
        return _sigmoid(x)
    if name == "silu":
        return x * _sigmoid(x)
    raise ValueError(name)


def _rms(x, g):
    ms = jnp.mean(x * x, axis=-1, keepdims=True)
    return x * lax.rsqrt(ms + NORM_EPS) * g


def _dot(x, y):
    return jnp.dot(x, y, preferred_element_type=F32)


def _dot_nt(x, y):
    return lax.dot_general(x, y, (((1,), (1,)), ((), ())), preferred_element_type=F32)


def _split_dot_left(m, x, parts):
    acc = None
    rem = x
    for i in range(parts):
        piece = rem.astype(BF16)
        term = jnp.dot(m, piece, preferred_element_type=F32)
        acc = term if acc is None else acc + term
        if i + 1 < parts:
            rem = rem - piece.astype(F32)
    return acc


def _dot_tn(x, y):
    return lax.dot_general(x, y, (((0,), (0,)), ((), ())), preferred_element_type=F32)


def _head_sum(x):
    d = x.shape[-1]
    first = lax.broadcasted_iota(jnp.int32, (1, LANE), 1) < HEAD
    cols = []
    for s in range(0, d, LANE):
        xs = x[:, s:s + LANE]
        lo = jnp.sum(jnp.where(first, xs, 0.0), axis=-1, keepdims=True)
        hi = jnp.sum(jnp.where(first, 0.0, xs), axis=-1, keepdims=True)
        cols.append(jnp.where(first, lo, hi))
    return cols[0] if len(cols) == 1 else jnp.concatenate(cols, axis=-1)


def _shift_rows(x, carry8, s):
    n = x.shape[0]
    rolled = pltpu.roll(x, s, 0)
    rolled_c = pltpu.roll(carry8, s, 0)
    row8 = lax.broadcasted_iota(jnp.int32, (SUBLANE, x.shape[1]), 0)
    top = jnp.where(row8 < s, rolled_c, rolled[0:SUBLANE])
    if n == SUBLANE:
        return top
    return jnp.concatenate([top, rolled[SUBLANE:]], axis=0)


def _mm_kernel(*refs, has_bias, has_x2, pre_act, act):
    refs = list(refs)
    x_ref, w_ref = refs[:2]
    x2_ref = refs[2] if has_x2 else None
    b_ref = refs[2 + has_x2] if has_bias else None
    o_ref = refs[2 + has_x2 + has_bias]
    o2_ref = refs[3 + has_x2 + has_bias] if has_x2 else None
    wb_ref = refs[-1]

    @pl.when(pl.program_id(1) == 0)
    def _():
        wb_ref[...] = w_ref[...].astype(BF16)

    def project(src, dst):
        x = src[...]
        if pre_act is not None:
            x = _act(x.astype(F32), pre_act)
        acc = _dot(x.astype(BF16), wb_ref[...])
        if has_bias:
            acc = acc + b_ref[...]
        dst[...] = _act(acc, act).astype(dst.dtype)

    project(x_ref, o_ref)
    if has_x2:
        @pl.when(pl.program_id(1) == pl.num_programs(1) - 1)
        def _():
            project(x2_ref, o2_ref)


def _wspec(w, layer, rows, cols, col_of):
    assert w.ndim == 3
    return pl.BlockSpec((None, rows, cols), lambda *ids: (layer, 0, col_of(*ids)))


def _mm(x, w, layer, bias=None, *, x2=None, pre_act=None, act=None, out_dtype=F32, tm_target=1024,
        tn_target=1024, name="mm"):
    m, k = x.shape
    n = w.shape[2]
    tm = _pick(m, tm_target, 16)
    tn = _pick(n, tn_target, LANE)
    in_specs = [pl.BlockSpec((tm, k), lambda j, i: (i, 0)),
                _wspec(w, layer, k, tn, lambda j, i: j)]
    args = [x, w]
    out_shape = [jax.ShapeDtypeStruct((m, n), out_dtype)]
    out_specs = [pl.BlockSpec((tm, tn), lambda j, i: (i, j))]
    if x2 is not None:
        m2 = x2.shape[0]
        in_specs.append(pl.BlockSpec((m2, k), lambda j, i: (0, 0)))
        args.append(x2)
        out_shape.append(jax.ShapeDtypeStruct((m2, n), out_dtype))
        out_specs.append(pl.BlockSpec((m2, tn), lambda j, i: (0, j)))
    if bias is not None:
        in_specs.append(pl.BlockSpec((1, tn), lambda j, i: (0, j)))
        args.append(bias.reshape(1, n))
    res = pl.pallas_call(
        functools.partial(_mm_kernel, has_bias=bias is not None, has_x2=x2 is not None, pre_act=pre_act, act=act),
        out_shape=out_shape,
        grid=(n // tn, m // tm),
        in_specs=in_specs,
        out_specs=out_specs,
        scratch_shapes=[pltpu.VMEM((k, tn), BF16)],
        compiler_params=_cp(2),
        name=name,
    )(*args)
    return res if x2 is not None else res[0]


def _mod_spec(mod, slot, d, tr):
    rm = mod.shape[1]
    if rm == 1:
        return pl.BlockSpec((1, 1, d), lambda g, t: (g, 0, slot))
    return pl.BlockSpec((1, tr, d), lambda g, t: (g, t, slot))


def _resid_norm_kernel(*refs, with_next):
    if with_next:
        x_ref, o_ref, g1_ref, gate_ref, g2_ref, sc_ref, sh_ref, xn_ref, h_ref = refs
    else:
        x_ref, o_ref, g1_ref, gate_ref, xn_ref = refs
    xn = x_ref[0] + gate_ref[0] * _rms(o_ref[0].astype(F32), g1_ref[...])
    xn_ref[0] = xn
    if with_next:
        h_ref[0] = (_rms(xn, g2_ref[...]) * (1.0 + sc_ref[0]) + sh_ref[0]).astype(h_ref.dtype)


def _resid_norm(x, o, g1, mod, gate_slot, nxt=None, name="resid_norm"):
    g, r, d = x.shape
    tr = _pick(r, 256, 16)
    row = pl.BlockSpec((1, tr, d), lambda gi, t: (gi, t, 0))
    vec = pl.BlockSpec((1, d), lambda gi, t: (0, 0))
    in_specs = [row, row, vec, _mod_spec(mod, gate_slot, d, tr)]
    args = [x, o, g1.reshape(1, d), mod]
    out_shape = [jax.ShapeDtypeStruct((g, r, d), F32)]
    out_specs = [row]
    if nxt is not None:
        g2, mod2, sc_slot, sh_slot = nxt
        in_specs += [vec, _mod_spec(mod2, sc_slot, d, tr), _mod_spec(mod2, sh_slot, d, tr)]
        args += [g2.reshape(1, d), mod2, mod2]
        out_shape.append(jax.ShapeDtypeStruct((g, r, d), BF16))
        out_specs.append(row)
    res = pl.pallas_call(
        functools.partial(_resid_norm_kernel, with_next=nxt is not None),
        out_shape=out_shape,
        grid=(g, r // tr),
        in_specs=in_specs,
        out_specs=out_specs,
        compiler_params=_cp(2),
        name=name,
    )(*args)
    return res if nxt is not None else res[0]


def _mm_resid_kernel(a_ref, w_ref, x_ref, g1_ref, gate_ref, g2_ref, sc_ref, sh_ref, xn_ref, h_ref):
    tm = a_ref.shape[1]
    ts = _pick(tm, MM_RESID_SUB_ROWS, 16)

    def rows_of(ref, rows):
        return ref[0] if ref.shape[1] == 1 else ref[0, rows, :]

    for s0 in range(0, tm, ts):
        rows = slice(s0, s0 + ts)
        o = _dot(a_ref[0, rows, :], w_ref[...])
        xn = x_ref[0, rows, :] + rows_of(gate_ref, rows) * _rms(o, g1_ref[...])
        xn_ref[0, rows, :] = xn
        h_ref[0, rows, :] = (_rms(xn, g2_ref[...]) * (1.0 + rows_of(sc_ref, rows))
                             + rows_of(sh_ref, rows)).astype(BF16)


def _mm_resid(a, w, layer, x, g1, mod, gate_slot, g2, sc_slot, sh_slot, name):
    g, r, k = a.shape
    n = w.shape[2]
    wb = w[layer].astype(BF16)
    tm = _pick(r, 256, 16)
    row = lambda c: pl.BlockSpec((1, tm, c), lambda gi, t: (gi, t, 0))
    vec = pl.BlockSpec((1, n), lambda gi, t: (0, 0))
    return pl.pallas_call(
        _mm_resid_kernel,
        out_shape=[jax.ShapeDtypeStruct((g, r, n), F32), jax.ShapeDtypeStruct((g, r, n), BF16)],
        grid=(g, r // tm),
        in_specs=[row(k), pl.BlockSpec((k, n), lambda gi, t: (0, 0)), row(n), vec,
                  _mod_spec(mod, gate_slot, n, tm), vec, _mod_spec(mod, sc_slot, n, tm),
                  _mod_spec(mod, sh_slot, n, tm)],
        out_specs=[row(n), row(n)],
        compiler_params=_cp(2),
        name=name,
    )(a, wb, x, g1.reshape(1, n), mod, g2.reshape(1, n), mod, mod)


def _rwkv_prep_kernel(x_ref, g_ref, sc_ref, sh_ref, mix_ref, shift_ref,
                      w1_ref, w2_ref, w0_ref, a1_ref, a2_ref, a0_ref, g1_ref, g2_ref,
                      xr_ref, xk_ref, xv_ref, wpre_ref, a_ref, gate_ref, shift_out, *scratch, seq):
    h = _rms(x_ref[0], g_ref[...]) * (1.0 + sc_ref[0]) + sh_ref[0]
    tr = h.shape[0]
    if seq:
        carry = scratch[0]
        t = pl.program_id(1)

        @pl.when(t == 0)
        def _():
            carry[...] = jnp.broadcast_to(shift_ref[0], carry.shape)

        hprev = _shift_rows(h, carry[...], 1)
        carry[...] = h[tr - SUBLANE:tr]

        @pl.when(t == pl.num_programs(1) - 1)
        def _():
            shift_out[0] = h[tr - 1:tr]
    else:
        hprev = shift_ref[0]
        shift_out[0] = h
    xx = hprev - h

    def mixed(j):
        return (h + xx * mix_ref[j:j + 1, :]).astype(BF16)

    xr_ref[0] = mixed(0)
    xk_ref[0] = mixed(2)
    xv_ref[0] = mixed(3)
    wpre_ref[0] = w0_ref[...] + _dot(jnp.tanh(_dot(mixed(1), w1_ref[...])).astype(BF16), w2_ref[...])
    a = _sigmoid(a0_ref[...] + _dot(_dot(mixed(4), a1_ref[...]).astype(BF16), a2_ref[...]))
    a_ref[0] = a.astype(a_ref.dtype)
    gate_ref[0] = _dot(_sigmoid(_dot(mixed(5), g1_ref[...])).astype(BF16), g2_ref[...]).astype(gate_ref.dtype)


def _lora_pair(w1, w2):
    r = w1.shape[1]
    rp = -(-r // LANE) * LANE
    if rp != r:
        w1 = jnp.pad(w1, ((0, 0), (0, rp - r)))
        w2 = jnp.pad(w2, ((0, rp - r), (0, 0)))
    return w1.astype(BF16), w2.astype(BF16)


def _rwkv_prep(x, g, mod, mix, shift, seq, lora):
    gg, r, d = x.shape
    tr = _pick(r, 256, 16)
    w1, w2, w0, a1, a2, a0, g1, g2 = lora
    w1, w2 = _lora_pair(w1, w2)
    a1, a2 = _lora_pair(a1, a2)
    g1, g2 = _lora_pair(g1, g2)
    row = pl.BlockSpec((1, tr, d), lambda gi, t: (gi, t, 0))
    vec = pl.BlockSpec((1, d), lambda gi, t: (0, 0))
    full = lambda w: pl.BlockSpec(w.shape, lambda gi, t: (0, 0))
    if seq:
        shift_spec = pl.BlockSpec((1, 1, d), lambda gi, t: (gi, 0, 0))
        shift_shape = (gg, 1, d)
        scratch = [pltpu.VMEM((SUBLANE, d), F32)]
    else:
        shift_spec = row
        shift_shape = (gg, r, d)
        scratch = []
    act = lambda dt: jax.ShapeDtypeStruct((gg, r, d), dt)
    outs = pl.pallas_call(
        functools.partial(_rwkv_prep_kernel, seq=seq),
        out_shape=[act(BF16)] * 3 + [act(F32), act(BF16), act(BF16)] + [jax.ShapeDtypeStruct(shift_shape, F32)],
        grid=(gg, r // tr),
        in_specs=[row, vec, _mod_spec(mod, 1, d, tr), _mod_spec(mod, 0, d, tr),
                  pl.BlockSpec((6, d), lambda gi, t: (0, 0)), shift_spec,
                  full(w1), full(w2), vec, full(a1), full(a2), vec, full(g1), full(g2)],
        out_specs=[row] * 6 + [shift_spec],
        scratch_shapes=scratch,
        compiler_params=_cp(2),
        name="rwkv_prep",
    )(x, g.reshape(1, d), mod, mod, mix, shift, w1, w2, w0.reshape(1, d), a1, a2, a0.reshape(1, d), g1, g2)
    return outs[:6], outs[6]


def _wkv_terms(k, wpre, a, kkp, kap):
    logd = -0.6065306597126334 * _sigmoid(wpre)
    kk = k * kkp
    kk = kk * lax.rsqrt(jnp.maximum(_head_sum(kk * kk), 1e-24))
    kmod = k * (1.0 + (a - 1.0) * kap)
    return logd, kk, kmod


def _wkv_readout(y, bonus, gate, lnx_g, lnx_b):
    inv_n = 1.0 / HEAD
    yc = y - _head_sum(y) * inv_n
    var = _head_sum(yc * yc) * inv_n
    return ((yc * lax.rsqrt(var + LNX_EPS) * lnx_g + lnx_b + bonus) * gate).astype(BF16)


def _wkv_chunk_kernel(r_ref, k_ref, v_ref, w_ref, a_ref, gate_ref, kkp_ref, kap_ref, rk_ref, lg_ref, lb_ref,
                      xo_ref, sout_ref, S, Rt, At, Bt, Kt, Vs, Gc, Ys, Bonus):
    c = pl.program_id(1)
    n_pairs = S.shape[0]
    C = r_ref.shape[1]

    @pl.when(c == 0)
    def _():
        S[...] = jnp.zeros(S.shape, F32)

    a = a_ref[0].astype(F32)
    r = r_ref[0].astype(F32)
    logd, kk, kmod = _wkv_terms(k_ref[0].astype(F32), w_ref[0], a, kkp_ref[...], kap_ref[...])
    ri = lax.broadcasted_iota(jnp.int32, (C, C), 0)
    ci = lax.broadcasted_iota(jnp.int32, (C, C), 1)
    tri = (ri >= ci).astype(BF16)
    cum = _split_dot_left(tri, logd, 3)
    gam = jnp.exp(cum)
    rt = r * gam
    at = -kk * jnp.exp(cum - logd)
    ginv = jnp.exp(-cum)
    bt = kk * a * ginv
    kt = kmod * ginv
    v = v_ref[0].astype(F32)
    Bonus[...] = _head_sum(r * kmod * rk_ref[...]) * v
    for p in range(n_pairs):
        sl = slice(p * LANE, (p + 1) * LANE)
        Rt[p] = rt[:, sl]
        At[p] = at[:, sl]
        Bt[p] = bt[:, sl]
        Kt[p] = kt[:, sl]
        Vs[p] = v[:, sl]
        Gc[p] = jnp.broadcast_to(gam[C - 1:C, sl], (SUBLANE, LANE))

    lane = lax.broadcasted_iota(jnp.int32, (1, LANE), 1)
    first = lane < HEAD
    r2 = lax.broadcasted_iota(jnp.int32, (2 * C, 2 * C), 0)
    c2 = lax.broadcasted_iota(jnp.int32, (2 * C, 2 * C), 1)
    strict = r2 > c2
    incl = r2 >= c2
    rs = lax.broadcasted_iota(jnp.int32, (LANE, LANE), 0)
    cs = lax.broadcasted_iota(jnp.int32, (LANE, LANE), 1)
    same_head = (rs < HEAD) == (cs < HEAD)

    def stack(x):
        return jnp.concatenate([jnp.where(first, x, 0.0), jnp.where(first, 0.0, x)], axis=0).astype(BF16)

    n_fac = max(1, (C - 1).bit_length())

    def pairs_step(ps):
        n = range(len(ps))
        rp, ap, bp = [Rt[p] for p in ps], [At[p] for p in ps], [Bt[p] for p in ps]
        kp, vp = [Kt[p] for p in ps], [Vs[p] for p in ps]
        sp, gc = [S[p] for p in ps], [Gc[p] for p in ps]
        am, rm = [stack(t) for t in ap], [stack(t) for t in rp]
        bm, km = [stack(t) for t in bp], [stack(t) for t in kp]
        ar = [jnp.concatenate([am[i], rm[i]], axis=0) for i in n]
        gb = [_dot_nt(ar[i], bm[i]) for i in n]
        gk = [_dot_nt(ar[i], km[i]) for i in n]
        lp = [jnp.where(strict, t[:2 * C], 0.0) for t in gb]
        lak = [jnp.where(strict, t[:2 * C], 0.0).astype(BF16) for t in gk]
        grbk = [jnp.concatenate([jnp.where(incl, gb[i][2 * C:], 0.0), jnp.where(incl, gk[i][2 * C:], 0.0)],
                                axis=1).astype(BF16) for i in n]
        spb = [t.astype(BF16) for t in sp]
        ars = [_dot_nt(jnp.concatenate([ap[i], rp[i]], axis=0).astype(BF16), spb[i]) for i in n]
        a_s, r_s = [t[:C] for t in ars], [t[C:] for t in ars]
        vst = [jnp.concatenate([t, t], axis=0).astype(BF16) for t in vp]
        x = [jnp.concatenate([a_s[i], a_s[i]], axis=0) + _dot(lak[i], vst[i]) for i in n]
        for f in range(n_fac):
            lh = [t.astype(BF16) for t in lp]
            xh = [t.astype(BF16) for t in x]
            xl = [(x[i] - xh[i].astype(F32)).astype(BF16) for i in n]
            lhs = [jnp.concatenate([t, t], axis=1) for t in lh]
            if f + 1 < n_fac:
                ll = [(lp[i] - lh[i].astype(F32)).astype(BF16) for i in n]
                rhs = [jnp.concatenate([jnp.concatenate([xh[i], lh[i]], axis=1),
                                        jnp.concatenate([xl[i], ll[i]], axis=1)], axis=0) for i in n]
                prod = [_dot(lhs[i], rhs[i]) for i in n]
                x = [x[i] + prod[i][:, :LANE] for i in n]
                lp = [t[:, LANE:] for t in prod]
            else:
                x = [x[i] + _dot(lhs[i], jnp.concatenate([xh[i], xl[i]], axis=0)) for i in n]
        yst = [jnp.concatenate([r_s[i], r_s[i]], axis=0)
               + _dot(grbk[i], jnp.concatenate([x[i].astype(BF16), vst[i]], axis=0)) for i in n]
        u = [jnp.where(first, t[:C], t[C:]) for t in x]
        ds = [_dot_tn(jnp.concatenate([u[i], vp[i]], axis=0).astype(BF16),
                      jnp.concatenate([bp[i], kp[i]], axis=0).astype(BF16)) for i in n]
        for i, p in enumerate(ps):
            Ys[p] = jnp.where(first, yst[i][:C], yst[i][C:])
            S[p] = jnp.where(same_head, (sp[i] + ds[i]) * gc[i][0:1, :], 0.0)

    for p0 in range(0, n_pairs, PAIR_UNROLL):
        pairs_step(list(range(p0, min(p0 + PAIR_UNROLL, n_pairs))))

    y = jnp.concatenate([Ys[p] for p in range(n_pairs)], axis=-1) if n_pairs > 1 else Ys[0]
    xo_ref[0] = _wkv_readout(y, Bonus[...], gate_ref[0].astype(F32), lg_ref[...], lb_ref[...])

    @pl.when(c == pl.num_programs(1) - 1)
    def _():
        sout_ref[0] = S[...]


def _wkv_chunked(r, k, v, wpre, a, gate, kkp, kap, rk, lnx_g, lnx_b):
    g, t, d = r.shape
    C = CHUNK
    np_ = d // LANE
    row = pl.BlockSpec((1, C, d), lambda gi, ci: (gi, ci, 0))
    vec = pl.BlockSpec((1, d), lambda gi, ci: (0, 0))
    pair_buf = pltpu.VMEM((np_, C, LANE), F32)
    return pl.pallas_call(
        _wkv_chunk_kernel,
        out_shape=[jax.ShapeDtypeStruct((g, t, d), BF16),
                   jax.ShapeDtypeStruct((g, np_, LANE, LANE), F32)],
        grid=(g, t // C),
        in_specs=[row] * 6 + [vec] * 5,
        out_specs=[row, pl.BlockSpec((1, np_, LANE, LANE), lambda gi, ci: (gi, 0, 0, 0))],
        scratch_shapes=[pltpu.VMEM((np_, LANE, LANE), F32), pair_buf, pair_buf, pair_buf, pair_buf, pair_buf,
                        pltpu.VMEM((np_, SUBLANE, LANE), F32), pair_buf, pltpu.VMEM((C, d), F32)],
        compiler_params=_cp(2),
        name="wkv_chunk",
    )(r, k, v, wpre, a, gate, *[p.reshape(1, d) for p in (kkp, kap, rk, lnx_g, lnx_b)])


def _wkv_vec_kernel(k_ref, w_ref, a_ref, r_ref, v_ref, kkp_ref, kap_ref,
                    d_ref, an_ref, b_ref, km_ref, rt_ref, vt_ref):
    a = a_ref[...].astype(F32)
    logd, kk, kmod = _wkv_terms(k_ref[...].astype(F32), w_ref[...], a, kkp_ref[...], kap_ref[...])
    d_ref[...] = jnp.exp(logd).T
    an_ref[...] = (-kk).T
    b_ref[...] = (kk * a).T
    km_ref[...] = kmod.T
    rt_ref[...] = r_ref[...].astype(F32).T
    vt_ref[...] = v_ref[...].astype(F32).T


def _wkv_vec(k, wpre, a, r, v, kkp, kap):
    b, d = k.shape
    full = pl.BlockSpec((b, d), lambda i: (0, 0))
    fullt = pl.BlockSpec((d, b), lambda i: (0, 0))
    vec = pl.BlockSpec((1, d), lambda i: (0, 0))
    return pl.pallas_call(
        _wkv_vec_kernel,
        out_shape=[jax.ShapeDtypeStruct((d, b), F32)] * 6,
        grid=(1,),
        in_specs=[full] * 5 + [vec, vec],
        out_specs=[fullt] * 6,
        compiler_params=_cp(1),
        name="wkv_vec",
    )(k, wpre, a, r, v, kkp.reshape(1, d), kap.reshape(1, d))


def _wkv_step_kernel(s_ref, d_ref, a_ref, b_ref, k_ref, r_ref, v_ref, sn_ref, y_ref):
    s = s_ref[...]
    key = lambda ref: ref[...][:, None]
    sa = jnp.sum(s * key(a_ref), axis=2, keepdims=True)
    sn = s * key(d_ref) + sa * key(b_ref) + v_ref[...] * key(k_ref)
    sn_ref[...] = sn
    y_ref[...] = jnp.sum(sn * key(r_ref), axis=2, keepdims=True)


def _wkv_step(s, dvec, an, bvec, kmod, r, v):
    h, n, _, b = s.shape
    hb = _pick(h, 2, 1)
    sblk = pl.BlockSpec((hb, n, n, b), lambda i: (i, 0, 0, 0))
    kblk = pl.BlockSpec((hb, n, b), lambda i: (i, 0, 0))
    vblk = pl.BlockSpec((hb, n, 1, b), lambda i: (i, 0, 0, 0))
    kv = lambda x: x.reshape(h, n, b)
    sn, y = pl.pallas_call(
        _wkv_step_kernel,
        out_shape=[jax.ShapeDtypeStruct((h, n, n, b), F32), jax.ShapeDtypeStruct((h, n, 1, b), F32)],
        grid=(h // hb,),
        in_specs=[sblk] + [kblk] * 5 + [vblk],
        out_specs=[sblk, vblk],
        compiler_params=_cp(1),
        name="wkv_step",
    )(s, kv(dvec), kv(an), kv(bvec), kv(kmod), kv(r), v.reshape(h, n, 1, b))
    return sn, y.reshape(h * n, b)


def _wkv_post_kernel(y_ref, r_ref, k_ref, v_ref, a_ref, g_ref, kap_ref, rk_ref, lg_ref, lb_ref, o_ref):
    f32 = lambda ref: ref[0].astype(F32)
    kmod = f32(k_ref) * (1.0 + (f32(a_ref) - 1.0) * kap_ref[...])
    bonus = _head_sum(f32(r_ref) * kmod * rk_ref[...]) * f32(v_ref)
    o_ref[0] = _wkv_readout(y_ref[0], bonus, f32(g_ref), lg_ref[...], lb_ref[...])


def _wkv_post(y, r, k, v, a, g, kap, rk, lnx_g, lnx_b):
    gg, t, d = y.shape
    tr = _pick(t, 256, 16)
    row = pl.BlockSpec((1, tr, d), lambda gi, ti: (gi, ti, 0))
    vec = pl.BlockSpec((1, d), lambda gi, ti: (0, 0))
    return pl.pallas_call(
        _wkv_post_kernel,
        out_shape=jax.ShapeDtypeStruct((gg, t, d), BF16),
        grid=(gg, t // tr),
        in_specs=[row] * 6 + [vec] * 4,
        out_specs=row,
        compiler_params=_cp(2),
        name="wkv_post",
    )(y, r, k, v, a, g, kap.reshape(1, d), rk.reshape(1, d), lnx_g.reshape(1, d), lnx_b.reshape(1, d))


def _lru_conv_kernel(x_ref, w_ref, b_ref, buf_ref, xc_ref, *, width):
    acc = b_ref[...] + x_ref[0].astype(F32) * w_ref[width - 1:width, :]
    for s in range(1, width):
        acc = acc + buf_ref[0, :, width - 1 - s, :] * w_ref[width - 1 - s:width - s, :]
    xc_ref[0] = acc


def _lru_conv(proj, w, b, buf):
    g, r, d2 = proj.shape
    d = d2 // 2
    width = w.shape[0]
    tr = _pick(r, 256, 16)
    row = pl.BlockSpec((1, tr, d), lambda gi, t: (gi, t, 0))
    return pl.pallas_call(
        functools.partial(_lru_conv_kernel, width=width),
        out_shape=jax.ShapeDtypeStruct((g, r, d), F32),
        grid=(g, r // tr),
        in_specs=[pl.BlockSpec((1, tr, d), lambda gi, t: (gi, t, 1)),
                  pl.BlockSpec((width, d), lambda gi, t: (0, 0)),
                  pl.BlockSpec((1, d), lambda gi, t: (0, 0)),
                  pl.BlockSpec((1, tr, width - 1, d), lambda gi, t: (gi, t, 0, 0))],
        out_specs=row,
        compiler_params=_cp(2),
        name="lru_conv",
    )(proj, w, b.reshape(1, d), buf)


def _lru_gates_kernel(x_ref, wa_ref, ba_ref, wx_ref, bx_ref, ga_ref, gx_ref):
    x = x_ref[...].astype(BF16)
    ga_ref[...] = _sigmoid(_dot(x, wa_ref[0].astype(BF16)) + ba_ref[...])
    gx_ref[...] = _sigmoid(_dot(x, wx_ref[0].astype(BF16)) + bx_ref[...])


def _lru_gates(xc, wa, ba, wx, bx):
    m, d = xc.shape
    nb, bw, _ = wa.shape
    tm = _pick(m, 1024, 8)
    xs = pl.BlockSpec((tm, bw), lambda n, i: (i, n))
    ws = pl.BlockSpec((1, bw, bw), lambda n, i: (n, 0, 0))
    bs = pl.BlockSpec((1, bw), lambda n, i: (0, n))
    return pl.pallas_call(
        _lru_gates_kernel,
        out_shape=[jax.ShapeDtypeStruct((m, d), F32)] * 2,
        grid=(nb, m // tm),
        in_specs=[xs, ws, bs, ws, bs],
        out_specs=[xs, xs],
        compiler_params=_cp(2),
        name="lru_gates",
    )(xc, wa, ba.reshape(1, d), wx, bx.reshape(1, d))


def _lru_coeffs(ga, gx, xc, lam):
    log_a = -LRU_C * ga * _softplus(-lam)
    a = jnp.exp(log_a)
    mult = jnp.sqrt(1.0 - jnp.exp(2.0 * log_a))
    return a, mult


def _lru_seq_kernel(xin_ref, y_ref, cw_ref, cb_ref, wa_ref, ba_ref, wx_ref, bx_ref, lam_ref, h0_ref, buf_ref,
                    o_ref, hl_ref, nbuf_ref, cconv, carry, abuf, bbuf, hbuf, *, width):
    t = pl.program_id(2)
    tr = xin_ref.shape[1]

    @pl.when(t == 0)
    def _():
        cconv[...] = buf_ref[0]
        carry[...] = jnp.broadcast_to(h0_ref[0], carry.shape)

    x = xin_ref[0].astype(F32)
    c8 = cconv[...]
    xc = cb_ref[...] + x * cw_ref[width - 1:width, :]
    for s in range(1, width):
        xc = xc + _shift_rows(x, c8, s) * cw_ref[width - 1 - s:width - s, :]
    cconv[...] = x[tr - SUBLANE:tr]
    nbuf_ref[0] = x[tr - SUBLANE:tr]
    xb = xc.astype(BF16)
    ga = _sigmoid(_dot(xb, wa_ref[...].astype(BF16)) + ba_ref[...])
    gx = _sigmoid(_dot(xb, wx_ref[...].astype(BF16)) + bx_ref[...])
    a, mult = _lru_coeffs(ga, gx, xc, lam_ref[...])
    row = lax.broadcasted_iota(jnp.int32, a.shape, 0)
    mult = jnp.where(jnp.logical_and(t == 0, row == 0), 1.0, mult)
    abuf[...] = a
    bbuf[...] = mult * gx * xc
    r8 = lax.broadcasted_iota(jnp.int32, (SUBLANE, a.shape[1]), 0)

    def body(i, c):
        off = pl.multiple_of(i * SUBLANE, SUBLANE)
        aa = abuf[pl.ds(off, SUBLANE), :]
        bb = bbuf[pl.ds(off, SUBLANE), :]
        for s in (1, 2, 4):
            a_s = jnp.where(r8 >= s, pltpu.roll(aa, s, 0), 1.0)
            b_s = jnp.where(r8 >= s, pltpu.roll(bb, s, 0), 0.0)
            bb = aa * b_s + bb
            aa = aa * a_s
        h = aa * c + bb
        hbuf[pl.ds(off, SUBLANE), :] = h
        return h[SUBLANE - 1:SUBLANE, :]

    c = lax.fori_loop(0, tr // SUBLANE, body, carry[0:1, :], unroll=LRU_SCAN_UNROLL)
    carry[...] = jnp.broadcast_to(c, carry.shape)
    o_ref[0] = (hbuf[...] * _gelu(y_ref[0].astype(F32))).astype(BF16)

    @pl.when(t == pl.num_programs(2) - 1)
    def _():
        hl_ref[0] = c


def _lru_seq(proj, cw, cb, wa, ba, wx, bx, lam, h0, buf):
    g, r, d2 = proj.shape
    d = d2 // 2
    nb, bw, _ = wa.shape
    width = cw.shape[0]
    tr = _pick(r, 512, 16)
    row = lambda off: pl.BlockSpec((1, tr, bw), lambda gi, n, t: (gi, t, n + off))
    vec = pl.BlockSpec((1, bw), lambda gi, n, t: (0, n))
    wsp = pl.BlockSpec((None, bw, bw), lambda gi, n, t: (n, 0, 0))
    st = pl.BlockSpec((1, 1, bw), lambda gi, n, t: (gi, 0, n))
    st8 = pl.BlockSpec((1, SUBLANE, bw), lambda gi, n, t: (gi, 0, n))
    tile = pltpu.VMEM((tr, bw), F32)
    small = pltpu.VMEM((SUBLANE, bw), F32)
    return pl.pallas_call(
        functools.partial(_lru_seq_kernel, width=width),
        out_shape=[jax.ShapeDtypeStruct((g, r, d), BF16), jax.ShapeDtypeStruct((g, 1, d), F32),
                   jax.ShapeDtypeStruct((g, SUBLANE, d), F32)],
        grid=(g, nb, r // tr),
        in_specs=[row(nb), row(0), pl.BlockSpec((width, bw), lambda gi, n, t: (0, n)), vec,
                  wsp, vec, wsp, vec, vec, st, st8],
        out_specs=[row(0), st, st8],
        scratch_shapes=[small, small, tile, tile, tile],
        compiler_params=_cp(3),
        name="lru_seq",
    )(proj, proj, cw, cb.reshape(1, d), wa, ba.reshape(1, d), wx, bx.reshape(1, d), lam.reshape(1, d), h0, buf)


def _lru_step_kernel(ga_ref, gx_ref, xc_ref, y_ref, lam_ref, h0_ref, o_ref, h_ref):
    xc = xc_ref[0]
    a, mult = _lru_coeffs(ga_ref[0], gx_ref[0], xc, lam_ref[...])
    h = a * h0_ref[0] + mult * gx_ref[0] * xc
    h_ref[0] = h
    o_ref[0] = (h * _gelu(y_ref[0].astype(F32))).astype(BF16)


def _lru_step(ga, gx, xc, proj, lam, h0):
    g, r, d = ga.shape
    row = pl.BlockSpec((1, r, d), lambda i: (0, 0, 0))
    return pl.pallas_call(
        _lru_step_kernel,
        out_shape=[jax.ShapeDtypeStruct((g, r, d), BF16), jax.ShapeDtypeStruct((g, r, d), F32)],
        grid=(1,),
        in_specs=[row, row, row, row, pl.BlockSpec((1, d), lambda i: (0, 0)), row],
        out_specs=[row, row],
        compiler_params=_cp(1),
        name="lru_step",
    )(ga, gx, xc, proj, lam.reshape(1, d), h0)


def _ffn_up_kernel(x_ref, wg_ref, wu_ref, cw_ref, cb_ref, buf_ref, x2_ref, hist2_ref,
                   o_ref, nb_ref, o2_ref, nhist2_ref, wgb, wub, carry, *, tiles_per_seq):
    i = pl.program_id(1)

    @pl.when(i == 0)
    def _():
        wgb[...] = wg_ref[...].astype(BF16)
        wub[...] = wu_ref[...].astype(BF16)

    def gated(gt, p1, p2, u):
        gc = cb_ref[...] + p2 * cw_ref[0:1, :] + p1 * cw_ref[1:2, :] + gt * cw_ref[2:3, :]
        return (_gelu(gc) * u).astype(BF16)

    tm = x_ref.shape[0]
    ts = _pick(tm, FFN_SUB_ROWS, 16)

    @pl.when(i % tiles_per_seq == 0)
    def _():
        carry[...] = buf_ref[0]

    c8 = carry[...]
    for s0 in range(0, tm, ts):
        rows = slice(s0, s0 + ts)
        x = x_ref[rows, :]
        gt = _dot(x, wgb[...])
        u = _dot(x, wub[...])
        o_ref[rows, :] = gated(gt, _shift_rows(gt, c8, 1), _shift_rows(gt, c8, 2), u)
        c8 = gt[ts - SUBLANE:ts]
    carry[...] = c8
    nb_ref[0] = c8

    @pl.when(i == pl.num_programs(1) - 1)
    def _():
        x2 = x2_ref[...]
        gt = _dot(x2, wgb[...])
        p1 = hist2_ref[:, 1, :]
        o2_ref[...] = gated(gt, p1, hist2_ref[:, 0, :], _dot(x2, wub[...]))
        nhist2_ref[:, 0, :] = p1
        nhist2_ref[:, 1, :] = gt


def _ffn_up(x, wg, wu, layer, cw, cb, buf, rows_per_seq, x2, hist2):
    m, k = x.shape
    m2 = x2.shape[0]
    f = wg.shape[2]
    tn = _pick(f, 512, LANE)
    tm = _pick(rows_per_seq, 1024, 16)
    tps = rows_per_seq // tm
    g = m // rows_per_seq
    ws = _wspec(wg, layer, k, tn, lambda j, i: j)
    bspec = pl.BlockSpec((1, SUBLANE, tn), lambda j, i: (i // tps, 0, j))
    hspec = pl.BlockSpec((m2, 2, tn), lambda j, i: (0, 0, j))
    o2spec = pl.BlockSpec((m2, tn), lambda j, i: (0, j))
    return pl.pallas_call(
        functools.partial(_ffn_up_kernel, tiles_per_seq=tps),
        out_shape=[jax.ShapeDtypeStruct((m, f), BF16), jax.ShapeDtypeStruct((g, SUBLANE, f), F32),
                   jax.ShapeDtypeStruct((m2, f), BF16), jax.ShapeDtypeStruct((m2, 2, f), F32)],
        grid=(f // tn, m // tm),
        in_specs=[pl.BlockSpec((tm, k), lambda j, i: (i, 0)), ws, ws,
                  pl.BlockSpec((3, tn), lambda j, i: (0, j)), pl.BlockSpec((1, tn), lambda j, i: (0, j)),
                  bspec, pl.BlockSpec((m2, k), lambda j, i: (0, 0)), hspec],
        out_specs=[pl.BlockSpec((tm, tn), lambda j, i: (i, j)), bspec, o2spec, hspec],
        scratch_shapes=[pltpu.VMEM((k, tn), BF16), pltpu.VMEM((k, tn), BF16), pltpu.VMEM((SUBLANE, tn), F32)],
        compiler_params=_cp(2),
        name="ffn_up",
    )(x, wg, wu, cw, cb.reshape(1, f), buf, x2, hist2)


def _trunk(xp, xs, mods_p, mods_s, st, P):
    gp, rp, d = xp.shape
    rs = xs.shape[1]
    depth = P["w_mod"].shape[0]
    keys = ("wkv", "shift", "h", "lconv", "fconv")
    out_p = {k: [] for k in keys}
    out_s = {k: [] for k in keys}
    flat = lambda t: t.reshape(-1, t.shape[-1])
    unp = lambda t: t.reshape(gp, rp, t.shape[-1])
    uns = lambda t: t.reshape(1, rs, t.shape[-1])
    hp = hs = None
    for i in range(depth):
        mp, ms = mods_p[i], mods_s[i]
        j = i // 2
        if i % 2 == 0:
            lora = tuple(P[nm][j] for nm in ("rw_w1", "rw_w2", "rw_w0", "rw_a1", "rw_a2", "rw_a0", "rw_g1", "rw_g2"))
            g0, mix = P["norm_g"][i, 0], P["rw_mix"][j]
            pre_p, shift_p = _rwkv_prep(xp, g0, mp, mix, jnp.zeros((gp, 1, d), F32), True, lora)
            pre_s, shift_s = _rwkv_prep(xs, g0, ms, mix, st["shift"][j].reshape(1, rs, d), False, lora)
            xr_p, xk_p, xv_p, w_p, a_p, g_p = pre_p
            xr_s, xk_s, xv_s, w_s, a_s, g_s = pre_s
            rr_p, rr_s = _mm(flat(xr_p), P["rw_wr"], j, x2=flat(xr_s), out_dtype=ACT_DTYPE, name="rw_r")
            kk_p, kk_s = _mm(flat(xk_p), P["rw_wk"], j, x2=flat(xk_s), out_dtype=ACT_DTYPE, name="rw_k")
            vv_p, vv_s = _mm(flat(xv_p), P["rw_wv"], j, x2=flat(xv_s), out_dtype=ACT_DTYPE, name="rw_v")
            heads = (P["rw_kk"][j], P["rw_ka"][j], P["rw_rk"][j], P["rw_lnx_g"][j], P["rw_lnx_b"][j])
            xo_p, s_bd = _wkv_chunked(unp(rr_p), unp(kk_p), unp(vv_p), w_p, a_p, g_p, *heads)
            npairs = d // LANE
            s_new = jnp.stack([s_bd[:, :, :HEAD, :HEAD], s_bd[:, :, HEAD:, HEAD:]], axis=2)
            out_p["wkv"].append(s_new.reshape(gp, 2 * npairs, HEAD, HEAD))
            out_p["shift"].append(shift_p.reshape(gp, d))
            s_t = jnp.transpose(st["wkv"][j], (1, 2, 3, 0))
            vecs = _wkv_vec(kk_s, flat(w_s), flat(a_s), rr_s, vv_s, heads[0], heads[1])
            s_t, y_s = _wkv_step(s_t, *vecs)
            out_s["wkv"].append(jnp.transpose(s_t, (3, 0, 1, 2)))
            out_s["shift"].append(shift_s.reshape(rs, d))
            xo_s = _wkv_post(y_s.T.reshape(1, rs, d), uns(rr_s), uns(kk_s), uns(vv_s), a_s, g_s, *heads[1:])
            mo_p, mo_s, w_out = xo_p, xo_s, P["rw_wo"]
        else:
            proj_p, proj_s = _mm(flat(hp), P["lru_w_in"], j, P["lru_b_in"][j], x2=flat(hs), out_dtype=ACT_DTYPE,
                                 name="lru_in")
            dr = proj_p.shape[1] // 2
            width = P["lru_conv_w"].shape[1]
            cw, cb, lam = P["lru_conv_w"][j], P["lru_conv_b"][j], P["lru_lambda"][j]
            gates = (P["lru_wa"][j], P["lru_ba"][j], P["lru_wx"][j], P["lru_bx"][j])
            hy_p, hl_p, nb8 = _lru_seq(unp(proj_p), cw, cb, *gates, lam, jnp.zeros((gp, 1, dr), F32),
                                       jnp.zeros((gp, SUBLANE, dr), F32))
            out_p["lconv"].append(nb8[:, SUBLANE - (width - 1):, :])
            out_p["h"].append(hl_p.reshape(gp, dr))
            hist = st["lru_conv"][j]
            xc = _lru_conv(uns(proj_s), cw, cb, hist.reshape(1, rs, width - 1, dr))
            out_s["lconv"].append(jnp.concatenate([hist[:, 1:], proj_s[:, None, dr:]], axis=1))
            ga, gx = _lru_gates(flat(xc), *gates)
            hy_s, hl_s = _lru_step(uns(ga), uns(gx), xc, uns(proj_s), lam, st["lru_h"][j].reshape(1, rs, dr))
            out_s["h"].append(hl_s.reshape(rs, dr))
            mo_p, mo_s, w_out = hy_p, hy_s, P["lru_w_out"]
        g1, g2 = P["norm_g"][i, 1], P["norm_g"][i, 2]
        xp, hf_p = _mm_resid(mo_p, w_out, j, xp, g1, mp, 2, g2, 4, 3, name="mix_out")
        xs, hf_s = _mm_resid(mo_s, w_out, j, xs, g1, ms, 2, g2, 4, 3, name="mix_out")
        fdim = P["ffn_w_gate"].shape[2]
        hact_p, nb8, hact_s, nhist = _ffn_up(
            flat(hf_p), P["ffn_w_gate"], P["ffn_w_up"], i, P["ffn_conv_w"][i], P["ffn_conv_b"][i],
            jnp.zeros((gp, SUBLANE, fdim), F32), rp, flat(hf_s), st["ffn_conv"][i])
        out_p["fconv"].append(nb8[:, SUBLANE - 2:, :])
        out_s["fconv"].append(nhist)
        f_p, f_s = _mm(hact_p, P["ffn_w_down"], i, x2=hact_s, out_dtype=ACT_DTYPE, tm_target=512, tn_target=512,
                       name="ffn_down")
        g3 = P["norm_g"][i, 3]
        if i + 1 < depth and (i + 1) % 2 == 1:
            nxt = lambda mod: (P["norm_g"][i + 1, 0], mod, 1, 0)
            xp, hp = _resid_norm(xp, unp(f_p), g3, mp, 5, nxt(mods_p[i + 1]), name="resid_ffn")
            xs, hs = _resid_norm(xs, uns(f_s), g3, ms, 5, nxt(mods_s[i + 1]), name="resid_ffn")
        else:
            xp = _resid_norm(xp, unp(f_p), g3, mp, 5, name="resid_ffn")
            xs = _resid_norm(xs, uns(f_s), g3, ms, 5, name="resid_ffn")
    stack = lambda o: tuple(jnp.stack(o[k]) for k in keys)
    return (xp,) + stack(out_p), (xs,) + stack(out_s)


def kernel(x_prompt, x_sample, c_prompt, c_sample, state_rwkv_wkv, state_rwkv_shift, state_lru_h,
           state_lru_conv, state_ffn_conv, w_mod, b_mod, norm_g, rw_mix, rw_wr, rw_wk, rw_wv, rw_wo,
           rw_w0, rw_w1, rw_w2, rw_a0, rw_a1, rw_a2, rw_g1, rw_g2, rw_kk, rw_ka, rw_rk, rw_lnx_g,
           rw_lnx_b, lru_w_in, lru_b_in, lru_conv_w, lru_conv_b, lru_wa, lru_ba, lru_wx, lru_bx,
           lru_lambda, lru_w_out, ffn_w_gate, ffn_w_up, ffn_w_down, ffn_conv_w, ffn_conv_b):
    P = dict(w_mod=w_mod, b_mod=b_mod, norm_g=norm_g, rw_mix=rw_mix, rw_wr=rw_wr, rw_wk=rw_wk,
             rw_wv=rw_wv, rw_wo=rw_wo, rw_w0=rw_w0, rw_w1=rw_w1, rw_w2=rw_w2, rw_a0=rw_a0, rw_a1=rw_a1,
             rw_a2=rw_a2, rw_g1=rw_g1, rw_g2=rw_g2, rw_kk=rw_kk, rw_ka=rw_ka, rw_rk=rw_rk,
             rw_lnx_g=rw_lnx_g, rw_lnx_b=rw_lnx_b, lru_w_in=lru_w_in, lru_b_in=lru_b_in,
             lru_conv_w=lru_conv_w, lru_conv_b=lru_conv_b, lru_wa=lru_wa, lru_ba=lru_ba, lru_wx=lru_wx,
             lru_bx=lru_bx, lru_lambda=lru_lambda, lru_w_out=lru_w_out, ffn_w_gate=ffn_w_gate,
             ffn_w_up=ffn_w_up, ffn_w_down=ffn_w_down, ffn_conv_w=ffn_conv_w, ffn_conv_b=ffn_conv_b)
    bp, t, d = x_prompt.shape
    bs = x_sample.shape[0]
    depth = w_mod.shape[0]
    c_all = jnp.concatenate([c_prompt, c_sample], axis=0)
    rows = c_all.shape[0]
    rows_p = -(-rows // 16) * 16
    c_all = jnp.pad(c_all, ((0, rows_p - rows), (0, 0)))
    mods_p, mods_s = [], []
    for i in range(depth):
        mod = _mm(c_all, w_mod, i, b_mod[i], pre_act="silu", name="mod")
        mods_p.append(mod[:bp].reshape(bp, 1, N_MOD * d))
        mods_s.append(mod[bp:bp + bs].reshape(1, bs, N_MOD * d))
    st = dict(wkv=state_rwkv_wkv, shift=state_rwkv_shift, lru_h=state_lru_h,
              lru_conv=state_lru_conv, ffn_conv=state_ffn_conv)
    (y_p, *state_p), (y_s, *state_s) = _trunk(x_prompt, x_sample.reshape(1, bs, d), mods_p, mods_s, st, P)
    return (y_p, y_s.reshape(bs, 1, d), *state_p, *state_s)
```

```python
import functools

import jax
import jax.numpy as jnp
from jax import lax
from jax.experimental import pallas as pl
from jax.experimental.pallas import tpu as pltpu

F32 = jnp.float32
BF16 = jnp.bfloat16
ACT_DTYPE = F32

NORM_EPS = 1e-6
LNX_EPS = 64e-5
LRU_C = 8.0
N_MOD = 6
HEAD = 64
LANE = 128
SUBLANE = 8
CHUNK = 64
PAIR_UNROLL = 16
LRU_SCAN_UNROLL = 4
MM_RESID_SUB_ROWS = 128
FFN_SUB_ROWS = 128
VMEM_LIMIT = 52 * 1024 * 1024


def _cp(n_axes):
    return pltpu.CompilerParams(dimension_semantics=("arbitrary",) * n_axes,
                                vmem_limit_bytes=VMEM_LIMIT)


def _pick(n, target, mult):
    best = None
    for t in range(mult, min(n, target) + 1, mult):
        if n % t == 0:
            best = t
    return best if best is not None else n


def _softplus(z):
    return jnp.maximum(z, 0.0) + jnp.log(1.0 + jnp.exp(-jnp.abs(z)))


def _sigmoid(z):
    return 1.0 / (1.0 + jnp.exp(-z))


def _gelu(x):
    return 0.5 * x * (1.0 + jnp.tanh(0.7978845608028654 * (x + 0.044715 * (x * x * x))))


def _act(x, name):
    if name is None:
        return x
    if name == "tanh":
        return jnp.tanh(x)
    if name == "sigmoid":
        return _sigmoid(x)
    if name == "silu":
        return x * _sigmoid(x)
    raise ValueError(name)


def _rms(x, g):
    ms = jnp.mean(x * x, axis=-1, keepdims=True)
    return x * lax.rsqrt(ms + NORM_EPS) * g


def _dot(x, y):
    return jnp.dot(x, y, preferred_element_type=F32)


def _dot_nt(x, y):
    return lax.dot_general(x, y, (((1,), (1,)), ((), ())), preferred_element_type=F32)


def _split_dot_left(m, x, parts):
    acc = None
    rem = x
    for i in range(parts):
        piece = rem.astype(BF16)
        term = jnp.dot(m, piece, preferred_element_type=F32)
        acc = term if acc is None else acc + term
        if i + 1 < parts:
            rem = rem - piece.astype(F32)
    return acc


def _dot_tn(x, y):
    return lax.dot_general(x, y, (((0,), (0,)), ((), ())), preferred_element_type=F32)


def _head_sum(x):
    d = x.shape[-1]
    first = lax.broadcasted_iota(jnp.int32, (1, LANE), 1) < HEAD
    cols = []
    for s in range(0, d, LANE):
        xs = x[:, s:s + LANE]
        lo = jnp.sum(jnp.where(first, xs, 0.0), axis=-1, keepdims=True)
        hi = jnp.sum(jnp.where(first, 0.0, xs), axis=-1, keepdims=True)
        cols.append(jnp.where(first, lo, hi))
    return cols[0] if len(cols) == 1 else jnp.concatenate(cols, axis=-1)


def _shift_rows(x, carry8, s):
    n = x.shape[0]
    rolled = pltpu.roll(x, s, 0)
    rolled_c = pltpu.roll(carry8, s, 0)
    row8 = lax.broadcasted_iota(jnp.int32, (SUBLANE, x.shape[1]), 0)
    top = jnp.where(row8 < s, rolled_c, rolled[0:SUBLANE])
    if n == SUBLANE:
        return top
    return jnp.concatenate([top, rolled[SUBLANE:]], axis=0)


def _mm_kernel(*refs, has_bias, has_x2, pre_act, act):
    refs = list(refs)
    x_ref, w_ref = refs[:2]
    x2_ref = refs[2] if has_x2 else None
    b_ref = refs[2 + has_x2] if has_bias else None
    o_ref = refs[2 + has_x2 + has_bias]
    o2_ref = refs[3 + has_x2 + has_bias] if has_x2 else None
    wb_ref = refs[-1]

    @pl.when(pl.program_id(1) == 0)
    def _():
        wb_ref[...] = w_ref[...].astype(BF16)

    def project(src, dst):
        x = src[...]
        if pre_act is not None:
            x = _act(x.astype(F32), pre_act)
        acc = _dot(x.astype(BF16), wb_ref[...])
        if has_bias:
            acc = acc + b_ref[...]
        dst[...] = _act(acc, act).astype(dst.dtype)

    project(x_ref, o_ref)
    if has_x2:
        @pl.when(pl.program_id(1) == pl.num_programs(1) - 1)
        def _():
            project(x2_ref, o2_ref)


def _wspec(w, layer, rows, cols, col_of):
    assert w.ndim == 3
    return pl.BlockSpec((None, rows, cols), lambda *ids: (layer, 0, col_of(*ids)))


def _mm(x, w, layer, bias=None, *, x2=None, pre_act=None, act=None, out_dtype=F32, tm_target=1024,
        tn_target=1024, name="mm"):
    m, k = x.shape
    n = w.shape[2]
    tm = _pick(m, tm_target, 16)
    tn = _pick(n, tn_target, LANE)
    in_specs = [pl.BlockSpec((tm, k), lambda j, i: (i, 0)),
                _wspec(w, layer, k, tn, lambda j, i: j)]
    args = [x, w]
    out_shape = [jax.ShapeDtypeStruct((m, n), out_dtype)]
    out_specs = [pl.BlockSpec((tm, tn), lambda j, i: (i, j))]
    if x2 is not None:
        m2 = x2.shape[0]
        in_specs.append(pl.BlockSpec((m2, k), lambda j, i: (0, 0)))
        args.append(x2)
        out_shape.append(jax.ShapeDtypeStruct((m2, n), out_dtype))
        out_specs.append(pl.BlockSpec((m2, tn), lambda j, i: (0, j)))
    if bias is not None:
        in_specs.append(pl.BlockSpec((1, tn), lambda j, i: (0, j)))
        args.append(bias.reshape(1, n))
    res = pl.pallas_call(
        functools.partial(_mm_kernel, has_bias=bias is not None, has_x2=x2 is not None, pre_act=pre_act, act=act),
        out_shape=out_shape,
        grid=(n // tn, m // tm),
        in_specs=in_specs,
        out_specs=out_specs,
        scratch_shapes=[pltpu.VMEM((k, tn), BF16)],
        compiler_params=_cp(2),
        name=name,
    )(*args)
    return res if x2 is not None else res[0]


def _mod_spec(mod, slot, d, tr):
    rm = mod.shape[1]
    if rm == 1:
        return pl.BlockSpec((1, 1, d), lambda g, t: (g, 0, slot))
    return pl.BlockSpec((1, tr, d), lambda g, t: (g, t, slot))


def _resid_norm_kernel(*refs, with_next):
    if with_next:
        x_ref, o_ref, g1_ref, gate_ref, g2_ref, sc_ref, sh_ref, xn_ref, h_ref = refs
    else:
        x_ref, o_ref, g1_ref, gate_ref, xn_ref = refs
    xn = x_ref[0] + gate_ref[0] * _rms(o_ref[0].astype(F32), g1_ref[...])
    xn_ref[0] = xn
    if with_next:
        h_ref[0] = (_rms(xn, g2_ref[...]) * (1.0 + sc_ref[0]) + sh_ref[0]).astype(h_ref.dtype)


def _resid_norm(x, o, g1, mod, gate_slot, nxt=None, name="resid_norm"):
    g, r, d = x.shape
    tr = _pick(r, 256, 16)
    row = pl.BlockSpec((1, tr, d), lambda gi, t: (gi, t, 0))
    vec = pl.BlockSpec((1, d), lambda gi, t: (0, 0))
    in_specs = [row, row, vec, _mod_spec(mod, gate_slot, d, tr)]
    args = [x, o, g1.reshape(1, d), mod]
    out_shape = [jax.ShapeDtypeStruct((g, r, d), F32)]
    out_specs = [row]
    if nxt is not None:
        g2, mod2, sc_slot, sh_slot = nxt
        in_specs += [vec, _mod_spec(mod2, sc_slot, d, tr), _mod_spec(mod2, sh_slot, d, tr)]
        args += [g2.reshape(1, d), mod2, mod2]
        out_shape.append(jax.ShapeDtypeStruct((g, r, d), BF16))
        out_specs.append(row)
    res = pl.pallas_call(
        functools.partial(_resid_norm_kernel, with_next=nxt is not None),
        out_shape=out_shape,
        grid=(g, r // tr),
        in_specs=in_specs,
        out_specs=out_specs,
        compiler_params=_cp(2),
        name=name,
    )(*args)
    return res if nxt is not None else res[0]


def _mm_resid_kernel(a_ref, w_ref, x_ref, g1_ref, gate_ref, g2_ref, sc_ref, sh_ref, xn_ref, h_ref):
    tm = a_ref.shape[1]
    ts = _pick(tm, MM_RESID_SUB_ROWS, 16)

    def rows_of(ref, rows):
        return ref[0] if ref.shape[1] == 1 else ref[0, rows, :]

    for s0 in range(0, tm, ts):
        rows = slice(s0, s0 + ts)
        o = _dot(a_ref[0, rows, :], w_ref[...])
        xn = x_ref[0, rows, :] + rows_of(gate_ref, rows) * _rms(o, g1_ref[...])
        xn_ref[0, rows, :] = xn
        h_ref[0, rows, :] = (_rms(xn, g2_ref[...]) * (1.0 + rows_of(sc_ref, rows))
                             + rows_of(sh_ref, rows)).astype(BF16)


def _mm_resid(a, w, layer, x, g1, mod, gate_slot, g2, sc_slot, sh_slot, name):
    g, r, k = a.shape
    n = w.shape[2]
    wb = w[layer].astype(BF16)
    tm = _pick(r, 256, 16)
    row = lambda c: pl.BlockSpec((1, tm, c), lambda gi, t: (gi, t, 0))
    vec = pl.BlockSpec((1, n), lambda gi, t: (0, 0))
    return pl.pallas_call(
        _mm_resid_kernel,
        out_shape=[jax.ShapeDtypeStruct((g, r, n), F32), jax.ShapeDtypeStruct((g, r, n), BF16)],
        grid=(g, r // tm),
        in_specs=[row(k), pl.BlockSpec((k, n), lambda gi, t: (0, 0)), row(n), vec,
                  _mod_spec(mod, gate_slot, n, tm), vec, _mod_spec(mod, sc_slot, n, tm),
                  _mod_spec(mod, sh_slot, n, tm)],
        out_specs=[row(n), row(n)],
        compiler_params=_cp(2),
        name=name,
    )(a, wb, x, g1.reshape(1, n), mod, g2.reshape(1, n), mod, mod)


def _rwkv_prep_kernel(x_ref, g_ref, sc_ref, sh_ref, mix_ref, shift_ref,
                      w1_ref, w2_ref, w0_ref, a1_ref, a2_ref, a0_ref, g1_ref, g2_ref,
                      xr_ref, xk_ref, xv_ref, wpre_ref, a_ref, gate_ref, shift_out, *scratch, seq):
    h = _rms(x_ref[0], g_ref[...]) * (1.0 + sc_ref[0]) + sh_ref[0]
    tr = h.shape[0]
    if seq:
        carry = scratch[0]
        t = pl.program_id(1)

        @pl.when(t == 0)
        def _():
            carry[...] = jnp.broadcast_to(shift_ref[0], carry.shape)

        hprev = _shift_rows(h, carry[...], 1)
        carry[...] = h[tr - SUBLANE:tr]

        @pl.when(t == pl.num_programs(1) - 1)
        def _():
            shift_out[0] = h[tr - 1:tr]
    else:
        hprev = shift_ref[0]
        shift_out[0] = h
    xx = hprev - h

    def mixed(j):
        return (h + xx * mix_ref[j:j + 1, :]).astype(BF16)

    xr_ref[0] = mixed(0)
    xk_ref[0] = mixed(2)
    xv_ref[0] = mixed(3)
    wpre_ref[0] = w0_ref[...] + _dot(jnp.tanh(_dot(mixed(1), w1_ref[...])).astype(BF16), w2_ref[...])
    a = _sigmoid(a0_ref[...] + _dot(_dot(mixed(4), a1_ref[...]).astype(BF16), a2_ref[...]))
    a_ref[0] = a.astype(a_ref.dtype)
    gate_ref[0] = _dot(_sigmoid(_dot(mixed(5), g1_ref[...])).astype(BF16), g2_ref[...]).astype(gate_ref.dtype)


def _lora_pair(w1, w2):
    r = w1.shape[1]
    rp = -(-r // LANE) * LANE
    if rp != r:
        w1 = jnp.pad(w1, ((0, 0), (0, rp - r)))
        w2 = jnp.pad(w2, ((0, rp - r), (0, 0)))
    return w1.astype(BF16), w2.astype(BF16)


def _rwkv_prep(x, g, mod, mix, shift, seq, lora):
    gg, r, d = x.shape
    tr = _pick(r, 256, 16)
    w1, w2, w0, a1, a2, a0, g1, g2 = lora
    w1, w2 = _lora_pair(w1, w2)
    a1, a2 = _lora_pair(a1, a2)
    g1, g2 = _lora_pair(g1, g2)
    row = pl.BlockSpec((1, tr, d), lambda gi, t: (gi, t, 0))
    vec = pl.BlockSpec((1, d), lambda gi, t: (0, 0))
    full = lambda w: pl.BlockSpec(w.shape, lambda gi, t: (0, 0))
    if seq:
        shift_spec = pl.BlockSpec((1, 1, d), lambda gi, t: (gi, 0, 0))
        shift_shape = (gg, 1, d)
        scratch = [pltpu.VMEM((SUBLANE, d), F32)]
    else:
        shift_spec = row
        shift_shape = (gg, r, d)
        scratch = []
    act = lambda dt: jax.ShapeDtypeStruct((gg, r, d), dt)
    outs = pl.pallas_call(
        functools.partial(_rwkv_prep_kernel, seq=seq),
        out_shape=[act(BF16)] * 3 + [act(F32), act(BF16), act(BF16)] + [jax.ShapeDtypeStruct(shift_shape, F32)],
        grid=(gg, r // tr),
        in_specs=[row, vec, _mod_spec(mod, 1, d, tr), _mod_spec(mod, 0, d, tr),
                  pl.BlockSpec((6, d), lambda gi, t: (0, 0)), shift_spec,
                  full(w1), full(w2), vec, full(a1), full(a2), vec, full(g1), full(g2)],
        out_specs=[row] * 6 + [shift_spec],
        scratch_shapes=scratch,
        compiler_params=_cp(2),
        name="rwkv_prep",
    )(x, g.reshape(1, d), mod, mod, mix, shift, w1, w2, w0.reshape(1, d), a1, a2, a0.reshape(1, d), g1, g2)
    return outs[:6], outs[6]


def _wkv_terms(k, wpre, a, kkp, kap):
    logd = -0.6065306597126334 * _sigmoid(wpre)
    kk = k * kkp
    kk = kk * lax.rsqrt(jnp.maximum(_head_sum(kk * kk), 1e-24))
    kmod = k * (1.0 + (a - 1.0) * kap)
    return logd, kk, kmod


def _wkv_readout(y, bonus, gate, lnx_g, lnx_b):
    inv_n = 1.0 / HEAD
    yc = y - _head_sum(y) * inv_n
    var = _head_sum(yc * yc) * inv_n
    return ((yc * lax.rsqrt(var + LNX_EPS) * lnx_g + lnx_b + bonus) * gate).astype(BF16)


def _wkv_chunk_kernel(r_ref, k_ref, v_ref, w_ref, a_ref, gate_ref, kkp_ref, kap_ref, rk_ref, lg_ref, lb_ref,
                      xo_ref, sout_ref, S, Rt, At, Bt, Kt, Vs, Gc, Ys, Bonus):
    c = pl.program_id(1)
    n_pairs = S.shape[0]
    C = r_ref.shape[1]

    @pl.when(c == 0)
    def _():
        S[...] = jnp.zeros(S.shape, F32)

    a = a_ref[0].astype(F32)
    r = r_ref[0].astype(F32)
    logd, kk, kmod = _wkv_terms(k_ref[0].astype(F32), w_ref[0], a, kkp_ref[...], kap_ref[...])
    ri = lax.broadcasted_iota(jnp.int32, (C, C), 0)
    ci = lax.broadcasted_iota(jnp.int32, (C, C), 1)
    tri = (ri >= ci).astype(BF16)
    cum = _split_dot_left(tri, logd, 3)
    gam = jnp.exp(cum)
    rt = r * gam
    at = -kk * jnp.exp(cum - logd)
    ginv = jnp.exp(-cum)
    bt = kk * a * ginv
    kt = kmod * ginv
    v = v_ref[0].astype(F32)
    Bonus[...] = _head_sum(r * kmod * rk_ref[...]) * v
    for p in range(n_pairs):
        sl = slice(p * LANE, (p + 1) * LANE)
        Rt[p] = rt[:, sl]
        At[p] = at[:, sl]
        Bt[p] = bt[:, sl]
        Kt[p] = kt[:, sl]
        Vs[p] = v[:, sl]
        Gc[p] = jnp.broadcast_to(gam[C - 1:C, sl], (SUBLANE, LANE))

    lane = lax.broadcasted_iota(jnp.int32, (1, LANE), 1)
    first = lane < HEAD
    r2 = lax.broadcasted_iota(jnp.int32, (2 * C, 2 * C), 0)
    c2 = lax.broadcasted_iota(jnp.int32, (2 * C, 2 * C), 1)
    same_blk = (r2 < C) == (c2 < C)
    strict = jnp.logical_and(r2 > c2, same_blk)
    incl = jnp.logical_and(r2 >= c2, same_blk)
    rs = lax.broadcasted_iota(jnp.int32, (LANE, LANE), 0)
    cs = lax.broadcasted_iota(jnp.int32, (LANE, LANE), 1)
    same_head = (rs < HEAD) == (cs < HEAD)
    zero_blk = jnp.zeros((2 * C, LANE), BF16)

    def stack(x):
        return jnp.concatenate([jnp.where(first, x, 0.0), jnp.where(first, 0.0, x)], axis=0).astype(BF16)

    def twice(x):
        xb = x.astype(BF16)
        return jnp.concatenate([xb, xb], axis=0)

    n_fac = max(1, (C - 1).bit_length())

    def pairs_step(ps):
        n = range(len(ps))
        rp, ap, bp = [Rt[p] for p in ps], [At[p] for p in ps], [Bt[p] for p in ps]
        kp, vp = [Kt[p] for p in ps], [Vs[p] for p in ps]
        sp, gc = [S[p] for p in ps], [Gc[p] for p in ps]
        am, rm = [stack(t) for t in ap], [stack(t) for t in rp]
        bm, km = [twice(t) for t in bp], [twice(t) for t in kp]
        ar = [jnp.concatenate([am[i], rm[i]], axis=0) for i in n]
        gb = [_dot_nt(ar[i], bm[i]) for i in n]
        gk = [_dot_nt(ar[i], km[i]) for i in n]
        lp = [jnp.where(strict, t[:2 * C], 0.0) for t in gb]
        lak = [jnp.where(strict, t[:2 * C], 0.0).astype(BF16) for t in gk]
        grbk = [jnp.concatenate([jnp.where(incl, gb[i][2 * C:], 0.0), jnp.where(incl, gk[i][2 * C:], 0.0)],
                                axis=1).astype(BF16) for i in n]
        spb = [t.astype(BF16) for t in sp]
        ars = [_dot_nt(jnp.concatenate([ap[i], rp[i]], axis=0).astype(BF16), spb[i]) for i in n]
        a_s, r_s = [t[:C] for t in ars], [t[C:] for t in ars]
        vst = [jnp.concatenate([t, t], axis=0).astype(BF16) for t in vp]
        x = [jnp.concatenate([a_s[i], a_s[i]], axis=0) + _dot(lak[i], vst[i]) for i in n]
        for f in range(n_fac):
            lh = [t.astype(BF16) for t in lp]
            xh = [t.astype(BF16) for t in x]
            xl = [(x[i] - xh[i].astype(F32)).astype(BF16) for i in n]
            lhs = [jnp.concatenate([t, t], axis=1) for t in lh]
            if f + 1 < n_fac:
                rhs = [jnp.concatenate([jnp.concatenate([xh[i], lh[i]], axis=1),
                                        jnp.concatenate([xl[i], zero_blk], axis=1)], axis=0) for i in n]
                prod = [_dot(lhs[i], rhs[i]) for i in n]
                x = [x[i] + prod[i][:, :LANE] for i in n]
                lp = [t[:, LANE:] for t in prod]
            else:
                x = [x[i] + _dot(lhs[i], jnp.concatenate([xh[i], xl[i]], axis=0)) for i in n]
        yst = [jnp.concatenate([r_s[i], r_s[i]], axis=0)
               + _dot(grbk[i], jnp.concatenate([x[i].astype(BF16), vst[i]], axis=0)) for i in n]
        u = [jnp.where(first, t[:C], t[C:]) for t in x]
        ds = [_dot_tn(jnp.concatenate([u[i], vp[i]], axis=0).astype(BF16),
                      jnp.concatenate([bp[i], kp[i]], axis=0).astype(BF16)) for i in n]
        for i, p in enumerate(ps):
            Ys[p] = jnp.where(first, yst[i][:C], yst[i][C:])
            S[p] = jnp.where(same_head, (sp[i] + ds[i]) * gc[i][0:1, :], 0.0)

    for p0 in range(0, n_pairs, PAIR_UNROLL):
        pairs_step(list(range(p0, min(p0 + PAIR_UNROLL, n_pairs))))

    y = jnp.concatenate([Ys[p] for p in range(n_pairs)], axis=-1) if n_pairs > 1 else Ys[0]
    xo_ref[0] = _wkv_readout(y, Bonus[...], gate_ref[0].astype(F32), lg_ref[...], lb_ref[...])

    @pl.when(c == pl.num_programs(1) - 1)
    def _():
        sout_ref[0] = S[...]


def _wkv_chunked(r, k, v, wpre, a, gate, kkp, kap, rk, lnx_g, lnx_b):
    g, t, d = r.shape
    C = CHUNK
    np_ = d // LANE
    row = pl.BlockSpec((1, C, d), lambda gi, ci: (gi, ci, 0))
    vec = pl.BlockSpec((1, d), lambda gi, ci: (0, 0))
    pair_buf = pltpu.VMEM((np_, C, LANE), F32)
    return pl.pallas_call(
        _wkv_chunk_kernel,
        out_shape=[jax.ShapeDtypeStruct((g, t, d), BF16),
                   jax.ShapeDtypeStruct((g, np_, LANE, LANE), F32)],
        grid=(g, t // C),
        in_specs=[row] * 6 + [vec] * 5,
        out_specs=[row, pl.BlockSpec((1, np_, LANE, LANE), lambda gi, ci: (gi, 0, 0, 0))],
        scratch_shapes=[pltpu.VMEM((np_, LANE, LANE), F32), pair_buf, pair_buf, pair_buf, pair_buf, pair_buf,
                        pltpu.VMEM((np_, SUBLANE, LANE), F32), pair_buf, pltpu.VMEM((C, d), F32)],
        compiler_params=_cp(2),
        name="wkv_chunk",
    )(r, k, v, wpre, a, gate, *[p.reshape(1, d) for p in (kkp, kap, rk, lnx_g, lnx_b)])


def _wkv_vec_kernel(k_ref, w_ref, a_ref, r_ref, v_ref, kkp_ref, kap_ref,
                    d_ref, an_ref, b_ref, km_ref, rt_ref, vt_ref):
    a = a_ref[...].astype(F32)
    logd, kk, kmod = _wkv_terms(k_ref[...].astype(F32), w_ref[...], a, kkp_ref[...], kap_ref[...])
    d_ref[...] = jnp.exp(logd).T
    an_ref[...] = (-kk).T
    b_ref[...] = (kk * a).T
    km_ref[...] = kmod.T
    rt_ref[...] = r_ref[...].astype(F32).T
    vt_ref[...] = v_ref[...].astype(F32).T


def _wkv_vec(k, wpre, a, r, v, kkp, kap):
    b, d = k.shape
    full = pl.BlockSpec((b, d), lambda i: (0, 0))
    fullt = pl.BlockSpec((d, b), lambda i: (0, 0))
    vec = pl.BlockSpec((1, d), lambda i: (0, 0))
    return pl.pallas_call(
        _wkv_vec_kernel,
        out_shape=[jax.ShapeDtypeStruct((d, b), F32)] * 6,
        grid=(1,),
        in_specs=[full] * 5 + [vec, vec],
        out_specs=[fullt] * 6,
        compiler_params=_cp(1),
        name="wkv_vec",
    )(k, wpre, a, r, v, kkp.reshape(1, d), kap.reshape(1, d))


def _wkv_step_kernel(s_ref, d_ref, a_ref, b_ref, k_ref, r_ref, v_ref, sn_ref, y_ref):
    s = s_ref[...]
    key = lambda ref: ref[...][:, None]
    sa = jnp.sum(s * key(a_ref), axis=2, keepdims=True)
    sn = s * key(d_ref) + sa * key(b_ref) + v_ref[...] * key(k_ref)
    sn_ref[...] = sn
    y_ref[...] = jnp.sum(sn * key(r_ref), axis=2, keepdims=True)


def _wkv_step(s, dvec, an, bvec, kmod, r, v):
    h, n, _, b = s.shape
    hb = _pick(h, 2, 1)
    sblk = pl.BlockSpec((hb, n, n, b), lambda i: (i, 0, 0, 0))
    kblk = pl.BlockSpec((hb, n, b), lambda i: (i, 0, 0))
    vblk = pl.BlockSpec((hb, n, 1, b), lambda i: (i, 0, 0, 0))
    kv = lambda x: x.reshape(h, n, b)
    sn, y = pl.pallas_call(
        _wkv_step_kernel,
        out_shape=[jax.ShapeDtypeStruct((h, n, n, b), F32), jax.ShapeDtypeStruct((h, n, 1, b), F32)],
        grid=(h // hb,),
        in_specs=[sblk] + [kblk] * 5 + [vblk],
        out_specs=[sblk, vblk],
        compiler_params=_cp(1),
        name="wkv_step",
    )(s, kv(dvec), kv(an), kv(bvec), kv(kmod), kv(r), v.reshape(h, n, 1, b))
    return sn, y.reshape(h * n, b)


def _wkv_post_kernel(y_ref, r_ref, k_ref, v_ref, a_ref, g_ref, kap_ref, rk_ref, lg_ref, lb_ref, o_ref):
    f32 = lambda ref: ref[0].astype(F32)
    kmod = f32(k_ref) * (1.0 + (f32(a_ref) - 1.0) * kap_ref[...])
    bonus = _head_sum(f32(r_ref) * kmod * rk_ref[...]) * f32(v_ref)
    o_ref[0] = _wkv_readout(y_ref[0], bonus, f32(g_ref), lg_ref[...], lb_ref[...])


def _wkv_post(y, r, k, v, a, g, kap, rk, lnx_g, lnx_b):
    gg, t, d = y.shape
    tr = _pick(t, 256, 16)
    row = pl.BlockSpec((1, tr, d), lambda gi, ti: (gi, ti, 0))
    vec = pl.BlockSpec((1, d), lambda gi, ti: (0, 0))
    return pl.pallas_call(
        _wkv_post_kernel,
        out_shape=jax.ShapeDtypeStruct((gg, t, d), BF16),
        grid=(gg, t // tr),
        in_specs=[row] * 6 + [vec] * 4,
        out_specs=row,
        compiler_params=_cp(2),
        name="wkv_post",
    )(y, r, k, v, a, g, kap.reshape(1, d), rk.reshape(1, d), lnx_g.reshape(1, d), lnx_b.reshape(1, d))


def _lru_conv_kernel(x_ref, w_ref, b_ref, buf_ref, xc_ref, *, width):
    acc = b_ref[...] + x_ref[0].astype(F32) * w_ref[width - 1:width, :]
    for s in range(1, width):
        acc = acc + buf_ref[0, :, width - 1 - s, :] * w_ref[width - 1 - s:width - s, :]
    xc_ref[0] = acc


def _lru_conv(proj, w, b, buf):
    g, r, d2 = proj.shape
    d = d2 // 2
    width = w.shape[0]
    tr = _pick(r, 256, 16)
    row = pl.BlockSpec((1, tr, d), lambda gi, t: (gi, t, 0))
    return pl.pallas_call(
        functools.partial(_lru_conv_kernel, width=width),
        out_shape=jax.ShapeDtypeStruct((g, r, d), F32),
        grid=(g, r // tr),
        in_specs=[pl.BlockSpec((1, tr, d), lambda gi, t: (gi, t, 1)),
                  pl.BlockSpec((width, d), lambda gi, t: (0, 0)),
                  pl.BlockSpec((1, d), lambda gi, t: (0, 0)),
                  pl.BlockSpec((1, tr, width - 1, d), lambda gi, t: (gi, t, 0, 0))],
        out_specs=row,
        compiler_params=_cp(2),
        name="lru_conv",
    )(proj, w, b.reshape(1, d), buf)


def _lru_gates_kernel(x_ref, wa_ref, ba_ref, wx_ref, bx_ref, ga_ref, gx_ref):
    x = x_ref[...].astype(BF16)
    ga_ref[...] = _sigmoid(_dot(x, wa_ref[0].astype(BF16)) + ba_ref[...])
    gx_ref[...] = _sigmoid(_dot(x, wx_ref[0].astype(BF16)) + bx_ref[...])


def _lru_gates(xc, wa, ba, wx, bx):
    m, d = xc.shape
    nb, bw, _ = wa.shape
    tm = _pick(m, 1024, 8)
    xs = pl.BlockSpec((tm, bw), lambda n, i: (i, n))
    ws = pl.BlockSpec((1, bw, bw), lambda n, i: (n, 0, 0))
    bs = pl.BlockSpec((1, bw), lambda n, i: (0, n))
    return pl.pallas_call(
        _lru_gates_kernel,
        out_shape=[jax.ShapeDtypeStruct((m, d), F32)] * 2,
        grid=(nb, m // tm),
        in_specs=[xs, ws, bs, ws, bs],
        out_specs=[xs, xs],
        compiler_params=_cp(2),
        name="lru_gates",
    )(xc, wa, ba.reshape(1, d), wx, bx.reshape(1, d))


def _lru_coeffs(ga, gx, xc, lam):
    log_a = -LRU_C * ga * _softplus(-lam)
    a = jnp.exp(log_a)
    mult = jnp.sqrt(1.0 - a * a)
    return a, mult


def _lru_seq_kernel(xin_ref, y_ref, cw_ref, cb_ref, wa_ref, ba_ref, wx_ref, bx_ref, lam_ref, h0_ref, buf_ref,
                    o_ref, hl_ref, nbuf_ref, cconv, carry, abuf, bbuf, hbuf, *, width):
    t = pl.program_id(2)
    tr = xin_ref.shape[1]

    @pl.when(t == 0)
    def _():
        cconv[...] = buf_ref[0]
        carry[...] = jnp.broadcast_to(h0_ref[0], carry.shape)

    x = xin_ref[0].astype(F32)
    c8 = cconv[...]
    xc = cb_ref[...] + x * cw_ref[width - 1:width, :]
    for s in range(1, width):
        xc = xc + _shift_rows(x, c8, s) * cw_ref[width - 1 - s:width - s, :]
    cconv[...] = x[tr - SUBLANE:tr]
    nbuf_ref[0] = x[tr - SUBLANE:tr]
    xb = xc.astype(BF16)
    ga = _sigmoid(_dot(xb, wa_ref[...].astype(BF16)) + ba_ref[...])
    gx = _sigmoid(_dot(xb, wx_ref[...].astype(BF16)) + bx_ref[...])
    a, mult = _lru_coeffs(ga, gx, xc, lam_ref[...])
    row = lax.broadcasted_iota(jnp.int32, a.shape, 0)
    mult = jnp.where(jnp.logical_and(t == 0, row == 0), 1.0, mult)
    abuf[...] = a
    bbuf[...] = mult * gx * xc
    r8 = lax.broadcasted_iota(jnp.int32, (SUBLANE, a.shape[1]), 0)

    def body(i, c):
        off = pl.multiple_of(i * SUBLANE, SUBLANE)
        aa = abuf[pl.ds(off, SUBLANE), :]
        bb = bbuf[pl.ds(off, SUBLANE), :]
        for s in (1, 2, 4):
            a_s = jnp.where(r8 >= s, pltpu.roll(aa, s, 0), 1.0)
            b_s = jnp.where(r8 >= s, pltpu.roll(bb, s, 0), 0.0)
            bb = aa * b_s + bb
            aa = aa * a_s
        h = aa * c + bb
        hbuf[pl.ds(off, SUBLANE), :] = h
        return h[SUBLANE - 1:SUBLANE, :]

    c = lax.fori_loop(0, tr // SUBLANE, body, carry[0:1, :], unroll=LRU_SCAN_UNROLL)
    carry[...] = jnp.broadcast_to(c, carry.shape)
    o_ref[0] = (hbuf[...] * _gelu(y_ref[0].astype(F32))).astype(BF16)

    @pl.when(t == pl.num_programs(2) - 1)
    def _():
        hl_ref[0] = c


def _lru_seq(proj, cw, cb, wa, ba, wx, bx, lam, h0, buf):
    g, r, d2 = proj.shape
    d = d2 // 2
    nb, bw, _ = wa.shape
    width = cw.shape[0]
    tr = _pick(r, 512, 16)
    row = lambda off: pl.BlockSpec((1, tr, bw), lambda gi, n, t: (gi, t, n + off))
    vec = pl.BlockSpec((1, bw), lambda gi, n, t: (0, n))
    wsp = pl.BlockSpec((None, bw, bw), lambda gi, n, t: (n, 0, 0))
    st = pl.BlockSpec((1, 1, bw), lambda gi, n, t: (gi, 0, n))
    st8 = pl.BlockSpec((1, SUBLANE, bw), lambda gi, n, t: (gi, 0, n))
    tile = pltpu.VMEM((tr, bw), F32)
    small = pltpu.VMEM((SUBLANE, bw), F32)
    return pl.pallas_call(
        functools.partial(_lru_seq_kernel, width=width),
        out_shape=[jax.ShapeDtypeStruct((g, r, d), BF16), jax.ShapeDtypeStruct((g, 1, d), F32),
                   jax.ShapeDtypeStruct((g, SUBLANE, d), F32)],
        grid=(g, nb, r // tr),
        in_specs=[row(nb), row(0), pl.BlockSpec((width, bw), lambda gi, n, t: (0, n)), vec,
                  wsp, vec, wsp, vec, vec, st, st8],
        out_specs=[row(0), st, st8],
        scratch_shapes=[small, small, tile, tile, tile],
        compiler_params=_cp(3),
        name="lru_seq",
    )(proj, proj, cw, cb.reshape(1, d), wa, ba.reshape(1, d), wx, bx.reshape(1, d), lam.reshape(1, d), h0, buf)


def _lru_step_kernel(ga_ref, gx_ref, xc_ref, y_ref, lam_ref, h0_ref, o_ref, h_ref):
    xc = xc_ref[0]
    a, mult = _lru_coeffs(ga_ref[0], gx_ref[0], xc, lam_ref[...])
    h = a * h0_ref[0] + mult * gx_ref[0] * xc
    h_ref[0] = h
    o_ref[0] = (h * _gelu(y_ref[0].astype(F32))).astype(BF16)


def _lru_step(ga, gx, xc, proj, lam, h0):
    g, r, d = ga.shape
    row = pl.BlockSpec((1, r, d), lambda i: (0, 0, 0))
    return pl.pallas_call(
        _lru_step_kernel,
        out_shape=[jax.ShapeDtypeStruct((g, r, d), BF16), jax.ShapeDtypeStruct((g, r, d), F32)],
        grid=(1,),
        in_specs=[row, row, row, row, pl.BlockSpec((1, d), lambda i: (0, 0)), row],
        out_specs=[row, row],
        compiler_params=_cp(1),
        name="lru_step",
    )(ga, gx, xc, proj, lam.reshape(1, d), h0)


def _ffn_up_kernel(x_ref, wg_ref, wu_ref, cw_ref, cb_ref, buf_ref, x2_ref, hist2_ref,
                   o_ref, nb_ref, o2_ref, nhist2_ref, wgb, wub, carry, *, tiles_per_seq):
    i = pl.program_id(1)

    @pl.when(i == 0)
    def _():
        wgb[...] = wg_ref[...].astype(BF16)
        wub[...] = wu_ref[...].astype(BF16)

    def gated(gt, p1, p2, u):
        gc = cb_ref[...] + p2 * cw_ref[0:1, :] + p1 * cw_ref[1:2, :] + gt * cw_ref[2:3, :]
        return (_gelu(gc) * u).astype(BF16)

    tm = x_ref.shape[0]
    ts = _pick(tm, FFN_SUB_ROWS, 16)

    @pl.when(i % tiles_per_seq == 0)
    def _():
        carry[...] = buf_ref[0]

    c8 = carry[...]
    for s0 in range(0, tm, ts):
        rows = slice(s0, s0 + ts)
        x = x_ref[rows, :]
        gt = _dot(x, wgb[...])
        u = _dot(x, wub[...])
        o_ref[rows, :] = gated(gt, _shift_rows(gt, c8, 1), _shift_rows(gt, c8, 2), u)
        c8 = gt[ts - SUBLANE:ts]
    carry[...] = c8
    nb_ref[0] = c8

    @pl.when(i == pl.num_programs(1) - 1)
    def _():
        x2 = x2_ref[...]
        gt = _dot(x2, wgb[...])
        p1 = hist2_ref[:, 1, :]
        o2_ref[...] = gated(gt, p1, hist2_ref[:, 0, :], _dot(x2, wub[...]))
        nhist2_ref[:, 0, :] = p1
        nhist2_ref[:, 1, :] = gt


def _ffn_up(x, wg, wu, layer, cw, cb, buf, rows_per_seq, x2, hist2):
    m, k = x.shape
    m2 = x2.shape[0]
    f = wg.shape[2]
    tn = _pick(f, 512, LANE)
    tm = _pick(rows_per_seq, 1024, 16)
    tps = rows_per_seq // tm
    g = m // rows_per_seq
    ws = _wspec(wg, layer, k, tn, lambda j, i: j)
    bspec = pl.BlockSpec((1, SUBLANE, tn), lambda j, i: (i // tps, 0, j))
    hspec = pl.BlockSpec((m2, 2, tn), lambda j, i: (0, 0, j))
    o2spec = pl.BlockSpec((m2, tn), lambda j, i: (0, j))
    return pl.pallas_call(
        functools.partial(_ffn_up_kernel, tiles_per_seq=tps),
        out_shape=[jax.ShapeDtypeStruct((m, f), BF16), jax.ShapeDtypeStruct((g, SUBLANE, f), F32),
                   jax.ShapeDtypeStruct((m2, f), BF16), jax.ShapeDtypeStruct((m2, 2, f), F32)],
        grid=(f // tn, m // tm),
        in_specs=[pl.BlockSpec((tm, k), lambda j, i: (i, 0)), ws, ws,
                  pl.BlockSpec((3, tn), lambda j, i: (0, j)), pl.BlockSpec((1, tn), lambda j, i: (0, j)),
                  bspec, pl.BlockSpec((m2, k), lambda j, i: (0, 0)), hspec],
        out_specs=[pl.BlockSpec((tm, tn), lambda j, i: (i, j)), bspec, o2spec, hspec],
        scratch_shapes=[pltpu.VMEM((k, tn), BF16), pltpu.VMEM((k, tn), BF16), pltpu.VMEM((SUBLANE, tn), F32)],
        compiler_params=_cp(2),
        name="ffn_up",
    )(x, wg, wu, cw, cb.reshape(1, f), buf, x2, hist2)


def _trunk(xp, xs, mods_p, mods_s, st, P):
    gp, rp, d = xp.shape
    rs = xs.shape[1]
    depth = P["w_mod"].shape[0]
    keys = ("wkv", "shift", "h", "lconv", "fconv")
    out_p = {k: [] for k in keys}
    out_s = {k: [] for k in keys}
    flat = lambda t: t.reshape(-1, t.shape[-1])
    unp = lambda t: t.reshape(gp, rp, t.shape[-1])
    uns = lambda t: t.reshape(1, rs, t.shape[-1])
    hp = hs = None
    for i in range(depth):
        mp, ms = mods_p[i], mods_s[i]
        j = i // 2
        if i % 2 == 0:
            lora = tuple(P[nm][j] for nm in ("rw_w1", "rw_w2", "rw_w0", "rw_a1", "rw_a2", "rw_a0", "rw_g1", "rw_g2"))
            g0, mix = P["norm_g"][i, 0], P["rw_mix"][j]
            pre_p, shift_p = _rwkv_prep(xp, g0, mp, mix, jnp.zeros((gp, 1, d), F32), True, lora)
            pre_s, shift_s = _rwkv_prep(xs, g0, ms, mix, st["shift"][j].reshape(1, rs, d), False, lora)
            xr_p, xk_p, xv_p, w_p, a_p, g_p = pre_p
            xr_s, xk_s, xv_s, w_s, a_s, g_s = pre_s
            rr_p, rr_s = _mm(flat(xr_p), P["rw_wr"], j, x2=flat(xr_s), out_dtype=ACT_DTYPE, name="rw_r")
            kk_p, kk_s = _mm(flat(xk_p), P["rw_wk"], j, x2=flat(xk_s), out_dtype=ACT_DTYPE, name="rw_k")
            vv_p, vv_s = _mm(flat(xv_p), P["rw_wv"], j, x2=flat(xv_s), out_dtype=ACT_DTYPE, name="rw_v")
            heads = (P["rw_kk"][j], P["rw_ka"][j], P["rw_rk"][j], P["rw_lnx_g"][j], P["rw_lnx_b"][j])
            xo_p, s_bd = _wkv_chunked(unp(rr_p), unp(kk_p), unp(vv_p), w_p, a_p, g_p, *heads)
            npairs = d // LANE
            s_new = jnp.stack([s_bd[:, :, :HEAD, :HEAD], s_bd[:, :, HEAD:, HEAD:]], axis=2)
            out_p["wkv"].append(s_new.reshape(gp, 2 * npairs, HEAD, HEAD))
            out_p["shift"].append(shift_p.reshape(gp, d))
            s_t = jnp.transpose(st["wkv"][j], (1, 2, 3, 0))
            vecs = _wkv_vec(kk_s, flat(w_s), flat(a_s), rr_s, vv_s, heads[0], heads[1])
            s_t, y_s = _wkv_step(s_t, *vecs)
            out_s["wkv"].append(jnp.transpose(s_t, (3, 0, 1, 2)))
            out_s["shift"].append(shift_s.reshape(rs, d))
            xo_s = _wkv_post(y_s.T.reshape(1, rs, d), uns(rr_s), uns(kk_s), uns(vv_s), a_s, g_s, *heads[1:])
            mo_p, mo_s, w_out = xo_p, xo_s, P["rw_wo"]
        else:
            proj_p, proj_s = _mm(flat(hp), P["lru_w_in"], j, P["lru_b_in"][j], x2=flat(hs), out_dtype=ACT_DTYPE,
                                 name="lru_in")
            dr = proj_p.shape[1] // 2
            width = P["lru_conv_w"].shape[1]
            cw, cb, lam = P["lru_conv_w"][j], P["lru_conv_b"][j], P["lru_lambda"][j]
            gates = (P["lru_wa"][j], P["lru_ba"][j], P["lru_wx"][j], P["lru_bx"][j])
            hy_p, hl_p, nb8 = _lru_seq(unp(proj_p), cw, cb, *gates, lam, jnp.zeros((gp, 1, dr), F32),
                                       jnp.zeros((gp, SUBLANE, dr), F32))
            out_p["lconv"].append(nb8[:, SUBLANE - (width - 1):, :])
            out_p["h"].append(hl_p.reshape(gp, dr))
            hist = st["lru_conv"][j]
            xc = _lru_conv(uns(proj_s), cw, cb, hist.reshape(1, rs, width - 1, dr))
            out_s["lconv"].append(jnp.concatenate([hist[:, 1:], proj_s[:, None, dr:]], axis=1))
            ga, gx = _lru_gates(flat(xc), *gates)
            hy_s, hl_s = _lru_step(uns(ga), uns(gx), xc, uns(proj_s), lam, st["lru_h"][j].reshape(1, rs, dr))
            out_s["h"].append(hl_s.reshape(rs, dr))
            mo_p, mo_s, w_out = hy_p, hy_s, P["lru_w_out"]
        g1, g2 = P["norm_g"][i, 1], P["norm_g"][i, 2]
        xp, hf_p = _mm_resid(mo_p, w_out, j, xp, g1, mp, 2, g2, 4, 3, name="mix_out")
        xs, hf_s = _mm_resid(mo_s, w_out, j, xs, g1, ms, 2, g2, 4, 3, name="mix_out")
        fdim = P["ffn_w_gate"].shape[2]
        hact_p, nb8, hact_s, nhist = _ffn_up(
            flat(hf_p), P["ffn_w_gate"], P["ffn_w_up"], i, P["ffn_conv_w"][i], P["ffn_conv_b"][i],
            jnp.zeros((gp, SUBLANE, fdim), F32), rp, flat(hf_s), st["ffn_conv"][i])
        out_p["fconv"].append(nb8[:, SUBLANE - 2:, :])
        out_s["fconv"].append(nhist)
        f_p, f_s = _mm(hact_p, P["ffn_w_down"], i, x2=hact_s, out_dtype=ACT_DTYPE, tm_target=512, tn_target=512,
                       name="ffn_down")
        g3 = P["norm_g"][i, 3]
        if i + 1 < depth and (i + 1) % 2 == 1:
            nxt = lambda mod: (P["norm_g"][i + 1, 0], mod, 1, 0)
            xp, hp = _resid_norm(xp, unp(f_p), g3, mp, 5, nxt(mods_p[i + 1]), name="resid_ffn")
            xs, hs = _resid_norm(xs, uns(f_s), g3, ms, 5, nxt(mods_s[i + 1]), name="resid_ffn")
        else:
            xp = _resid_norm(xp, unp(f_p), g3, mp, 5, name="resid_ffn")
            xs = _resid_norm(xs, uns(f_s), g3, ms, 5, name="resid_ffn")
    stack = lambda o: tuple(jnp.stack(o[k]) for k in keys)
    return (xp,) + stack(out_p), (xs,) + stack(out_s)


def kernel(x_prompt, x_sample, c_prompt, c_sample, state_rwkv_wkv, state_rwkv_shift, state_lru_h,
           state_lru_conv, state_ffn_conv, w_mod, b_mod, norm_g, rw_mix, rw_wr, rw_wk, rw_wv, rw_wo,
           rw_w0, rw_w1, rw_w2, rw_a0, rw_a1, rw_a2, rw_g1, rw_g2, rw_kk, rw_ka, rw_rk, rw_lnx_g,
           rw_lnx_b, lru_w_in, lru_b_in, lru_conv_w, lru_conv_b, lru_wa, lru_ba, lru_wx, lru_bx,
           lru_lambda, lru_w_out, ffn_w_gate, ffn_w_up, ffn_w_down, ffn_conv_w, ffn_conv_b):
    P = dict(w_mod=w_mod, b_mod=b_mod, norm_g=norm_g, rw_mix=rw_mix, rw_wr=rw_wr, rw_wk=rw_wk,
             rw_wv=rw_wv, rw_wo=rw_wo, rw_w0=rw_w0, rw_w1=rw_w1, rw_w2=rw_w2, rw_a0=rw_a0, rw_a1=rw_a1,
             rw_a2=rw_a2, rw_g1=rw_g1, rw_g2=rw_g2, rw_kk=rw_kk, rw_ka=rw_ka, rw_rk=rw_rk,
             rw_lnx_g=rw_lnx_g, rw_lnx_b=rw_lnx_b, lru_w_in=lru_w_in, lru_b_in=lru_b_in,
             lru_conv_w=lru_conv_w, lru_conv_b=lru_conv_b, lru_wa=lru_wa, lru_ba=lru_ba, lru_wx=lru_wx,
             lru_bx=lru_bx, lru_lambda=lru_lambda, lru_w_out=lru_w_out, ffn_w_gate=ffn_w_gate,
             ffn_w_up=ffn_w_up, ffn_w_down=ffn_w_down, ffn_conv_w=ffn_conv_w, ffn_conv_b=ffn_conv_b)
    bp, t, d = x_prompt.shape
    bs = x_sample.shape[0]
    depth = w_mod.shape[0]
    c_p = jnp.pad(c_prompt, ((0, -bp % 16), (0, 0)))
    mods_p, mods_s = [], []
    for i in range(depth):
        mod_p, mod_s = _mm(c_p, w_mod, i, b_mod[i], x2=c_sample, pre_act="silu", name="mod")
        mods_p.append(mod_p[:bp].reshape(bp, 1, N_MOD * d))
        mods_s.append(mod_s.reshape(1, bs, N_MOD * d))
    st = dict(wkv=state_rwkv_wkv, shift=state_rwkv_shift, lru_h=state_lru_h,
              lru_conv=state_lru_conv, ffn_conv=state_ffn_conv)
    (y_p, *state_p), (y_s, *state_s) = _trunk(x_prompt, x_sample.reshape(1, bs, d), mods_p, mods_s, st, P)
    return (y_p, y_s.reshape(bs, 1, d), *state_p, *state_s)
```

```python
import functools

import jax
import jax.numpy as jnp
from jax import lax
from jax.experimental import pallas as pl
from jax.experimental.pallas import tpu as pltpu

F32 = jnp.float32
BF16 = jnp.bfloat16
ACT_DTYPE = F32

NORM_EPS = 1e-6
LNX_EPS = 64e-5
LRU_C = 8.0
N_MOD = 6
HEAD = 64
LANE = 128
SUBLANE = 8
CHUNK = 64
PAIR_UNROLL = 16
LRU_SCAN_UNROLL = 4
MM_RESID_SUB_ROWS = 128
FFN_SUB_ROWS = 128
VMEM_LIMIT = 52 * 1024 * 1024


def _cp(n_axes):
    return pltpu.CompilerParams(dimension_semantics=("arbitrary",) * n_axes,
                                vmem_limit_bytes=VMEM_LIMIT)


def _pick(n, target, mult):
    best = None
    for t in range(mult, min(n, target) + 1, mult):
        if n % t == 0:
            best = t
    return best if best is not None else n


def _softplus(z):
    return jnp.maximum(z, 0.0) + jnp.log(1.0 + jnp.exp(-jnp.abs(z)))


def _sigmoid(z):
    return 1.0 / (1.0 + jnp.exp(-z))


def _gelu(x):
    return 0.5 * x * (1.0 + jnp.tanh(0.7978845608028654 * (x + 0.044715 * (x * x * x))))


def _act(x, name):
    if name is None:
        return x
    if name == "tanh":
        return jnp.tanh(x)
    if name == "sigmoid":
        return _sigmoid(x)
    if name == "silu":
        return x * _sigmoid(x)
    raise ValueError(name)


def _rms(x, g):
    ms = jnp.mean(x * x, axis=-1, keepdims=True)
    return x * lax.rsqrt(ms + NORM_EPS) * g


def _dot(x, y):
    return jnp.dot(x, y, preferred_element_type=F32)


def _dot_nt(x, y):
    return lax.dot_general(x, y, (((1,), (1,)), ((), ())), preferred_element_type=F32)


def _split_dot_left(m, x, parts):
    acc = None
    rem = x
    for i in range(parts):
        piece = rem.astype(BF16)
        term = jnp.dot(m, piece, preferred_element_type=F32)
        acc = term if acc is None else acc + term
        if i + 1 < parts:
            rem = rem - piece.astype(F32)
    return acc


def _dot_tn(x, y):
    return lax.dot_general(x, y, (((0,), (0,)), ((), ())), preferred_element_type=F32)


def _head_sum(x):
    d = x.shape[-1]
    first = lax.broadcasted_iota(jnp.int32, (1, LANE), 1) < HEAD
    cols = []
    for s in range(0, d, LANE):
        xs = x[:, s:s + LANE]
        lo = jnp.sum(jnp.where(first, xs, 0.0), axis=-1, keepdims=True)
        hi = jnp.sum(jnp.where(first, 0.0, xs), axis=-1, keepdims=True)
        cols.append(jnp.where(first, lo, hi))
    return cols[0] if len(cols) == 1 else jnp.concatenate(cols, axis=-1)


def _shift_rows(x, carry8, s):
    n = x.shape[0]
    rolled = pltpu.roll(x, s, 0)
    rolled_c = pltpu.roll(carry8, s, 0)
    row8 = lax.broadcasted_iota(jnp.int32, (SUBLANE, x.shape[1]), 0)
    top = jnp.where(row8 < s, rolled_c, rolled[0:SUBLANE])
    if n == SUBLANE:
        return top
    return jnp.concatenate([top, rolled[SUBLANE:]], axis=0)


def _mm_kernel(*refs, has_bias, has_x2, pre_act, act):
    refs = list(refs)
    x_ref, w_ref = refs[:2]
    x2_ref = refs[2] if has_x2 else None
    b_ref = refs[2 + has_x2] if has_bias else None
    o_ref = refs[2 + has_x2 + has_bias]
    o2_ref = refs[3 + has_x2 + has_bias] if has_x2 else None
    wb_ref = refs[-1]

    @pl.when(pl.program_id(1) == 0)
    def _():
        wb_ref[...] = w_ref[...].astype(BF16)

    def project(src, dst):
        x = src[...]
        if pre_act is not None:
            x = _act(x.astype(F32), pre_act)
        acc = _dot(x.astype(BF16), wb_ref[...])
        if has_bias:
            acc = acc + b_ref[...]
        dst[...] = _act(acc, act).astype(dst.dtype)

    project(x_ref, o_ref)
    if has_x2:
        @pl.when(pl.program_id(1) == pl.num_programs(1) - 1)
        def _():
            project(x2_ref, o2_ref)


def _wspec(w, layer, rows, cols, col_of):
    assert w.ndim == 3
    return pl.BlockSpec((None, rows, cols), lambda *ids: (layer, 0, col_of(*ids)))


def _mm(x, w, layer, bias=None, *, x2=None, pre_act=None, act=None, out_dtype=F32, tm_target=1024,
        tn_target=1024, name="mm"):
    m, k = x.shape
    n = w.shape[2]
    tm = _pick(m, tm_target, 16)
    tn = _pick(n, tn_target, LANE)
    in_specs = [pl.BlockSpec((tm, k), lambda j, i: (i, 0)),
                _wspec(w, layer, k, tn, lambda j, i: j)]
    args = [x, w]
    out_shape = [jax.ShapeDtypeStruct((m, n), out_dtype)]
    out_specs = [pl.BlockSpec((tm, tn), lambda j, i: (i, j))]
    if x2 is not None:
        m2 = x2.shape[0]
        in_specs.append(pl.BlockSpec((m2, k), lambda j, i: (0, 0)))
        args.append(x2)
        out_shape.append(jax.ShapeDtypeStruct((m2, n), out_dtype))
        out_specs.append(pl.BlockSpec((m2, tn), lambda j, i: (0, j)))
    if bias is not None:
        in_specs.append(pl.BlockSpec((1, tn), lambda j, i: (0, j)))
        args.append(bias.reshape(1, n))
    res = pl.pallas_call(
        functools.partial(_mm_kernel, has_bias=bias is not None, has_x2=x2 is not None, pre_act=pre_act, act=act),
        out_shape=out_shape,
        grid=(n // tn, m // tm),
        in_specs=in_specs,
        out_specs=out_specs,
        scratch_shapes=[pltpu.VMEM((k, tn), BF16)],
        compiler_params=_cp(2),
        name=name,
    )(*args)
    return res if x2 is not None else res[0]


def _mod_spec(mod, slot, d, tr):
    rm = mod.shape[1]
    if rm == 1:
        return pl.BlockSpec((1, 1, d), lambda g, t: (g, 0, slot))
    return pl.BlockSpec((1, tr, d), lambda g, t: (g, t, slot))


def _resid_norm_kernel(*refs, with_next):
    if with_next:
        x_ref, o_ref, g1_ref, gate_ref, g2_ref, sc_ref, sh_ref, xn_ref, h_ref = refs
    else:
        x_ref, o_ref, g1_ref, gate_ref, xn_ref = refs
    xn = x_ref[0] + gate_ref[0] * _rms(o_ref[0].astype(F32), g1_ref[...])
    xn_ref[0] = xn
    if with_next:
        h_ref[0] = (_rms(xn, g2_ref[...]) * (1.0 + sc_ref[0]) + sh_ref[0]).astype(h_ref.dtype)


def _resid_norm(x, o, g1, mod, gate_slot, nxt=None, name="resid_norm"):
    g, r, d = x.shape
    tr = _pick(r, 256, 16)
    row = pl.BlockSpec((1, tr, d), lambda gi, t: (gi, t, 0))
    vec = pl.BlockSpec((1, d), lambda gi, t: (0, 0))
    in_specs = [row, row, vec, _mod_spec(mod, gate_slot, d, tr)]
    args = [x, o, g1.reshape(1, d), mod]
    out_shape = [jax.ShapeDtypeStruct((g, r, d), F32)]
    out_specs = [row]
    if nxt is not None:
        g2, mod2, sc_slot, sh_slot = nxt
        in_specs += [vec, _mod_spec(mod2, sc_slot, d, tr), _mod_spec(mod2, sh_slot, d, tr)]
        args += [g2.reshape(1, d), mod2, mod2]
        out_shape.append(jax.ShapeDtypeStruct((g, r, d), BF16))
        out_specs.append(row)
    res = pl.pallas_call(
        functools.partial(_resid_norm_kernel, with_next=nxt is not None),
        out_shape=out_shape,
        grid=(g, r // tr),
        in_specs=in_specs,
        out_specs=out_specs,
        compiler_params=_cp(2),
        name=name,
    )(*args)
    return res if nxt is not None else res[0]


def _mm_resid_kernel(a_ref, w_ref, x_ref, g1_ref, gate_ref, g2_ref, sc_ref, sh_ref, xn_ref, h_ref):
    tm = a_ref.shape[1]
    ts = _pick(tm, MM_RESID_SUB_ROWS, 16)

    def rows_of(ref, rows):
        return ref[0] if ref.shape[1] == 1 else ref[0, rows, :]

    for s0 in range(0, tm, ts):
        rows = slice(s0, s0 + ts)
        o = _dot(a_ref[0, rows, :], w_ref[...])
        xn = x_ref[0, rows, :] + rows_of(gate_ref, rows) * _rms(o, g1_ref[...])
        xn_ref[0, rows, :] = xn
        h_ref[0, rows, :] = (_rms(xn, g2_ref[...]) * (1.0 + rows_of(sc_ref, rows))
                             + rows_of(sh_ref, rows)).astype(BF16)


def _mm_resid(a, w, layer, x, g1, mod, gate_slot, g2, sc_slot, sh_slot, name):
    g, r, k = a.shape
    n = w.shape[2]
    wb = w[layer].astype(BF16)
    tm = _pick(r, 512, 16)
    row = lambda c: pl.BlockSpec((1, tm, c), lambda gi, t: (gi, t, 0))
    vec = pl.BlockSpec((1, n), lambda gi, t: (0, 0))
    return pl.pallas_call(
        _mm_resid_kernel,
        out_shape=[jax.ShapeDtypeStruct((g, r, n), F32), jax.ShapeDtypeStruct((g, r, n), BF16)],
        grid=(g, r // tm),
        in_specs=[row(k), pl.BlockSpec((k, n), lambda gi, t: (0, 0)), row(n), vec,
                  _mod_spec(mod, gate_slot, n, tm), vec, _mod_spec(mod, sc_slot, n, tm),
                  _mod_spec(mod, sh_slot, n, tm)],
        out_specs=[row(n), row(n)],
        compiler_params=_cp(2),
        name=name,
    )(a, wb, x, g1.reshape(1, n), mod, g2.reshape(1, n), mod, mod)


def _rwkv_prep_kernel(x_ref, g_ref, sc_ref, sh_ref, mix_ref, shift_ref,
                      w1_ref, w2_ref, w0_ref, a1_ref, a2_ref, a0_ref, g1_ref, g2_ref,
                      xr_ref, xk_ref, xv_ref, wpre_ref, a_ref, gate_ref, shift_out, *scratch, seq):
    h = _rms(x_ref[0], g_ref[...]) * (1.0 + sc_ref[0]) + sh_ref[0]
    tr = h.shape[0]
    if seq:
        carry = scratch[0]
        t = pl.program_id(1)

        @pl.when(t == 0)
        def _():
            carry[...] = jnp.broadcast_to(shift_ref[0], carry.shape)

        hprev = _shift_rows(h, carry[...], 1)
        carry[...] = h[tr - SUBLANE:tr]

        @pl.when(t == pl.num_programs(1) - 1)
        def _():
            shift_out[0] = h[tr - 1:tr]
    else:
        hprev = shift_ref[0]
        shift_out[0] = h
    xx = hprev - h

    def mixed(j):
        return (h + xx * mix_ref[j:j + 1, :]).astype(BF16)

    xr_ref[0] = mixed(0)
    xk_ref[0] = mixed(2)
    xv_ref[0] = mixed(3)
    wpre_ref[0] = w0_ref[...] + _dot(jnp.tanh(_dot(mixed(1), w1_ref[...])).astype(BF16), w2_ref[...])
    a = _sigmoid(a0_ref[...] + _dot(_dot(mixed(4), a1_ref[...]).astype(BF16), a2_ref[...]))
    a_ref[0] = a.astype(a_ref.dtype)
    gate_ref[0] = _dot(_sigmoid(_dot(mixed(5), g1_ref[...])).astype(BF16), g2_ref[...]).astype(gate_ref.dtype)


def _lora_pair(w1, w2):
    r = w1.shape[1]
    rp = -(-r // LANE) * LANE
    if rp != r:
        w1 = jnp.pad(w1, ((0, 0), (0, rp - r)))
        w2 = jnp.pad(w2, ((0, rp - r), (0, 0)))
    return w1.astype(BF16), w2.astype(BF16)


def _rwkv_prep(x, g, mod, mix, shift, seq, lora):
    gg, r, d = x.shape
    tr = _pick(r, 256, 16)
    w1, w2, w0, a1, a2, a0, g1, g2 = lora
    w1, w2 = _lora_pair(w1, w2)
    a1, a2 = _lora_pair(a1, a2)
    g1, g2 = _lora_pair(g1, g2)
    row = pl.BlockSpec((1, tr, d), lambda gi, t: (gi, t, 0))
    vec = pl.BlockSpec((1, d), lambda gi, t: (0, 0))
    full = lambda w: pl.BlockSpec(w.shape, lambda gi, t: (0, 0))
    if seq:
        shift_spec = pl.BlockSpec((1, 1, d), lambda gi, t: (gi, 0, 0))
        shift_shape = (gg, 1, d)
        scratch = [pltpu.VMEM((SUBLANE, d), F32)]
    else:
        shift_spec = row
        shift_shape = (gg, r, d)
        scratch = []
    act = lambda dt: jax.ShapeDtypeStruct((gg, r, d), dt)
    outs = pl.pallas_call(
        functools.partial(_rwkv_prep_kernel, seq=seq),
        out_shape=[act(BF16)] * 3 + [act(F32), act(BF16), act(BF16)] + [jax.ShapeDtypeStruct(shift_shape, F32)],
        grid=(gg, r // tr),
        in_specs=[row, vec, _mod_spec(mod, 1, d, tr), _mod_spec(mod, 0, d, tr),
                  pl.BlockSpec((6, d), lambda gi, t: (0, 0)), shift_spec,
                  full(w1), full(w2), vec, full(a1), full(a2), vec, full(g1), full(g2)],
        out_specs=[row] * 6 + [shift_spec],
        scratch_shapes=scratch,
        compiler_params=_cp(2),
        name="rwkv_prep",
    )(x, g.reshape(1, d), mod, mod, mix, shift, w1, w2, w0.reshape(1, d), a1, a2, a0.reshape(1, d), g1, g2)
    return outs[:6], outs[6]


def _wkv_terms(k, wpre, a, kkp, kap):
    logd = -0.6065306597126334 * _sigmoid(wpre)
    kk = k * kkp
    kk = kk * lax.rsqrt(jnp.maximum(_head_sum(kk * kk), 1e-24))
    kmod = k * (1.0 + (a - 1.0) * kap)
    return logd, kk, kmod


def _wkv_readout(y, bonus, gate, lnx_g, lnx_b):
    inv_n = 1.0 / HEAD
    yc = y - _head_sum(y) * inv_n
    var = _head_sum(yc * yc) * inv_n
    return ((yc * lax.rsqrt(var + LNX_EPS) * lnx_g + lnx_b + bonus) * gate).astype(BF16)


def _wkv_chunk_kernel(r_ref, k_ref, v_ref, w_ref, a_ref, gate_ref, kkp_ref, kap_ref, rk_ref, lg_ref, lb_ref,
                      xo_ref, sout_ref, S, Rt, At, Bt, Kt, Vs, Gc, Ys, Bonus):
    c = pl.program_id(1)
    n_pairs = S.shape[0]
    C = r_ref.shape[1]

    @pl.when(c == 0)
    def _():
        S[...] = jnp.zeros(S.shape, F32)

    a = a_ref[0].astype(F32)
    r = r_ref[0].astype(F32)
    logd, kk, kmod = _wkv_terms(k_ref[0].astype(F32), w_ref[0], a, kkp_ref[...], kap_ref[...])
    ri = lax.broadcasted_iota(jnp.int32, (C, C), 0)
    ci = lax.broadcasted_iota(jnp.int32, (C, C), 1)
    tri = (ri >= ci).astype(BF16)
    cum = _split_dot_left(tri, logd, 3)
    gam = jnp.exp(cum)
    rt = r * gam
    at = -kk * jnp.exp(cum - logd)
    ginv = jnp.exp(-cum)
    bt = kk * a * ginv
    kt = kmod * ginv
    v = v_ref[0].astype(F32)
    Bonus[...] = _head_sum(r * kmod * rk_ref[...]) * v
    for p in range(n_pairs):
        sl = slice(p * LANE, (p + 1) * LANE)
        Rt[p] = rt[:, sl]
        At[p] = at[:, sl]
        Bt[p] = bt[:, sl]
        Kt[p] = kt[:, sl]
        Vs[p] = v[:, sl]
        Gc[p] = jnp.broadcast_to(gam[C - 1:C, sl], (SUBLANE, LANE))

    lane = lax.broadcasted_iota(jnp.int32, (1, LANE), 1)
    first = lane < HEAD
    r2 = lax.broadcasted_iota(jnp.int32, (2 * C, 2 * C), 0)
    c2 = lax.broadcasted_iota(jnp.int32, (2 * C, 2 * C), 1)
    same_blk = (r2 < C) == (c2 < C)
    strict = jnp.logical_and(r2 > c2, same_blk)
    incl = jnp.logical_and(r2 >= c2, same_blk)
    rs = lax.broadcasted_iota(jnp.int32, (LANE, LANE), 0)
    cs = lax.broadcasted_iota(jnp.int32, (LANE, LANE), 1)
    same_head = (rs < HEAD) == (cs < HEAD)
    zero_blk = jnp.zeros((2 * C, LANE), BF16)

    def stack(x):
        return jnp.concatenate([jnp.where(first, x, 0.0), jnp.where(first, 0.0, x)], axis=0).astype(BF16)

    def twice(x):
        xb = x.astype(BF16)
        return jnp.concatenate([xb, xb], axis=0)

    n_fac = max(1, (C - 1).bit_length())

    def pairs_step(ps):
        n = range(len(ps))
        rp, ap, bp = [Rt[p] for p in ps], [At[p] for p in ps], [Bt[p] for p in ps]
        kp, vp = [Kt[p] for p in ps], [Vs[p] for p in ps]
        sp, gc = [S[p] for p in ps], [Gc[p] for p in ps]
        am, rm = [stack(t) for t in ap], [stack(t) for t in rp]
        bm, km = [twice(t) for t in bp], [twice(t) for t in kp]
        ar = [jnp.concatenate([am[i], rm[i]], axis=0) for i in n]
        gb = [_dot_nt(ar[i], bm[i]) for i in n]
        gk = [_dot_nt(ar[i], km[i]) for i in n]
        lp = [jnp.where(strict, t[:2 * C], 0.0) for t in gb]
        lak = [jnp.where(strict, t[:2 * C], 0.0).astype(BF16) for t in gk]
        grbk = [jnp.concatenate([jnp.where(incl, gb[i][2 * C:], 0.0), jnp.where(incl, gk[i][2 * C:], 0.0)],
                                axis=1).astype(BF16) for i in n]
        spb = [t.astype(BF16) for t in sp]
        ars = [_dot_nt(jnp.concatenate([ap[i], rp[i]], axis=0).astype(BF16), spb[i]) for i in n]
        a_s, r_s = [t[:C] for t in ars], [t[C:] for t in ars]
        vst = [jnp.concatenate([t, t], axis=0).astype(BF16) for t in vp]
        x = [jnp.concatenate([a_s[i], a_s[i]], axis=0) + _dot(lak[i], vst[i]) for i in n]
        for f in range(n_fac):
            lh = [t.astype(BF16) for t in lp]
            xh = [t.astype(BF16) for t in x]
            xl = [(x[i] - xh[i].astype(F32)).astype(BF16) for i in n]
            lhs = [jnp.concatenate([t, t], axis=1) for t in lh]
            if f + 1 < n_fac:
                rhs = [jnp.concatenate([jnp.concatenate([xh[i], lh[i]], axis=1),
                                        jnp.concatenate([xl[i], zero_blk], axis=1)], axis=0) for i in n]
                prod = [_dot(lhs[i], rhs[i]) for i in n]
                x = [x[i] + prod[i][:, :LANE] for i in n]
                lp = [t[:, LANE:] for t in prod]
            else:
                x = [x[i] + _dot(lhs[i], jnp.concatenate([xh[i], xl[i]], axis=0)) for i in n]
        yst = [jnp.concatenate([r_s[i], r_s[i]], axis=0)
               + _dot(grbk[i], jnp.concatenate([x[i].astype(BF16), vst[i]], axis=0)) for i in n]
        u = [jnp.where(first, t[:C], t[C:]) for t in x]
        ds = [_dot_tn(jnp.concatenate([u[i], vp[i]], axis=0).astype(BF16),
                      jnp.concatenate([bp[i], kp[i]], axis=0).astype(BF16)) for i in n]
        for i, p in enumerate(ps):
            Ys[p] = jnp.where(first, yst[i][:C], yst[i][C:])
            S[p] = jnp.where(same_head, (sp[i] + ds[i]) * gc[i][0:1, :], 0.0)

    for p0 in range(0, n_pairs, PAIR_UNROLL):
        pairs_step(list(range(p0, min(p0 + PAIR_UNROLL, n_pairs))))

    y = jnp.concatenate([Ys[p] for p in range(n_pairs)], axis=-1) if n_pairs > 1 else Ys[0]
    xo_ref[0] = _wkv_readout(y, Bonus[...], gate_ref[0].astype(F32), lg_ref[...], lb_ref[...])

    @pl.when(c == pl.num_programs(1) - 1)
    def _():
        sout_ref[0] = S[...]


def _wkv_chunked(r, k, v, wpre, a, gate, kkp, kap, rk, lnx_g, lnx_b):
    g, t, d = r.shape
    C = CHUNK
    np_ = d // LANE
    row = pl.BlockSpec((1, C, d), lambda gi, ci: (gi, ci, 0))
    vec = pl.BlockSpec((1, d), lambda gi, ci: (0, 0))
    pair_buf = pltpu.VMEM((np_, C, LANE), F32)
    return pl.pallas_call(
        _wkv_chunk_kernel,
        out_shape=[jax.ShapeDtypeStruct((g, t, d), BF16),
                   jax.ShapeDtypeStruct((g, np_, LANE, LANE), F32)],
        grid=(g, t // C),
        in_specs=[row] * 6 + [vec] * 5,
        out_specs=[row, pl.BlockSpec((1, np_, LANE, LANE), lambda gi, ci: (gi, 0, 0, 0))],
        scratch_shapes=[pltpu.VMEM((np_, LANE, LANE), F32), pair_buf, pair_buf, pair_buf, pair_buf, pair_buf,
                        pltpu.VMEM((np_, SUBLANE, LANE), F32), pair_buf, pltpu.VMEM((C, d), F32)],
        compiler_params=_cp(2),
        name="wkv_chunk",
    )(r, k, v, wpre, a, gate, *[p.reshape(1, d) for p in (kkp, kap, rk, lnx_g, lnx_b)])


def _wkv_vec_kernel(k_ref, w_ref, a_ref, r_ref, v_ref, kkp_ref, kap_ref,
                    d_ref, an_ref, b_ref, km_ref, rt_ref, vt_ref):
    a = a_ref[...].astype(F32)
    logd, kk, kmod = _wkv_terms(k_ref[...].astype(F32), w_ref[...], a, kkp_ref[...], kap_ref[...])
    d_ref[...] = jnp.exp(logd).T
    an_ref[...] = (-kk).T
    b_ref[...] = (kk * a).T
    km_ref[...] = kmod.T
    rt_ref[...] = r_ref[...].astype(F32).T
    vt_ref[...] = v_ref[...].astype(F32).T


def _wkv_vec(k, wpre, a, r, v, kkp, kap):
    b, d = k.shape
    full = pl.BlockSpec((b, d), lambda i: (0, 0))
    fullt = pl.BlockSpec((d, b), lambda i: (0, 0))
    vec = pl.BlockSpec((1, d), lambda i: (0, 0))
    return pl.pallas_call(
        _wkv_vec_kernel,
        out_shape=[jax.ShapeDtypeStruct((d, b), F32)] * 6,
        grid=(1,),
        in_specs=[full] * 5 + [vec, vec],
        out_specs=[fullt] * 6,
        compiler_params=_cp(1),
        name="wkv_vec",
    )(k, wpre, a, r, v, kkp.reshape(1, d), kap.reshape(1, d))


def _wkv_step_kernel(s_ref, d_ref, a_ref, b_ref, k_ref, r_ref, v_ref, sn_ref, y_ref):
    s = s_ref[...]
    key = lambda ref: ref[...][:, None]
    sa = jnp.sum(s * key(a_ref), axis=2, keepdims=True)
    sn = s * key(d_ref) + sa * key(b_ref) + v_ref[...] * key(k_ref)
    sn_ref[...] = sn
    y_ref[...] = jnp.sum(sn * key(r_ref), axis=2, keepdims=True)


def _wkv_step(s, dvec, an, bvec, kmod, r, v):
    h, n, _, b = s.shape
    hb = _pick(h, 2, 1)
    sblk = pl.BlockSpec((hb, n, n, b), lambda i: (i, 0, 0, 0))
    kblk = pl.BlockSpec((hb, n, b), lambda i: (i, 0, 0))
    vblk = pl.BlockSpec((hb, n, 1, b), lambda i: (i, 0, 0, 0))
    kv = lambda x: x.reshape(h, n, b)
    sn, y = pl.pallas_call(
        _wkv_step_kernel,
        out_shape=[jax.ShapeDtypeStruct((h, n, n, b), F32), jax.ShapeDtypeStruct((h, n, 1, b), F32)],
        grid=(h // hb,),
        in_specs=[sblk] + [kblk] * 5 + [vblk],
        out_specs=[sblk, vblk],
        compiler_params=_cp(1),
        name="wkv_step",
    )(s, kv(dvec), kv(an), kv(bvec), kv(kmod), kv(r), v.reshape(h, n, 1, b))
    return sn, y.reshape(h * n, b)


def _wkv_post_kernel(y_ref, r_ref, k_ref, v_ref, a_ref, g_ref, kap_ref, rk_ref, lg_ref, lb_ref, o_ref):
    f32 = lambda ref: ref[0].astype(F32)
    kmod = f32(k_ref) * (1.0 + (f32(a_ref) - 1.0) * kap_ref[...])
    bonus = _head_sum(f32(r_ref) * kmod * rk_ref[...]) * f32(v_ref)
    o_ref[0] = _wkv_readout(y_ref[0], bonus, f32(g_ref), lg_ref[...], lb_ref[...])


def _wkv_post(y, r, k, v, a, g, kap, rk, lnx_g, lnx_b):
    gg, t, d = y.shape
    tr = _pick(t, 256, 16)
    row = pl.BlockSpec((1, tr, d), lambda gi, ti: (gi, ti, 0))
    vec = pl.BlockSpec((1, d), lambda gi, ti: (0, 0))
    return pl.pallas_call(
        _wkv_post_kernel,
        out_shape=jax.ShapeDtypeStruct((gg, t, d), BF16),
        grid=(gg, t // tr),
        in_specs=[row] * 6 + [vec] * 4,
        out_specs=row,
        compiler_params=_cp(2),
        name="wkv_post",
    )(y, r, k, v, a, g, kap.reshape(1, d), rk.reshape(1, d), lnx_g.reshape(1, d), lnx_b.reshape(1, d))


def _lru_conv_kernel(x_ref, w_ref, b_ref, buf_ref, xc_ref, *, width):
    acc = b_ref[...] + x_ref[0].astype(F32) * w_ref[width - 1:width, :]
    for s in range(1, width):
        acc = acc + buf_ref[0, :, width - 1 - s, :] * w_ref[width - 1 - s:width - s, :]
    xc_ref[0] = acc


def _lru_conv(proj, w, b, buf):
    g, r, d2 = proj.shape
    d = d2 // 2
    width = w.shape[0]
    tr = _pick(r, 256, 16)
    row = pl.BlockSpec((1, tr, d), lambda gi, t: (gi, t, 0))
    return pl.pallas_call(
        functools.partial(_lru_conv_kernel, width=width),
        out_shape=jax.ShapeDtypeStruct((g, r, d), F32),
        grid=(g, r // tr),
        in_specs=[pl.BlockSpec((1, tr, d), lambda gi, t: (gi, t, 1)),
                  pl.BlockSpec((width, d), lambda gi, t: (0, 0)),
                  pl.BlockSpec((1, d), lambda gi, t: (0, 0)),
                  pl.BlockSpec((1, tr, width - 1, d), lambda gi, t: (gi, t, 0, 0))],
        out_specs=row,
        compiler_params=_cp(2),
        name="lru_conv",
    )(proj, w, b.reshape(1, d), buf)


def _lru_gates_kernel(x_ref, wa_ref, ba_ref, wx_ref, bx_ref, ga_ref, gx_ref):
    x = x_ref[...].astype(BF16)
    ga_ref[...] = _sigmoid(_dot(x, wa_ref[0].astype(BF16)) + ba_ref[...])
    gx_ref[...] = _sigmoid(_dot(x, wx_ref[0].astype(BF16)) + bx_ref[...])


def _lru_gates(xc, wa, ba, wx, bx):
    m, d = xc.shape
    nb, bw, _ = wa.shape
    tm = _pick(m, 1024, 8)
    xs = pl.BlockSpec((tm, bw), lambda n, i: (i, n))
    ws = pl.BlockSpec((1, bw, bw), lambda n, i: (n, 0, 0))
    bs = pl.BlockSpec((1, bw), lambda n, i: (0, n))
    return pl.pallas_call(
        _lru_gates_kernel,
        out_shape=[jax.ShapeDtypeStruct((m, d), F32)] * 2,
        grid=(nb, m // tm),
        in_specs=[xs, ws, bs, ws, bs],
        out_specs=[xs, xs],
        compiler_params=_cp(2),
        name="lru_gates",
    )(xc, wa, ba.reshape(1, d), wx, bx.reshape(1, d))


def _lru_coeffs(ga, gx, xc, lam):
    log_a = -LRU_C * ga * _softplus(-lam)
    a = jnp.exp(log_a)
    mult = jnp.sqrt(1.0 - a * a)
    return a, mult


def _lru_seq_kernel(xin_ref, y_ref, cw_ref, cb_ref, wa_ref, ba_ref, wx_ref, bx_ref, lam_ref, h0_ref, buf_ref,
                    o_ref, hl_ref, nbuf_ref, cconv, carry, abuf, bbuf, hbuf, *, width):
    t = pl.program_id(2)
    tr = xin_ref.shape[1]

    @pl.when(t == 0)
    def _():
        cconv[...] = buf_ref[0]
        carry[...] = jnp.broadcast_to(h0_ref[0], carry.shape)

    x = xin_ref[0].astype(F32)
    c8 = cconv[...]
    xc = cb_ref[...] + x * cw_ref[width - 1:width, :]
    for s in range(1, width):
        xc = xc + _shift_rows(x, c8, s) * cw_ref[width - 1 - s:width - s, :]
    cconv[...] = x[tr - SUBLANE:tr]
    nbuf_ref[0] = x[tr - SUBLANE:tr]
    xb = xc.astype(BF16)
    ga = _sigmoid(_dot(xb, wa_ref[...].astype(BF16)) + ba_ref[...])
    gx = _sigmoid(_dot(xb, wx_ref[...].astype(BF16)) + bx_ref[...])
    a, mult = _lru_coeffs(ga, gx, xc, lam_ref[...])
    row = lax.broadcasted_iota(jnp.int32, a.shape, 0)
    mult = jnp.where(jnp.logical_and(t == 0, row == 0), 1.0, mult)
    abuf[...] = a
    bbuf[...] = mult * gx * xc
    r8 = lax.broadcasted_iota(jnp.int32, (SUBLANE, a.shape[1]), 0)

    def body(i, c):
        off = pl.multiple_of(i * SUBLANE, SUBLANE)
        aa = abuf[pl.ds(off, SUBLANE), :]
        bb = bbuf[pl.ds(off, SUBLANE), :]
        for s in (1, 2, 4):
            a_s = jnp.where(r8 >= s, pltpu.roll(aa, s, 0), 1.0)
            b_s = jnp.where(r8 >= s, pltpu.roll(bb, s, 0), 0.0)
            bb = aa * b_s + bb
            aa = aa * a_s
        h = aa * c + bb
        hbuf[pl.ds(off, SUBLANE), :] = h
        return h[SUBLANE - 1:SUBLANE, :]

    c = lax.fori_loop(0, tr // SUBLANE, body, carry[0:1, :], unroll=LRU_SCAN_UNROLL)
    carry[...] = jnp.broadcast_to(c, carry.shape)
    o_ref[0] = (hbuf[...] * _gelu(y_ref[0].astype(F32))).astype(BF16)

    @pl.when(t == pl.num_programs(2) - 1)
    def _():
        hl_ref[0] = c


def _lru_seq(proj, cw, cb, wa, ba, wx, bx, lam, h0, buf):
    g, r, d2 = proj.shape
    d = d2 // 2
    nb, bw, _ = wa.shape
    width = cw.shape[0]
    tr = _pick(r, 512, 16)
    row = lambda off: pl.BlockSpec((1, tr, bw), lambda gi, n, t: (gi, t, n + off))
    vec = pl.BlockSpec((1, bw), lambda gi, n, t: (0, n))
    wsp = pl.BlockSpec((None, bw, bw), lambda gi, n, t: (n, 0, 0))
    st = pl.BlockSpec((1, 1, bw), lambda gi, n, t: (gi, 0, n))
    st8 = pl.BlockSpec((1, SUBLANE, bw), lambda gi, n, t: (gi, 0, n))
    tile = pltpu.VMEM((tr, bw), F32)
    small = pltpu.VMEM((SUBLANE, bw), F32)
    return pl.pallas_call(
        functools.partial(_lru_seq_kernel, width=width),
        out_shape=[jax.ShapeDtypeStruct((g, r, d), BF16), jax.ShapeDtypeStruct((g, 1, d), F32),
                   jax.ShapeDtypeStruct((g, SUBLANE, d), F32)],
        grid=(g, nb, r // tr),
        in_specs=[row(nb), row(0), pl.BlockSpec((width, bw), lambda gi, n, t: (0, n)), vec,
                  wsp, vec, wsp, vec, vec, st, st8],
        out_specs=[row(0), st, st8],
        scratch_shapes=[small, small, tile, tile, tile],
        compiler_params=_cp(3),
        name="lru_seq",
    )(proj, proj, cw, cb.reshape(1, d), wa, ba.reshape(1, d), wx, bx.reshape(1, d), lam.reshape(1, d), h0, buf)


def _lru_step_kernel(ga_ref, gx_ref, xc_ref, y_ref, lam_ref, h0_ref, o_ref, h_ref):
    xc = xc_ref[0]
    a, mult = _lru_coeffs(ga_ref[0], gx_ref[0], xc, lam_ref[...])
    h = a * h0_ref[0] + mult * gx_ref[0] * xc
    h_ref[0] = h
    o_ref[0] = (h * _gelu(y_ref[0].astype(F32))).astype(BF16)


def _lru_step(ga, gx, xc, proj, lam, h0):
    g, r, d = ga.shape
    row = pl.BlockSpec((1, r, d), lambda i: (0, 0, 0))
    return pl.pallas_call(
        _lru_step_kernel,
        out_shape=[jax.ShapeDtypeStruct((g, r, d), BF16), jax.ShapeDtypeStruct((g, r, d), F32)],
        grid=(1,),
        in_specs=[row, row, row, row, pl.BlockSpec((1, d), lambda i: (0, 0)), row],
        out_specs=[row, row],
        compiler_params=_cp(1),
        name="lru_step",
    )(ga, gx, xc, proj, lam.reshape(1, d), h0)


def _ffn_up_kernel(x_ref, wg_ref, wu_ref, cw_ref, cb_ref, buf_ref, x2_ref, hist2_ref,
                   o_ref, nb_ref, o2_ref, nhist2_ref, wgb, wub, carry, *, tiles_per_seq):
    i = pl.program_id(1)

    @pl.when(i == 0)
    def _():
        wgb[...] = wg_ref[...].astype(BF16)
        wub[...] = wu_ref[...].astype(BF16)

    def gated(gt, p1, p2, u):
        gc = cb_ref[...] + p2 * cw_ref[0:1, :] + p1 * cw_ref[1:2, :] + gt * cw_ref[2:3, :]
        return (_gelu(gc) * u).astype(BF16)

    tm = x_ref.shape[0]
    ts = _pick(tm, FFN_SUB_ROWS, 16)

    @pl.when(i % tiles_per_seq == 0)
    def _():
        carry[...] = buf_ref[0]

    c8 = carry[...]
    for s0 in range(0, tm, ts):
        rows = slice(s0, s0 + ts)
        x = x_ref[rows, :]
        gt = _dot(x, wgb[...])
        u = _dot(x, wub[...])
        o_ref[rows, :] = gated(gt, _shift_rows(gt, c8, 1), _shift_rows(gt, c8, 2), u)
        c8 = gt[ts - SUBLANE:ts]
    carry[...] = c8
    nb_ref[0] = c8

    @pl.when(i == pl.num_programs(1) - 1)
    def _():
        x2 = x2_ref[...]
        gt = _dot(x2, wgb[...])
        p1 = hist2_ref[:, 1, :]
        o2_ref[...] = gated(gt, p1, hist2_ref[:, 0, :], _dot(x2, wub[...]))
        nhist2_ref[:, 0, :] = p1
        nhist2_ref[:, 1, :] = gt


def _ffn_up(x, wg, wu, layer, cw, cb, buf, rows_per_seq, x2, hist2):
    m, k = x.shape
    m2 = x2.shape[0]
    f = wg.shape[2]
    tn = _pick(f, 512, LANE)
    tm = _pick(rows_per_seq, 2048, 16)
    tps = rows_per_seq // tm
    g = m // rows_per_seq
    ws = _wspec(wg, layer, k, tn, lambda j, i: j)
    bspec = pl.BlockSpec((1, SUBLANE, tn), lambda j, i: (i // tps, 0, j))
    hspec = pl.BlockSpec((m2, 2, tn), lambda j, i: (0, 0, j))
    o2spec = pl.BlockSpec((m2, tn), lambda j, i: (0, j))
    return pl.pallas_call(
        functools.partial(_ffn_up_kernel, tiles_per_seq=tps),
        out_shape=[jax.ShapeDtypeStruct((m, f), BF16), jax.ShapeDtypeStruct((g, SUBLANE, f), F32),
                   jax.ShapeDtypeStruct((m2, f), BF16), jax.ShapeDtypeStruct((m2, 2, f), F32)],
        grid=(f // tn, m // tm),
        in_specs=[pl.BlockSpec((tm, k), lambda j, i: (i, 0)), ws, ws,
                  pl.BlockSpec((3, tn), lambda j, i: (0, j)), pl.BlockSpec((1, tn), lambda j, i: (0, j)),
                  bspec, pl.BlockSpec((m2, k), lambda j, i: (0, 0)), hspec],
        out_specs=[pl.BlockSpec((tm, tn), lambda j, i: (i, j)), bspec, o2spec, hspec],
        scratch_shapes=[pltpu.VMEM((k, tn), BF16), pltpu.VMEM((k, tn), BF16), pltpu.VMEM((SUBLANE, tn), F32)],
        compiler_params=_cp(2),
        name="ffn_up",
    )(x, wg, wu, cw, cb.reshape(1, f), buf, x2, hist2)


def _trunk(xp, xs, mods_p, mods_s, st, P):
    gp, rp, d = xp.shape
    rs = xs.shape[1]
    depth = P["w_mod"].shape[0]
    keys = ("wkv", "shift", "h", "lconv", "fconv")
    out_p = {k: [] for k in keys}
    out_s = {k: [] for k in keys}
    flat = lambda t: t.reshape(-1, t.shape[-1])
    unp = lambda t: t.reshape(gp, rp, t.shape[-1])
    uns = lambda t: t.reshape(1, rs, t.shape[-1])
    hp = hs = None
    for i in range(depth):
        mp, ms = mods_p[i], mods_s[i]
        j = i // 2
        if i % 2 == 0:
            lora = tuple(P[nm][j] for nm in ("rw_w1", "rw_w2", "rw_w0", "rw_a1", "rw_a2", "rw_a0", "rw_g1", "rw_g2"))
            g0, mix = P["norm_g"][i, 0], P["rw_mix"][j]
            pre_p, shift_p = _rwkv_prep(xp, g0, mp, mix, jnp.zeros((gp, 1, d), F32), True, lora)
            pre_s, shift_s = _rwkv_prep(xs, g0, ms, mix, st["shift"][j].reshape(1, rs, d), False, lora)
            xr_p, xk_p, xv_p, w_p, a_p, g_p = pre_p
            xr_s, xk_s, xv_s, w_s, a_s, g_s = pre_s
            rr_p, rr_s = _mm(flat(xr_p), P["rw_wr"], j, x2=flat(xr_s), out_dtype=ACT_DTYPE, name="rw_r")
            kk_p, kk_s = _mm(flat(xk_p), P["rw_wk"], j, x2=flat(xk_s), out_dtype=ACT_DTYPE, name="rw_k")
            vv_p, vv_s = _mm(flat(xv_p), P["rw_wv"], j, x2=flat(xv_s), out_dtype=ACT_DTYPE, name="rw_v")
            heads = (P["rw_kk"][j], P["rw_ka"][j], P["rw_rk"][j], P["rw_lnx_g"][j], P["rw_lnx_b"][j])
            xo_p, s_bd = _wkv_chunked(unp(rr_p), unp(kk_p), unp(vv_p), w_p, a_p, g_p, *heads)
            npairs = d // LANE
            s_new = jnp.stack([s_bd[:, :, :HEAD, :HEAD], s_bd[:, :, HEAD:, HEAD:]], axis=2)
            out_p["wkv"].append(s_new.reshape(gp, 2 * npairs, HEAD, HEAD))
            out_p["shift"].append(shift_p.reshape(gp, d))
            s_t = jnp.transpose(st["wkv"][j], (1, 2, 3, 0))
            vecs = _wkv_vec(kk_s, flat(w_s), flat(a_s), rr_s, vv_s, heads[0], heads[1])
            s_t, y_s = _wkv_step(s_t, *vecs)
            out_s["wkv"].append(jnp.transpose(s_t, (3, 0, 1, 2)))
            out_s["shift"].append(shift_s.reshape(rs, d))
            xo_s = _wkv_post(y_s.T.reshape(1, rs, d), uns(rr_s), uns(kk_s), uns(vv_s), a_s, g_s, *heads[1:])
            mo_p, mo_s, w_out = xo_p, xo_s, P["rw_wo"]
        else:
            proj_p, proj_s = _mm(flat(hp), P["lru_w_in"], j, P["lru_b_in"][j], x2=flat(hs), out_dtype=ACT_DTYPE,
                                 name="lru_in")
            dr = proj_p.shape[1] // 2
            width = P["lru_conv_w"].shape[1]
            cw, cb, lam = P["lru_conv_w"][j], P["lru_conv_b"][j], P["lru_lambda"][j]
            gates = (P["lru_wa"][j], P["lru_ba"][j], P["lru_wx"][j], P["lru_bx"][j])
            hy_p, hl_p, nb8 = _lru_seq(unp(proj_p), cw, cb, *gates, lam, jnp.zeros((gp, 1, dr), F32),
                                       jnp.zeros((gp, SUBLANE, dr), F32))
            out_p["lconv"].append(nb8[:, SUBLANE - (width - 1):, :])
            out_p["h"].append(hl_p.reshape(gp, dr))
            hist = st["lru_conv"][j]
            xc = _lru_conv(uns(proj_s), cw, cb, hist.reshape(1, rs, width - 1, dr))
            out_s["lconv"].append(jnp.concatenate([hist[:, 1:], proj_s[:, None, dr:]], axis=1))
            ga, gx = _lru_gates(flat(xc), *gates)
            hy_s, hl_s = _lru_step(uns(ga), uns(gx), xc, uns(proj_s), lam, st["lru_h"][j].reshape(1, rs, dr))
            out_s["h"].append(hl_s.reshape(rs, dr))
            mo_p, mo_s, w_out = hy_p, hy_s, P["lru_w_out"]
        g1, g2 = P["norm_g"][i, 1], P["norm_g"][i, 2]
        xp, hf_p = _mm_resid(mo_p, w_out, j, xp, g1, mp, 2, g2, 4, 3, name="mix_out")
        xs, hf_s = _mm_resid(mo_s, w_out, j, xs, g1, ms, 2, g2, 4, 3, name="mix_out")
        fdim = P["ffn_w_gate"].shape[2]
        hact_p, nb8, hact_s, nhist = _ffn_up(
            flat(hf_p), P["ffn_w_gate"], P["ffn_w_up"], i, P["ffn_conv_w"][i], P["ffn_conv_b"][i],
            jnp.zeros((gp, SUBLANE, fdim), F32), rp, flat(hf_s), st["ffn_conv"][i])
        out_p["fconv"].append(nb8[:, SUBLANE - 2:, :])
        out_s["fconv"].append(nhist)
        f_p, f_s = _mm(hact_p, P["ffn_w_down"], i, x2=hact_s, out_dtype=ACT_DTYPE, tm_target=512, tn_target=512,
                       name="ffn_down")
        g3 = P["norm_g"][i, 3]
        if i + 1 < depth and (i + 1) % 2 == 1:
            nxt = lambda mod: (P["norm_g"][i + 1, 0], mod, 1, 0)
            xp, hp = _resid_norm(xp, unp(f_p), g3, mp, 5, nxt(mods_p[i + 1]), name="resid_ffn")
            xs, hs = _resid_norm(xs, uns(f_s), g3, ms, 5, nxt(mods_s[i + 1]), name="resid_ffn")
        else:
            xp = _resid_norm(xp, unp(f_p), g3, mp, 5, name="resid_ffn")
            xs = _resid_norm(xs, uns(f_s), g3, ms, 5, name="resid_ffn")
    stack = lambda o: tuple(jnp.stack(o[k]) for k in keys)
    return (xp,) + stack(out_p), (xs,) + stack(out_s)


def kernel(x_prompt, x_sample, c_prompt, c_sample, state_rwkv_wkv, state_rwkv_shift, state_lru_h,
           state_lru_conv, state_ffn_conv, w_mod, b_mod, norm_g, rw_mix, rw_wr, rw_wk, rw_wv, rw_wo,
           rw_w0, rw_w1, rw_w2, rw_a0, rw_a1, rw_a2, rw_g1, rw_g2, rw_kk, rw_ka, rw_rk, rw_lnx_g,
           rw_lnx_b, lru_w_in, lru_b_in, lru_conv_w, lru_conv_b, lru_wa, lru_ba, lru_wx, lru_bx,
           lru_lambda, lru_w_out, ffn_w_gate, ffn_w_up, ffn_w_down, ffn_conv_w, ffn_conv_b):
    P = dict(w_mod=w_mod, b_mod=b_mod, norm_g=norm_g, rw_mix=rw_mix, rw_wr=rw_wr, rw_wk=rw_wk,
             rw_wv=rw_wv, rw_wo=rw_wo, rw_w0=rw_w0, rw_w1=rw_w1, rw_w2=rw_w2, rw_a0=rw_a0, rw_a1=rw_a1,
             rw_a2=rw_a2, rw_g1=rw_g1, rw_g2=rw_g2, rw_kk=rw_kk, rw_ka=rw_ka, rw_rk=rw_rk,
             rw_lnx_g=rw_lnx_g, rw_lnx_b=rw_lnx_b, lru_w_in=lru_w_in, lru_b_in=lru_b_in,
             lru_conv_w=lru_conv_w, lru_conv_b=lru_conv_b, lru_wa=lru_wa, lru_ba=lru_ba, lru_wx=lru_wx,
             lru_bx=lru_bx, lru_lambda=lru_lambda, lru_w_out=lru_w_out, ffn_w_gate=ffn_w_gate,
             ffn_w_up=ffn_w_up, ffn_w_down=ffn_w_down, ffn_conv_w=ffn_conv_w, ffn_conv_b=ffn_conv_b)
    bp, t, d = x_prompt.shape
    bs = x_sample.shape[0]
    depth = w_mod.shape[0]
    c_p = jnp.pad(c_prompt, ((0, -bp % 16), (0, 0)))
    mods_p, mods_s = [], []
    for i in range(depth):
        mod_p, mod_s = _mm(c_p, w_mod, i, b_mod[i], x2=c_sample, pre_act="silu", name="mod")
        mods_p.append(mod_p[:bp].reshape(bp, 1, N_MOD * d))
        mods_s.append(mod_s.reshape(1, bs, N_MOD * d))
    st = dict(wkv=state_rwkv_wkv, shift=state_rwkv_shift, lru_h=state_lru_h,
              lru_conv=state_lru_conv, ffn_conv=state_ffn_conv)
    (y_p, *state_p), (y_s, *state_s) = _trunk(x_prompt, x_sample.reshape(1, bs, d), mods_p, mods_s, st, P)
    return (y_p, y_s.reshape(bs, 1, d), *state_p, *state_s)
```

```python
import functools

import jax
import jax.numpy as jnp
from jax import lax
from jax.experimental import pallas as pl
from jax.experimental.pallas import tpu as pltpu

F32 = jnp.float32
BF16 = jnp.bfloat16
ACT_DTYPE = F32

NORM_EPS = 1e-6
LNX_EPS = 64e-5
LRU_C = 8.0
N_MOD = 6
HEAD = 64
LANE = 128
SUBLANE = 8
CHUNK = 64
PAIR_UNROLL = 16
LRU_SCAN_UNROLL = 4
MM_RESID_SUB_ROWS = 128
FFN_SUB_ROWS = 128
VMEM_LIMIT = 52 * 1024 * 1024


def _cp(n_axes):
    return pltpu.CompilerParams(dimension_semantics=("arbitrary",) * n_axes,
                                vmem_limit_bytes=VMEM_LIMIT)


def _pick(n, target, mult):
    best = None
    for t in range(mult, min(n, target) + 1, mult):
        if n % t == 0:
            best = t
    return best if best is not None else n


def _softplus(z):
    return jnp.maximum(z, 0.0) + jnp.log(1.0 + jnp.exp(-jnp.abs(z)))


def _sigmoid(z):
    return 1.0 / (1.0 + jnp.exp(-z))


def _gelu(x):
    return 0.5 * x * (1.0 + jnp.tanh(0.7978845608028654 * (x + 0.044715 * (x * x * x))))


def _act(x, name):
    if name is None:
        return x
    if name == "tanh":
        return jnp.tanh(x)
    if name == "sigmoid":
        return _sigmoid(x)
    if name == "silu":
        return x * _sigmoid(x)
    raise ValueError(name)


def _rms(x, g):
    ms = jnp.mean(x * x, axis=-1, keepdims=True)
    return x * lax.rsqrt(ms + NORM_EPS) * g


def _dot(x, y):
    return jnp.dot(x, y, preferred_element_type=F32)


def _dot_nt(x, y):
    return lax.dot_general(x, y, (((1,), (1,)), ((), ())), preferred_element_type=F32)


def _split_dot_left(m, x, parts):
    acc = None
    rem = x
    for i in range(parts):
        piece = rem.astype(BF16)
        term = jnp.dot(m, piece, preferred_element_type=F32)
        acc = term if acc is None else acc + term
        if i + 1 < parts:
            rem = rem - piece.astype(F32)
    return acc


def _dot_tn(x, y):
    return lax.dot_general(x, y, (((0,), (0,)), ((), ())), preferred_element_type=F32)


def _head_sum(x):
    d = x.shape[-1]
    first = lax.broadcasted_iota(jnp.int32, (1, LANE), 1) < HEAD
    cols = []
    for s in range(0, d, LANE):
        xs = x[:, s:s + LANE]
        lo = jnp.sum(jnp.where(first, xs, 0.0), axis=-1, keepdims=True)
        hi = jnp.sum(jnp.where(first, 0.0, xs), axis=-1, keepdims=True)
        cols.append(jnp.where(first, lo, hi))
    return cols[0] if len(cols) == 1 else jnp.concatenate(cols, axis=-1)


def _shift_rows(x, carry8, s):
    n = x.shape[0]
    rolled = pltpu.roll(x, s, 0)
    rolled_c = pltpu.roll(carry8, s, 0)
    row8 = lax.broadcasted_iota(jnp.int32, (SUBLANE, x.shape[1]), 0)
    top = jnp.where(row8 < s, rolled_c, rolled[0:SUBLANE])
    if n == SUBLANE:
        return top
    return jnp.concatenate([top, rolled[SUBLANE:]], axis=0)


def _mm_kernel(*refs, has_bias, has_x2, pre_act, act):
    refs = list(refs)
    x_ref, w_ref = refs[:2]
    x2_ref = refs[2] if has_x2 else None
    b_ref = refs[2 + has_x2] if has_bias else None
    o_ref = refs[2 + has_x2 + has_bias]
    o2_ref = refs[3 + has_x2 + has_bias] if has_x2 else None
    wb_ref = refs[-1]

    @pl.when(pl.program_id(1) == 0)
    def _():
        wb_ref[...] = w_ref[...].astype(BF16)

    def project(src, dst):
        x = src[...]
        if pre_act is not None:
            x = _act(x.astype(F32), pre_act)
        acc = _dot(x.astype(BF16), wb_ref[...])
        if has_bias:
            acc = acc + b_ref[...]
        dst[...] = _act(acc, act).astype(dst.dtype)

    project(x_ref, o_ref)
    if has_x2:
        @pl.when(pl.program_id(1) == pl.num_programs(1) - 1)
        def _():
            project(x2_ref, o2_ref)


def _wspec(w, layer, rows, cols, col_of):
    assert w.ndim == 3
    return pl.BlockSpec((None, rows, cols), lambda *ids: (layer, 0, col_of(*ids)))


def _mm(x, w, layer, bias=None, *, x2=None, pre_act=None, act=None, out_dtype=F32, tm_target=1024,
        tn_target=1024, name="mm"):
    m, k = x.shape
    n = w.shape[2]
    tm = _pick(m, tm_target, 16)
    tn = _pick(n, tn_target, LANE)
    in_specs = [pl.BlockSpec((tm, k), lambda j, i: (i, 0)),
                _wspec(w, layer, k, tn, lambda j, i: j)]
    args = [x, w]
    out_shape = [jax.ShapeDtypeStruct((m, n), out_dtype)]
    out_specs = [pl.BlockSpec((tm, tn), lambda j, i: (i, j))]
    if x2 is not None:
        m2 = x2.shape[0]
        in_specs.append(pl.BlockSpec((m2, k), lambda j, i: (0, 0)))
        args.append(x2)
        out_shape.append(jax.ShapeDtypeStruct((m2, n), out_dtype))
        out_specs.append(pl.BlockSpec((m2, tn), lambda j, i: (0, j)))
    if bias is not None:
        in_specs.append(pl.BlockSpec((1, tn), lambda j, i: (0, j)))
        args.append(bias.reshape(1, n))
    res = pl.pallas_call(
        functools.partial(_mm_kernel, has_bias=bias is not None, has_x2=x2 is not None, pre_act=pre_act, act=act),
        out_shape=out_shape,
        grid=(n // tn, m // tm),
        in_specs=in_specs,
        out_specs=out_specs,
        scratch_shapes=[pltpu.VMEM((k, tn), BF16)],
        compiler_params=_cp(2),
        name=name,
    )(*args)
    return res if x2 is not None else res[0]


def _mod_spec(mod, slot, d, tr):
    rm = mod.shape[1]
    if rm == 1:
        return pl.BlockSpec((1, 1, d), lambda g, t: (g, 0, slot))
    return pl.BlockSpec((1, tr, d), lambda g, t: (g, t, slot))


def _resid_norm_kernel(*refs, with_next):
    if with_next:
        x_ref, o_ref, g1_ref, gate_ref, g2_ref, sc_ref, sh_ref, xn_ref, h_ref = refs
    else:
        x_ref, o_ref, g1_ref, gate_ref, xn_ref = refs
    xn = x_ref[0] + gate_ref[0] * _rms(o_ref[0].astype(F32), g1_ref[...])
    xn_ref[0] = xn
    if with_next:
        h_ref[0] = (_rms(xn, g2_ref[...]) * (1.0 + sc_ref[0]) + sh_ref[0]).astype(h_ref.dtype)


def _resid_norm(x, o, g1, mod, gate_slot, nxt=None, name="resid_norm"):
    g, r, d = x.shape
    tr = _pick(r, 256, 16)
    row = pl.BlockSpec((1, tr, d), lambda gi, t: (gi, t, 0))
    vec = pl.BlockSpec((1, d), lambda gi, t: (0, 0))
    in_specs = [row, row, vec, _mod_spec(mod, gate_slot, d, tr)]
    args = [x, o, g1.reshape(1, d), mod]
    out_shape = [jax.ShapeDtypeStruct((g, r, d), F32)]
    out_specs = [row]
    if nxt is not None:
        g2, mod2, sc_slot, sh_slot = nxt
        in_specs += [vec, _mod_spec(mod2, sc_slot, d, tr), _mod_spec(mod2, sh_slot, d, tr)]
        args += [g2.reshape(1, d), mod2, mod2]
        out_shape.append(jax.ShapeDtypeStruct((g, r, d), BF16))
        out_specs.append(row)
    res = pl.pallas_call(
        functools.partial(_resid_norm_kernel, with_next=nxt is not None),
        out_shape=out_shape,
        grid=(g, r // tr),
        in_specs=in_specs,
        out_specs=out_specs,
        compiler_params=_cp(2),
        name=name,
    )(*args)
    return res if nxt is not None else res[0]


def _mm_resid_kernel(a_ref, w_ref, x_ref, g1_ref, gate_ref, g2_ref, sc_ref, sh_ref, xn_ref, h_ref, wb_ref):
    @pl.when(jnp.logical_and(pl.program_id(0) == 0, pl.program_id(1) == 0))
    def _():
        wb_ref[...] = w_ref[...].astype(BF16)

    tm = a_ref.shape[1]
    ts = _pick(tm, MM_RESID_SUB_ROWS, 16)

    def rows_of(ref, rows):
        return ref[0] if ref.shape[1] == 1 else ref[0, rows, :]

    for s0 in range(0, tm, ts):
        rows = slice(s0, s0 + ts)
        o = _dot(a_ref[0, rows, :], wb_ref[...])
        xn = x_ref[0, rows, :] + rows_of(gate_ref, rows) * _rms(o, g1_ref[...])
        xn_ref[0, rows, :] = xn
        h_ref[0, rows, :] = (_rms(xn, g2_ref[...]) * (1.0 + rows_of(sc_ref, rows))
                             + rows_of(sh_ref, rows)).astype(BF16)


def _mm_resid(a, w, layer, x, g1, mod, gate_slot, g2, sc_slot, sh_slot, name):
    g, r, k = a.shape
    n = w.shape[2]
    tm = _pick(r, 512, 16)
    row = lambda c: pl.BlockSpec((1, tm, c), lambda gi, t: (gi, t, 0))
    vec = pl.BlockSpec((1, n), lambda gi, t: (0, 0))
    wspec = pl.BlockSpec((None, k, n), lambda gi, t: (layer, 0, 0), pipeline_mode=pl.Buffered(1))
    return pl.pallas_call(
        _mm_resid_kernel,
        out_shape=[jax.ShapeDtypeStruct((g, r, n), F32), jax.ShapeDtypeStruct((g, r, n), BF16)],
        grid=(g, r // tm),
        in_specs=[row(k), wspec, row(n), vec,
                  _mod_spec(mod, gate_slot, n, tm), vec, _mod_spec(mod, sc_slot, n, tm),
                  _mod_spec(mod, sh_slot, n, tm)],
        out_specs=[row(n), row(n)],
        scratch_shapes=[pltpu.VMEM((k, n), BF16)],
        compiler_params=_cp(2),
        name=name,
    )(a, w, x, g1.reshape(1, n), mod, g2.reshape(1, n), mod, mod)


def _rwkv_prep_kernel(x_ref, g_ref, sc_ref, sh_ref, mix_ref, shift_ref,
                      w1_ref, w2_ref, w0_ref, a1_ref, a2_ref, a0_ref, g1_ref, g2_ref,
                      xr_ref, xk_ref, xv_ref, wpre_ref, a_ref, gate_ref, shift_out, *scratch, seq):
    h = _rms(x_ref[0], g_ref[...]) * (1.0 + sc_ref[0]) + sh_ref[0]
    tr = h.shape[0]
    if seq:
        carry = scratch[0]
        t = pl.program_id(1)

        @pl.when(t == 0)
        def _():
            carry[...] = jnp.broadcast_to(shift_ref[0], carry.shape)

        hprev = _shift_rows(h, carry[...], 1)
        carry[...] = h[tr - SUBLANE:tr]

        @pl.when(t == pl.num_programs(1) - 1)
        def _():
            shift_out[0] = h[tr - 1:tr]
    else:
        hprev = shift_ref[0]
        shift_out[0] = h
    xx = hprev - h

    def mixed(j):
        return (h + xx * mix_ref[j:j + 1, :]).astype(BF16)

    xr_ref[0] = mixed(0)
    xk_ref[0] = mixed(2)
    xv_ref[0] = mixed(3)
    wpre_ref[0] = w0_ref[...] + _dot(jnp.tanh(_dot(mixed(1), w1_ref[...])).astype(BF16), w2_ref[...])
    a = _sigmoid(a0_ref[...] + _dot(_dot(mixed(4), a1_ref[...]).astype(BF16), a2_ref[...]))
    a_ref[0] = a.astype(a_ref.dtype)
    gate_ref[0] = _dot(_sigmoid(_dot(mixed(5), g1_ref[...])).astype(BF16), g2_ref[...]).astype(gate_ref.dtype)


def _lora_pair(w1, w2):
    r = w1.shape[1]
    rp = -(-r // LANE) * LANE
    if rp != r:
        w1 = jnp.pad(w1, ((0, 0), (0, rp - r)))
        w2 = jnp.pad(w2, ((0, rp - r), (0, 0)))
    return w1.astype(BF16), w2.astype(BF16)


def _rwkv_prep(x, g, mod, mix, shift, seq, lora):
    gg, r, d = x.shape
    tr = _pick(r, 256, 16)
    w1, w2, w0, a1, a2, a0, g1, g2 = lora
    w1, w2 = _lora_pair(w1, w2)
    a1, a2 = _lora_pair(a1, a2)
    g1, g2 = _lora_pair(g1, g2)
    row = pl.BlockSpec((1, tr, d), lambda gi, t: (gi, t, 0))
    vec = pl.BlockSpec((1, d), lambda gi, t: (0, 0))
    full = lambda w: pl.BlockSpec(w.shape, lambda gi, t: (0, 0))
    if seq:
        shift_spec = pl.BlockSpec((1, 1, d), lambda gi, t: (gi, 0, 0))
        shift_shape = (gg, 1, d)
        scratch = [pltpu.VMEM((SUBLANE, d), F32)]
    else:
        shift_spec = row
        shift_shape = (gg, r, d)
        scratch = []
    act = lambda dt: jax.ShapeDtypeStruct((gg, r, d), dt)
    outs = pl.pallas_call(
        functools.partial(_rwkv_prep_kernel, seq=seq),
        out_shape=[act(BF16)] * 3 + [act(F32), act(BF16), act(BF16)] + [jax.ShapeDtypeStruct(shift_shape, F32)],
        grid=(gg, r // tr),
        in_specs=[row, vec, _mod_spec(mod, 1, d, tr), _mod_spec(mod, 0, d, tr),
                  pl.BlockSpec((6, d), lambda gi, t: (0, 0)), shift_spec,
                  full(w1), full(w2), vec, full(a1), full(a2), vec, full(g1), full(g2)],
        out_specs=[row] * 6 + [shift_spec],
        scratch_shapes=scratch,
        compiler_params=_cp(2),
        name="rwkv_prep",
    )(x, g.reshape(1, d), mod, mod, mix, shift, w1, w2, w0.reshape(1, d), a1, a2, a0.reshape(1, d), g1, g2)
    return outs[:6], outs[6]


def _wkv_terms(k, wpre, a, kkp, kap):
    logd = -0.6065306597126334 * _sigmoid(wpre)
    kk = k * kkp
    kk = kk * lax.rsqrt(jnp.maximum(_head_sum(kk * kk), 1e-24))
    kmod = k * (1.0 + (a - 1.0) * kap)
    return logd, kk, kmod


def _wkv_readout(y, bonus, gate, lnx_g, lnx_b):
    inv_n = 1.0 / HEAD
    yc = y - _head_sum(y) * inv_n
    var = _head_sum(yc * yc) * inv_n
    return ((yc * lax.rsqrt(var + LNX_EPS) * lnx_g + lnx_b + bonus) * gate).astype(BF16)


def _wkv_chunk_kernel(r_ref, k_ref, v_ref, w_ref, a_ref, gate_ref, kkp_ref, kap_ref, rk_ref, lg_ref, lb_ref,
                      xo_ref, sout_ref, S, Rt, At, Bt, Kt, Vs, Gc, Ys, Bonus):
    c = pl.program_id(1)
    n_pairs = S.shape[0]
    C = r_ref.shape[1]

    @pl.when(c == 0)
    def _():
        S[...] = jnp.zeros(S.shape, F32)

    a = a_ref[0].astype(F32)
    r = r_ref[0].astype(F32)
    logd, kk, kmod = _wkv_terms(k_ref[0].astype(F32), w_ref[0], a, kkp_ref[...], kap_ref[...])
    ri = lax.broadcasted_iota(jnp.int32, (C, C), 0)
    ci = lax.broadcasted_iota(jnp.int32, (C, C), 1)
    tri = (ri >= ci).astype(BF16)
    cum = _split_dot_left(tri, logd, 3)
    gam = jnp.exp(cum)
    rt = r * gam
    at = -kk * jnp.exp(cum - logd)
    ginv = jnp.exp(-cum)
    bt = kk * a * ginv
    kt = kmod * ginv
    v = v_ref[0].astype(F32)
    Bonus[...] = _head_sum(r * kmod * rk_ref[...]) * v
    for p in range(n_pairs):
        sl = slice(p * LANE, (p + 1) * LANE)
        Rt[p] = rt[:, sl]
        At[p] = at[:, sl]
        Bt[p] = bt[:, sl]
        Kt[p] = kt[:, sl]
        Vs[p] = v[:, sl]
        Gc[p] = jnp.broadcast_to(gam[C - 1:C, sl], (SUBLANE, LANE))

    lane = lax.broadcasted_iota(jnp.int32, (1, LANE), 1)
    first = lane < HEAD
    r2 = lax.broadcasted_iota(jnp.int32, (2 * C, 2 * C), 0)
    c2 = lax.broadcasted_iota(jnp.int32, (2 * C, 2 * C), 1)
    same_blk = (r2 < C) == (c2 < C)
    strict = jnp.logical_and(r2 > c2, same_blk)
    incl = jnp.logical_and(r2 >= c2, same_blk)
    rs = lax.broadcasted_iota(jnp.int32, (LANE, LANE), 0)
    cs = lax.broadcasted_iota(jnp.int32, (LANE, LANE), 1)
    same_head = (rs < HEAD) == (cs < HEAD)
    zero_blk = jnp.zeros((2 * C, LANE), BF16)

    def stack(x):
        return jnp.concatenate([jnp.where(first, x, 0.0), jnp.where(first, 0.0, x)], axis=0).astype(BF16)

    def twice(x):
        xb = x.astype(BF16)
        return jnp.concatenate([xb, xb], axis=0)

    n_fac = max(1, (C - 1).bit_length())

    def pairs_step(ps):
        n = range(len(ps))
        rp, ap, bp = [Rt[p] for p in ps], [At[p] for p in ps], [Bt[p] for p in ps]
        kp, vp = [Kt[p] for p in ps], [Vs[p] for p in ps]
        sp, gc = [S[p] for p in ps], [Gc[p] for p in ps]
        am, rm = [stack(t) for t in ap], [stack(t) for t in rp]
        bm, km = [twice(t) for t in bp], [twice(t) for t in kp]
        ar = [jnp.concatenate([am[i], rm[i]], axis=0) for i in n]
        gb = [_dot_nt(ar[i], bm[i]) for i in n]
        gk = [_dot_nt(ar[i], km[i]) for i in n]
        lp = [jnp.where(strict, t[:2 * C], 0.0) for t in gb]
        lak = [jnp.where(strict, t[:2 * C], 0.0).astype(BF16) for t in gk]
        grbk = [jnp.concatenate([jnp.where(incl, gb[i][2 * C:], 0.0), jnp.where(incl, gk[i][2 * C:], 0.0)],
                                axis=1).astype(BF16) for i in n]
        spb = [t.astype(BF16) for t in sp]
        ars = [_dot_nt(jnp.concatenate([ap[i], rp[i]], axis=0).astype(BF16), spb[i]) for i in n]
        a_s, r_s = [t[:C] for t in ars], [t[C:] for t in ars]
        vst = [jnp.concatenate([t, t], axis=0).astype(BF16) for t in vp]
        x = [jnp.concatenate([a_s[i], a_s[i]], axis=0) + _dot(lak[i], vst[i]) for i in n]
        for f in range(n_fac):
            lh = [t.astype(BF16) for t in lp]
            xh = [t.astype(BF16) for t in x]
            xl = [(x[i] - xh[i].astype(F32)).astype(BF16) for i in n]
            lhs = [jnp.concatenate([t, t], axis=1) for t in lh]
            if f + 1 < n_fac:
                rhs = [jnp.concatenate([jnp.concatenate([xh[i], lh[i]], axis=1),
                                        jnp.concatenate([xl[i], zero_blk], axis=1)], axis=0) for i in n]
                prod = [_dot(lhs[i], rhs[i]) for i in n]
                x = [x[i] + prod[i][:, :LANE] for i in n]
                lp = [t[:, LANE:] for t in prod]
            else:
                x = [x[i] + _dot(lhs[i], jnp.concatenate([xh[i], xl[i]], axis=0)) for i in n]
        yst = [jnp.concatenate([r_s[i], r_s[i]], axis=0)
               + _dot(grbk[i], jnp.concatenate([x[i].astype(BF16), vst[i]], axis=0)) for i in n]
        u = [jnp.where(first, t[:C], t[C:]) for t in x]
        ds = [_dot_tn(jnp.concatenate([u[i], vp[i]], axis=0).astype(BF16),
                      jnp.concatenate([bp[i], kp[i]], axis=0).astype(BF16)) for i in n]
        for i, p in enumerate(ps):
            Ys[p] = jnp.where(first, yst[i][:C], yst[i][C:])
            S[p] = jnp.where(same_head, (sp[i] + ds[i]) * gc[i][0:1, :], 0.0)

    for p0 in range(0, n_pairs, PAIR_UNROLL):
        pairs_step(list(range(p0, min(p0 + PAIR_UNROLL, n_pairs))))

    y = jnp.concatenate([Ys[p] for p in range(n_pairs)], axis=-1) if n_pairs > 1 else Ys[0]
    xo_ref[0] = _wkv_readout(y, Bonus[...], gate_ref[0].astype(F32), lg_ref[...], lb_ref[...])

    @pl.when(c == pl.num_programs(1) - 1)
    def _():
        sout_ref[0] = S[...]


def _wkv_chunked(r, k, v, wpre, a, gate, kkp, kap, rk, lnx_g, lnx_b):
    g, t, d = r.shape
    C = CHUNK
    np_ = d // LANE
    row = pl.BlockSpec((1, C, d), lambda gi, ci: (gi, ci, 0))
    vec = pl.BlockSpec((1, d), lambda gi, ci: (0, 0))
    pair_buf = pltpu.VMEM((np_, C, LANE), F32)
    return pl.pallas_call(
        _wkv_chunk_kernel,
        out_shape=[jax.ShapeDtypeStruct((g, t, d), BF16),
                   jax.ShapeDtypeStruct((g, np_, LANE, LANE), F32)],
        grid=(g, t // C),
        in_specs=[row] * 6 + [vec] * 5,
        out_specs=[row, pl.BlockSpec((1, np_, LANE, LANE), lambda gi, ci: (gi, 0, 0, 0))],
        scratch_shapes=[pltpu.VMEM((np_, LANE, LANE), F32), pair_buf, pair_buf, pair_buf, pair_buf, pair_buf,
                        pltpu.VMEM((np_, SUBLANE, LANE), F32), pair_buf, pltpu.VMEM((C, d), F32)],
        compiler_params=_cp(2),
        name="wkv_chunk",
    )(r, k, v, wpre, a, gate, *[p.reshape(1, d) for p in (kkp, kap, rk, lnx_g, lnx_b)])


def _wkv_vec_kernel(k_ref, w_ref, a_ref, r_ref, v_ref, kkp_ref, kap_ref,
                    d_ref, an_ref, b_ref, km_ref, rt_ref, vt_ref):
    a = a_ref[...].astype(F32)
    logd, kk, kmod = _wkv_terms(k_ref[...].astype(F32), w_ref[...], a, kkp_ref[...], kap_ref[...])
    d_ref[...] = jnp.exp(logd).T
    an_ref[...] = (-kk).T
    b_ref[...] = (kk * a).T
    km_ref[...] = kmod.T
    rt_ref[...] = r_ref[...].astype(F32).T
    vt_ref[...] = v_ref[...].astype(F32).T


def _wkv_vec(k, wpre, a, r, v, kkp, kap):
    b, d = k.shape
    full = pl.BlockSpec((b, d), lambda i: (0, 0))
    fullt = pl.BlockSpec((d, b), lambda i: (0, 0))
    vec = pl.BlockSpec((1, d), lambda i: (0, 0))
    return pl.pallas_call(
        _wkv_vec_kernel,
        out_shape=[jax.ShapeDtypeStruct((d, b), F32)] * 6,
        grid=(1,),
        in_specs=[full] * 5 + [vec, vec],
        out_specs=[fullt] * 6,
        compiler_params=_cp(1),
        name="wkv_vec",
    )(k, wpre, a, r, v, kkp.reshape(1, d), kap.reshape(1, d))


def _wkv_step_kernel(s_ref, d_ref, a_ref, b_ref, k_ref, r_ref, v_ref, sn_ref, y_ref):
    s = s_ref[...]
    key = lambda ref: ref[...][:, None]
    sa = jnp.sum(s * key(a_ref), axis=2, keepdims=True)
    sn = s * key(d_ref) + sa * key(b_ref) + v_ref[...] * key(k_ref)
    sn_ref[...] = sn
    y_ref[...] = jnp.sum(sn * key(r_ref), axis=2, keepdims=True)


def _wkv_step(s, dvec, an, bvec, kmod, r, v):
    h, n, _, b = s.shape
    hb = _pick(h, 2, 1)
    sblk = pl.BlockSpec((hb, n, n, b), lambda i: (i, 0, 0, 0))
    kblk = pl.BlockSpec((hb, n, b), lambda i: (i, 0, 0))
    vblk = pl.BlockSpec((hb, n, 1, b), lambda i: (i, 0, 0, 0))
    kv = lambda x: x.reshape(h, n, b)
    sn, y = pl.pallas_call(
        _wkv_step_kernel,
        out_shape=[jax.ShapeDtypeStruct((h, n, n, b), F32), jax.ShapeDtypeStruct((h, n, 1, b), F32)],
        grid=(h // hb,),
        in_specs=[sblk] + [kblk] * 5 + [vblk],
        out_specs=[sblk, vblk],
        compiler_params=_cp(1),
        name="wkv_step",
    )(s, kv(dvec), kv(an), kv(bvec), kv(kmod), kv(r), v.reshape(h, n, 1, b))
    return sn, y.reshape(h * n, b)


def _wkv_post_kernel(y_ref, r_ref, k_ref, v_ref, a_ref, g_ref, kap_ref, rk_ref, lg_ref, lb_ref, o_ref):
    f32 = lambda ref: ref[0].astype(F32)
    kmod = f32(k_ref) * (1.0 + (f32(a_ref) - 1.0) * kap_ref[...])
    bonus = _head_sum(f32(r_ref) * kmod * rk_ref[...]) * f32(v_ref)
    o_ref[0] = _wkv_readout(y_ref[0], bonus, f32(g_ref), lg_ref[...], lb_ref[...])


def _wkv_post(y, r, k, v, a, g, kap, rk, lnx_g, lnx_b):
    gg, t, d = y.shape
    tr = _pick(t, 256, 16)
    row = pl.BlockSpec((1, tr, d), lambda gi, ti: (gi, ti, 0))
    vec = pl.BlockSpec((1, d), lambda gi, ti: (0, 0))
    return pl.pallas_call(
        _wkv_post_kernel,
        out_shape=jax.ShapeDtypeStruct((gg, t, d), BF16),
        grid=(gg, t // tr),
        in_specs=[row] * 6 + [vec] * 4,
        out_specs=row,
        compiler_params=_cp(2),
        name="wkv_post",
    )(y, r, k, v, a, g, kap.reshape(1, d), rk.reshape(1, d), lnx_g.reshape(1, d), lnx_b.reshape(1, d))


def _lru_conv_kernel(x_ref, w_ref, b_ref, buf_ref, xc_ref, *, width):
    acc = b_ref[...] + x_ref[0].astype(F32) * w_ref[width - 1:width, :]
    for s in range(1, width):
        acc = acc + buf_ref[0, :, width - 1 - s, :] * w_ref[width - 1 - s:width - s, :]
    xc_ref[0] = acc


def _lru_conv(proj, w, b, buf):
    g, r, d2 = proj.shape
    d = d2 // 2
    width = w.shape[0]
    tr = _pick(r, 256, 16)
    row = pl.BlockSpec((1, tr, d), lambda gi, t: (gi, t, 0))
    return pl.pallas_call(
        functools.partial(_lru_conv_kernel, width=width),
        out_shape=jax.ShapeDtypeStruct((g, r, d), F32),
        grid=(g, r // tr),
        in_specs=[pl.BlockSpec((1, tr, d), lambda gi, t: (gi, t, 1)),
                  pl.BlockSpec((width, d), lambda gi, t: (0, 0)),
                  pl.BlockSpec((1, d), lambda gi, t: (0, 0)),
                  pl.BlockSpec((1, tr, width - 1, d), lambda gi, t: (gi, t, 0, 0))],
        out_specs=row,
        compiler_params=_cp(2),
        name="lru_conv",
    )(proj, w, b.reshape(1, d), buf)


def _lru_gates_kernel(x_ref, wa_ref, ba_ref, wx_ref, bx_ref, ga_ref, gx_ref):
    x = x_ref[...].astype(BF16)
    ga_ref[...] = _sigmoid(_dot(x, wa_ref[0].astype(BF16)) + ba_ref[...])
    gx_ref[...] = _sigmoid(_dot(x, wx_ref[0].astype(BF16)) + bx_ref[...])


def _lru_gates(xc, wa, ba, wx, bx):
    m, d = xc.shape
    nb, bw, _ = wa.shape
    tm = _pick(m, 1024, 8)
    xs = pl.BlockSpec((tm, bw), lambda n, i: (i, n))
    ws = pl.BlockSpec((1, bw, bw), lambda n, i: (n, 0, 0))
    bs = pl.BlockSpec((1, bw), lambda n, i: (0, n))
    return pl.pallas_call(
        _lru_gates_kernel,
        out_shape=[jax.ShapeDtypeStruct((m, d), F32)] * 2,
        grid=(nb, m // tm),
        in_specs=[xs, ws, bs, ws, bs],
        out_specs=[xs, xs],
        compiler_params=_cp(2),
        name="lru_gates",
    )(xc, wa, ba.reshape(1, d), wx, bx.reshape(1, d))


def _lru_coeffs(ga, gx, xc, lam):
    log_a = -LRU_C * ga * _softplus(-lam)
    a = jnp.exp(log_a)
    mult = jnp.sqrt(1.0 - a * a)
    return a, mult


def _lru_seq_kernel(xin_ref, y_ref, cw_ref, cb_ref, wa_ref, ba_ref, wx_ref, bx_ref, lam_ref, h0_ref, buf_ref,
                    o_ref, hl_ref, nbuf_ref, cconv, carry, abuf, bbuf, hbuf, *, width):
    t = pl.program_id(2)
    tr = xin_ref.shape[1]

    @pl.when(t == 0)
    def _():
        cconv[...] = buf_ref[0]
        carry[...] = jnp.broadcast_to(h0_ref[0], carry.shape)

    x = xin_ref[0].astype(F32)
    c8 = cconv[...]
    xc = cb_ref[...] + x * cw_ref[width - 1:width, :]
    for s in range(1, width):
        xc = xc + _shift_rows(x, c8, s) * cw_ref[width - 1 - s:width - s, :]
    cconv[...] = x[tr - SUBLANE:tr]
    nbuf_ref[0] = x[tr - SUBLANE:tr]
    xb = xc.astype(BF16)
    ga = _sigmoid(_dot(xb, wa_ref[...].astype(BF16)) + ba_ref[...])
    gx = _sigmoid(_dot(xb, wx_ref[...].astype(BF16)) + bx_ref[...])
    a, mult = _lru_coeffs(ga, gx, xc, lam_ref[...])
    row = lax.broadcasted_iota(jnp.int32, a.shape, 0)
    mult = jnp.where(jnp.logical_and(t == 0, row == 0), 1.0, mult)
    abuf[...] = a
    bbuf[...] = mult * gx * xc
    r8 = lax.broadcasted_iota(jnp.int32, (SUBLANE, a.shape[1]), 0)

    def body(i, c):
        off = pl.multiple_of(i * SUBLANE, SUBLANE)
        aa = abuf[pl.ds(off, SUBLANE), :]
        bb = bbuf[pl.ds(off, SUBLANE), :]
        for s in (1, 2, 4):
            a_s = jnp.where(r8 >= s, pltpu.roll(aa, s, 0), 1.0)
            b_s = jnp.where(r8 >= s, pltpu.roll(bb, s, 0), 0.0)
            bb = aa * b_s + bb
            aa = aa * a_s
        h = aa * c + bb
        hbuf[pl.ds(off, SUBLANE), :] = h
        return h[SUBLANE - 1:SUBLANE, :]

    c = lax.fori_loop(0, tr // SUBLANE, body, carry[0:1, :], unroll=LRU_SCAN_UNROLL)
    carry[...] = jnp.broadcast_to(c, carry.shape)
    o_ref[0] = (hbuf[...] * _gelu(y_ref[0].astype(F32))).astype(BF16)

    @pl.when(t == pl.num_programs(2) - 1)
    def _():
        hl_ref[0] = c


def _lru_seq(proj, cw, cb, wa, ba, wx, bx, lam, h0, buf):
    g, r, d2 = proj.shape
    d = d2 // 2
    nb, bw, _ = wa.shape
    width = cw.shape[0]
    tr = _pick(r, 512, 16)
    row = lambda off: pl.BlockSpec((1, tr, bw), lambda gi, n, t: (gi, t, n + off))
    vec = pl.BlockSpec((1, bw), lambda gi, n, t: (0, n))
    wsp = pl.BlockSpec((None, bw, bw), lambda gi, n, t: (n, 0, 0))
    st = pl.BlockSpec((1, 1, bw), lambda gi, n, t: (gi, 0, n))
    st8 = pl.BlockSpec((1, SUBLANE, bw), lambda gi, n, t: (gi, 0, n))
    tile = pltpu.VMEM((tr, bw), F32)
    small = pltpu.VMEM((SUBLANE, bw), F32)
    return pl.pallas_call(
        functools.partial(_lru_seq_kernel, width=width),
        out_shape=[jax.ShapeDtypeStruct((g, r, d), BF16), jax.ShapeDtypeStruct((g, 1, d), F32),
                   jax.ShapeDtypeStruct((g, SUBLANE, d), F32)],
        grid=(g, nb, r // tr),
        in_specs=[row(nb), row(0), pl.BlockSpec((width, bw), lambda gi, n, t: (0, n)), vec,
                  wsp, vec, wsp, vec, vec, st, st8],
        out_specs=[row(0), st, st8],
        scratch_shapes=[small, small, tile, tile, tile],
        compiler_params=_cp(3),
        name="lru_seq",
    )(proj, proj, cw, cb.reshape(1, d), wa, ba.reshape(1, d), wx, bx.reshape(1, d), lam.reshape(1, d), h0, buf)


def _lru_step_kernel(ga_ref, gx_ref, xc_ref, y_ref, lam_ref, h0_ref, o_ref, h_ref):
    xc = xc_ref[0]
    a, mult = _lru_coeffs(ga_ref[0], gx_ref[0], xc, lam_ref[...])
    h = a * h0_ref[0] + mult * gx_ref[0] * xc
    h_ref[0] = h
    o_ref[0] = (h * _gelu(y_ref[0].astype(F32))).astype(BF16)


def _lru_step(ga, gx, xc, proj, lam, h0):
    g, r, d = ga.shape
    row = pl.BlockSpec((1, r, d), lambda i: (0, 0, 0))
    return pl.pallas_call(
        _lru_step_kernel,
        out_shape=[jax.ShapeDtypeStruct((g, r, d), BF16), jax.ShapeDtypeStruct((g, r, d), F32)],
        grid=(1,),
        in_specs=[row, row, row, row, pl.BlockSpec((1, d), lambda i: (0, 0)), row],
        out_specs=[row, row],
        compiler_params=_cp(1),
        name="lru_step",
    )(ga, gx, xc, proj, lam.reshape(1, d), h0)


def _ffn_up_kernel(x_ref, wg_ref, wu_ref, cw_ref, cb_ref, buf_ref, x2_ref, hist2_ref,
                   o_ref, nb_ref, o2_ref, nhist2_ref, wgb, wub, carry, *, tiles_per_seq):
    i = pl.program_id(1)

    @pl.when(i == 0)
    def _():
        wgb[...] = wg_ref[...].astype(BF16)
        wub[...] = wu_ref[...].astype(BF16)

    def gated(gt, p1, p2, u):
        gc = cb_ref[...] + p2 * cw_ref[0:1, :] + p1 * cw_ref[1:2, :] + gt * cw_ref[2:3, :]
        return (_gelu(gc) * u).astype(BF16)

    tm = x_ref.shape[0]
    ts = _pick(tm, FFN_SUB_ROWS, 16)

    @pl.when(i % tiles_per_seq == 0)
    def _():
        carry[...] = buf_ref[0]

    c8 = carry[...]
    for s0 in range(0, tm, ts):
        rows = slice(s0, s0 + ts)
        x = x_ref[rows, :]
        gt = _dot(x, wgb[...])
        u = _dot(x, wub[...])
        o_ref[rows, :] = gated(gt, _shift_rows(gt, c8, 1), _shift_rows(gt, c8, 2), u)
        c8 = gt[ts - SUBLANE:ts]
    carry[...] = c8
    nb_ref[0] = c8

    @pl.when(i == pl.num_programs(1) - 1)
    def _():
        x2 = x2_ref[...]
        gt = _dot(x2, wgb[...])
        p1 = hist2_ref[:, 1, :]
        o2_ref[...] = gated(gt, p1, hist2_ref[:, 0, :], _dot(x2, wub[...]))
        nhist2_ref[:, 0, :] = p1
        nhist2_ref[:, 1, :] = gt


def _ffn_up(x, wg, wu, layer, cw, cb, buf, rows_per_seq, x2, hist2):
    m, k = x.shape
    m2 = x2.shape[0]
    f = wg.shape[2]
    tn = _pick(f, 512, LANE)
    tm = _pick(rows_per_seq, 2048, 16)
    tps = rows_per_seq // tm
    g = m // rows_per_seq
    ws = _wspec(wg, layer, k, tn, lambda j, i: j)
    bspec = pl.BlockSpec((1, SUBLANE, tn), lambda j, i: (i // tps, 0, j))
    hspec = pl.BlockSpec((m2, 2, tn), lambda j, i: (0, 0, j))
    o2spec = pl.BlockSpec((m2, tn), lambda j, i: (0, j))
    return pl.pallas_call(
        functools.partial(_ffn_up_kernel, tiles_per_seq=tps),
        out_shape=[jax.ShapeDtypeStruct((m, f), BF16), jax.ShapeDtypeStruct((g, SUBLANE, f), F32),
                   jax.ShapeDtypeStruct((m2, f), BF16), jax.ShapeDtypeStruct((m2, 2, f), F32)],
        grid=(f // tn, m // tm),
        in_specs=[pl.BlockSpec((tm, k), lambda j, i: (i, 0)), ws, ws,
                  pl.BlockSpec((3, tn), lambda j, i: (0, j)), pl.BlockSpec((1, tn), lambda j, i: (0, j)),
                  bspec, pl.BlockSpec((m2, k), lambda j, i: (0, 0)), hspec],
        out_specs=[pl.BlockSpec((tm, tn), lambda j, i: (i, j)), bspec, o2spec, hspec],
        scratch_shapes=[pltpu.VMEM((k, tn), BF16), pltpu.VMEM((k, tn), BF16), pltpu.VMEM((SUBLANE, tn), F32)],
        compiler_params=_cp(2),
        name="ffn_up",
    )(x, wg, wu, cw, cb.reshape(1, f), buf, x2, hist2)


def _trunk(xp, xs, mods_p, mods_s, st, P):
    gp, rp, d = xp.shape
    rs = xs.shape[1]
    depth = P["w_mod"].shape[0]
    keys = ("wkv", "shift", "h", "lconv", "fconv")
    out_p = {k: [] for k in keys}
    out_s = {k: [] for k in keys}
    flat = lambda t: t.reshape(-1, t.shape[-1])
    unp = lambda t: t.reshape(gp, rp, t.shape[-1])
    uns = lambda t: t.reshape(1, rs, t.shape[-1])
    hp = hs = None
    for i in range(depth):
        mp, ms = mods_p[i], mods_s[i]
        j = i // 2
        if i % 2 == 0:
            lora = tuple(P[nm][j] for nm in ("rw_w1", "rw_w2", "rw_w0", "rw_a1", "rw_a2", "rw_a0", "rw_g1", "rw_g2"))
            g0, mix = P["norm_g"][i, 0], P["rw_mix"][j]
            pre_p, shift_p = _rwkv_prep(xp, g0, mp, mix, jnp.zeros((gp, 1, d), F32), True, lora)
            pre_s, shift_s = _rwkv_prep(xs, g0, ms, mix, st["shift"][j].reshape(1, rs, d), False, lora)
            xr_p, xk_p, xv_p, w_p, a_p, g_p = pre_p
            xr_s, xk_s, xv_s, w_s, a_s, g_s = pre_s
            rr_p, rr_s = _mm(flat(xr_p), P["rw_wr"], j, x2=flat(xr_s), out_dtype=ACT_DTYPE, name="rw_r")
            kk_p, kk_s = _mm(flat(xk_p), P["rw_wk"], j, x2=flat(xk_s), out_dtype=ACT_DTYPE, name="rw_k")
            vv_p, vv_s = _mm(flat(xv_p), P["rw_wv"], j, x2=flat(xv_s), out_dtype=ACT_DTYPE, name="rw_v")
            heads = (P["rw_kk"][j], P["rw_ka"][j], P["rw_rk"][j], P["rw_lnx_g"][j], P["rw_lnx_b"][j])
            xo_p, s_bd = _wkv_chunked(unp(rr_p), unp(kk_p), unp(vv_p), w_p, a_p, g_p, *heads)
            npairs = d // LANE
            s_new = jnp.stack([s_bd[:, :, :HEAD, :HEAD], s_bd[:, :, HEAD:, HEAD:]], axis=2)
            out_p["wkv"].append(s_new.reshape(gp, 2 * npairs, HEAD, HEAD))
            out_p["shift"].append(shift_p.reshape(gp, d))
            s_t = jnp.transpose(st["wkv"][j], (1, 2, 3, 0))
            vecs = _wkv_vec(kk_s, flat(w_s), flat(a_s), rr_s, vv_s, heads[0], heads[1])
            s_t, y_s = _wkv_step(s_t, *vecs)
            out_s["wkv"].append(jnp.transpose(s_t, (3, 0, 1, 2)))
            out_s["shift"].append(shift_s.reshape(rs, d))
            xo_s = _wkv_post(y_s.T.reshape(1, rs, d), uns(rr_s), uns(kk_s), uns(vv_s), a_s, g_s, *heads[1:])
            mo_p, mo_s, w_out = xo_p, xo_s, P["rw_wo"]
        else:
            proj_p, proj_s = _mm(flat(hp), P["lru_w_in"], j, P["lru_b_in"][j], x2=flat(hs), out_dtype=ACT_DTYPE,
                                 name="lru_in")
            dr = proj_p.shape[1] // 2
            width = P["lru_conv_w"].shape[1]
            cw, cb, lam = P["lru_conv_w"][j], P["lru_conv_b"][j], P["lru_lambda"][j]
            gates = (P["lru_wa"][j], P["lru_ba"][j], P["lru_wx"][j], P["lru_bx"][j])
            hy_p, hl_p, nb8 = _lru_seq(unp(proj_p), cw, cb, *gates, lam, jnp.zeros((gp, 1, dr), F32),
                                       jnp.zeros((gp, SUBLANE, dr), F32))
            out_p["lconv"].append(nb8[:, SUBLANE - (width - 1):, :])
            out_p["h"].append(hl_p.reshape(gp, dr))
            hist = st["lru_conv"][j]
            xc = _lru_conv(uns(proj_s), cw, cb, hist.reshape(1, rs, width - 1, dr))
            out_s["lconv"].append(jnp.concatenate([hist[:, 1:], proj_s[:, None, dr:]], axis=1))
            ga, gx = _lru_gates(flat(xc), *gates)
            hy_s, hl_s = _lru_step(uns(ga), uns(gx), xc, uns(proj_s), lam, st["lru_h"][j].reshape(1, rs, dr))
            out_s["h"].append(hl_s.reshape(rs, dr))
            mo_p, mo_s, w_out = hy_p, hy_s, P["lru_w_out"]
        g1, g2 = P["norm_g"][i, 1], P["norm_g"][i, 2]
        xp, hf_p = _mm_resid(mo_p, w_out, j, xp, g1, mp, 2, g2, 4, 3, name="mix_out")
        xs, hf_s = _mm_resid(mo_s, w_out, j, xs, g1, ms, 2, g2, 4, 3, name="mix_out")
        fdim = P["ffn_w_gate"].shape[2]
        hact_p, nb8, hact_s, nhist = _ffn_up(
            flat(hf_p), P["ffn_w_gate"], P["ffn_w_up"], i, P["ffn_conv_w"][i], P["ffn_conv_b"][i],
            jnp.zeros((gp, SUBLANE, fdim), F32), rp, flat(hf_s), st["ffn_conv"][i])
        out_p["fconv"].append(nb8[:, SUBLANE - 2:, :])
        out_s["fconv"].append(nhist)
        f_p, f_s = _mm(hact_p, P["ffn_w_down"], i, x2=hact_s, out_dtype=ACT_DTYPE, tm_target=512, tn_target=512,
                       name="ffn_down")
        g3 = P["norm_g"][i, 3]
        if i + 1 < depth and (i + 1) % 2 == 1:
            nxt = lambda mod: (P["norm_g"][i + 1, 0], mod, 1, 0)
            xp, hp = _resid_norm(xp, unp(f_p), g3, mp, 5, nxt(mods_p[i + 1]), name="resid_ffn")
            xs, hs = _resid_norm(xs, uns(f_s), g3, ms, 5, nxt(mods_s[i + 1]), name="resid_ffn")
        else:
            xp = _resid_norm(xp, unp(f_p), g3, mp, 5, name="resid_ffn")
            xs = _resid_norm(xs, uns(f_s), g3, ms, 5, name="resid_ffn")
    stack = lambda o: tuple(jnp.stack(o[k]) for k in keys)
    return (xp,) + stack(out_p), (xs,) + stack(out_s)


def kernel(x_prompt, x_sample, c_prompt, c_sample, state_rwkv_wkv, state_rwkv_shift, state_lru_h,
           state_lru_conv, state_ffn_conv, w_mod, b_mod, norm_g, rw_mix, rw_wr, rw_wk, rw_wv, rw_wo,
           rw_w0, rw_w1, rw_w2, rw_a0, rw_a1, rw_a2, rw_g1, rw_g2, rw_kk, rw_ka, rw_rk, rw_lnx_g,
           rw_lnx_b, lru_w_in, lru_b_in, lru_conv_w, lru_conv_b, lru_wa, lru_ba, lru_wx, lru_bx,
           lru_lambda, lru_w_out, ffn_w_gate, ffn_w_up, ffn_w_down, ffn_conv_w, ffn_conv_b):
    P = dict(w_mod=w_mod, b_mod=b_mod, norm_g=norm_g, rw_mix=rw_mix, rw_wr=rw_wr, rw_wk=rw_wk,
             rw_wv=rw_wv, rw_wo=rw_wo, rw_w0=rw_w0, rw_w1=rw_w1, rw_w2=rw_w2, rw_a0=rw_a0, rw_a1=rw_a1,
             rw_a2=rw_a2, rw_g1=rw_g1, rw_g2=rw_g2, rw_kk=rw_kk, rw_ka=rw_ka, rw_rk=rw_rk,
             rw_lnx_g=rw_lnx_g, rw_lnx_b=rw_lnx_b, lru_w_in=lru_w_in, lru_b_in=lru_b_in,
             lru_conv_w=lru_conv_w, lru_conv_b=lru_conv_b, lru_wa=lru_wa, lru_ba=lru_ba, lru_wx=lru_wx,
             lru_bx=lru_bx, lru_lambda=lru_lambda, lru_w_out=lru_w_out, ffn_w_gate=ffn_w_gate,
             ffn_w_up=ffn_w_up, ffn_w_down=ffn_w_down, ffn_conv_w=ffn_conv_w, ffn_conv_b=ffn_conv_b)
    bp, t, d = x_prompt.shape
    bs = x_sample.shape[0]
    depth = w_mod.shape[0]
    c_p = jnp.pad(c_prompt, ((0, -bp % 16), (0, 0)))
    mods_p, mods_s = [], []
    for i in range(depth):
        mod_p, mod_s = _mm(c_p, w_mod, i, b_mod[i], x2=c_sample, pre_act="silu", name="mod")
        mods_p.append(mod_p[:bp].reshape(bp, 1, N_MOD * d))
        mods_s.append(mod_s.reshape(1, bs, N_MOD * d))
    st = dict(wkv=state_rwkv_wkv, shift=state_rwkv_shift, lru_h=state_lru_h,
              lru_conv=state_lru_conv, ffn_conv=state_ffn_conv)
    (y_p, *state_p), (y_s, *state_s) = _trunk(x_prompt, x_sample.reshape(1, bs, d), mods_p, mods_s, st, P)
    return (y_p, y_s.reshape(bs, 1, d), *state_p, *state_s)
```

```python
import functools

import jax
import jax.numpy as jnp
from jax import lax
from jax.experimental import pallas as pl
from jax.experimental.pallas import tpu as pltpu

F32 = jnp.float32
BF16 = jnp.bfloat16
ACT_DTYPE = F32

NORM_EPS = 1e-6
LNX_EPS = 64e-5
LRU_C = 8.0
N_MOD = 6
HEAD = 64
LANE = 128
SUBLANE = 8
CHUNK = 64
PAIR_UNROLL = 16
LRU_SCAN_UNROLL = 4
MM_RESID_SUB_ROWS = 128
FFN_SUB_ROWS = 128
VMEM_LIMIT = 52 * 1024 * 1024


def _cp(n_axes):
    return pltpu.CompilerParams(dimension_semantics=("arbitrary",) * n_axes,
                                vmem_limit_bytes=VMEM_LIMIT)


def _pick(n, target, mult):
    best = None
    for t in range(mult, min(n, target) + 1, mult):
        if n % t == 0:
            best = t
    return best if best is not None else n


def _softplus(z):
    return jnp.maximum(z, 0.0) + jnp.log(1.0 + jnp.exp(-jnp.abs(z)))


def _sigmoid(z):
    return 1.0 / (1.0 + jnp.exp(-z))


def _gelu(x):
    return 0.5 * x * (1.0 + jnp.tanh(0.7978845608028654 * (x + 0.044715 * (x * x * x))))


def _act(x, name):
    if name is None:
        return x
    if name == "tanh":
        return jnp.tanh(x)
    if name == "sigmoid":
        return _sigmoid(x)
    if name == "silu":
        return x * _sigmoid(x)
    raise ValueError(name)


def _rms(x, g):
    ms = jnp.mean(x * x, axis=-1, keepdims=True)
    return x * lax.rsqrt(ms + NORM_EPS) * g


def _dot(x, y):
    return jnp.dot(x, y, preferred_element_type=F32)


def _dot_nt(x, y):
    return lax.dot_general(x, y, (((1,), (1,)), ((), ())), preferred_element_type=F32)


def _split_dot_left(m, x, parts):
    acc = None
    rem = x
    for i in range(parts):
        piece = rem.astype(BF16)
        term = jnp.dot(m, piece, preferred_element_type=F32)
        acc = term if acc is None else acc + term
        if i + 1 < parts:
            rem = rem - piece.astype(F32)
    return acc


def _dot_tn(x, y):
    return lax.dot_general(x, y, (((0,), (0,)), ((), ())), preferred_element_type=F32)


def _head_sum(x):
    d = x.shape[-1]
    first = lax.broadcasted_iota(jnp.int32, (1, LANE), 1) < HEAD
    cols = []
    for s in range(0, d, LANE):
        xs = x[:, s:s + LANE]
        lo = jnp.sum(jnp.where(first, xs, 0.0), axis=-1, keepdims=True)
        hi = jnp.sum(jnp.where(first, 0.0, xs), axis=-1, keepdims=True)
        cols.append(jnp.where(first, lo, hi))
    return cols[0] if len(cols) == 1 else jnp.concatenate(cols, axis=-1)


def _shift_rows(x, carry8, s):
    n = x.shape[0]
    rolled = pltpu.roll(x, s, 0)
    rolled_c = pltpu.roll(carry8, s, 0)
    row8 = lax.broadcasted_iota(jnp.int32, (SUBLANE, x.shape[1]), 0)
    top = jnp.where(row8 < s, rolled_c, rolled[0:SUBLANE])
    if n == SUBLANE:
        return top
    return jnp.concatenate([top, rolled[SUBLANE:]], axis=0)


def _mm_kernel(*refs, has_bias, has_x2, pre_act, act):
    refs = list(refs)
    x_ref, w_ref = refs[:2]
    x2_ref = refs[2] if has_x2 else None
    b_ref = refs[2 + has_x2] if has_bias else None
    o_ref = refs[2 + has_x2 + has_bias]
    o2_ref = refs[3 + has_x2 + has_bias] if has_x2 else None
    wb_ref = refs[-1]

    @pl.when(pl.program_id(1) == 0)
    def _():
        wb_ref[...] = w_ref[...].astype(BF16)

    def project(src, dst):
        x = src[...]
        if pre_act is not None:
            x = _act(x.astype(F32), pre_act)
        acc = _dot(x.astype(BF16), wb_ref[...])
        if has_bias:
            acc = acc + b_ref[...]
        dst[...] = _act(acc, act).astype(dst.dtype)

    project(x_ref, o_ref)
    if has_x2:
        @pl.when(pl.program_id(1) == pl.num_programs(1) - 1)
        def _():
            project(x2_ref, o2_ref)


def _wspec(w, layer, rows, cols, col_of):
    assert w.ndim == 3
    return pl.BlockSpec((None, rows, cols), lambda *ids: (layer, 0, col_of(*ids)))


def _mm(x, w, layer, bias=None, *, x2=None, pre_act=None, act=None, out_dtype=F32, tm_target=1024,
        tn_target=1024, name="mm"):
    m, k = x.shape
    n = w.shape[2]
    tm = _pick(m, tm_target, 16)
    tn = _pick(n, tn_target, LANE)
    in_specs = [pl.BlockSpec((tm, k), lambda j, i: (i, 0)),
                _wspec(w, layer, k, tn, lambda j, i: j)]
    args = [x, w]
    out_shape = [jax.ShapeDtypeStruct((m, n), out_dtype)]
    out_specs = [pl.BlockSpec((tm, tn), lambda j, i: (i, j))]
    if x2 is not None:
        m2 = x2.shape[0]
        in_specs.append(pl.BlockSpec((m2, k), lambda j, i: (0, 0)))
        args.append(x2)
        out_shape.append(jax.ShapeDtypeStruct((m2, n), out_dtype))
        out_specs.append(pl.BlockSpec((m2, tn), lambda j, i: (0, j)))
    if bias is not None:
        in_specs.append(pl.BlockSpec((1, tn), lambda j, i: (0, j)))
        args.append(bias.reshape(1, n))
    res = pl.pallas_call(
        functools.partial(_mm_kernel, has_bias=bias is not None, has_x2=x2 is not None, pre_act=pre_act, act=act),
        out_shape=out_shape,
        grid=(n // tn, m // tm),
        in_specs=in_specs,
        out_specs=out_specs,
        scratch_shapes=[pltpu.VMEM((k, tn), BF16)],
        compiler_params=_cp(2),
        name=name,
    )(*args)
    return res if x2 is not None else res[0]


def _mod_spec(mod, slot, d, tr):
    rm = mod.shape[1]
    if rm == 1:
        return pl.BlockSpec((1, 1, d), lambda g, t: (g, 0, slot))
    return pl.BlockSpec((1, tr, d), lambda g, t: (g, t, slot))


def _mm_resid_kernel(*refs, with_next, cast_inside):
    refs = list(refs)
    a_ref, w_ref, x_ref, g1_ref, gate_ref = refs[:5]
    pos = 5
    if with_next:
        g2_ref, sc_ref, sh_ref = refs[pos:pos + 3]
        pos += 3
    xn_ref = refs[pos]
    h_ref = refs[pos + 1] if with_next else None
    if cast_inside:
        wb_ref = refs[-1]

        @pl.when(jnp.logical_and(pl.program_id(0) == 0, pl.program_id(1) == 0))
        def _():
            wb_ref[...] = w_ref[...].astype(BF16)
    else:
        wb_ref = w_ref

    tm = a_ref.shape[1]
    ts = _pick(tm, MM_RESID_SUB_ROWS, 16)

    def rows_of(ref, rows):
        return ref[0] if ref.shape[1] == 1 else ref[0, rows, :]

    for s0 in range(0, tm, ts):
        rows = slice(s0, s0 + ts)
        o = _dot(a_ref[0, rows, :], wb_ref[...])
        xn = x_ref[0, rows, :] + rows_of(gate_ref, rows) * _rms(o, g1_ref[...])
        xn_ref[0, rows, :] = xn
        if with_next:
            h_ref[0, rows, :] = (_rms(xn, g2_ref[...]) * (1.0 + rows_of(sc_ref, rows))
                                 + rows_of(sh_ref, rows)).astype(BF16)


def _mm_resid(a, w, layer, x, g1, mod, gate_slot, nxt, name, tm_target=512):
    g, r, k = a.shape
    n = w.shape[-1]
    cast_inside = w.ndim == 3
    tm = _pick(r, tm_target, 16)
    row = lambda c: pl.BlockSpec((1, tm, c), lambda gi, t: (gi, t, 0))
    vec = pl.BlockSpec((1, n), lambda gi, t: (0, 0))
    if cast_inside:
        wspec = pl.BlockSpec((None, k, n), lambda gi, t: (layer, 0, 0), pipeline_mode=pl.Buffered(1))
        scratch = [pltpu.VMEM((k, n), BF16)]
    else:
        wspec = pl.BlockSpec((k, n), lambda gi, t: (0, 0), pipeline_mode=pl.Buffered(1))
        scratch = []
    in_specs = [row(k), wspec, row(n), vec, _mod_spec(mod, gate_slot, n, tm)]
    args = [a, w, x, g1.reshape(1, n), mod]
    out_shape = [jax.ShapeDtypeStruct((g, r, n), F32)]
    out_specs = [row(n)]
    if nxt is not None:
        g2, mod2, sc_slot, sh_slot = nxt
        in_specs += [vec, _mod_spec(mod2, sc_slot, n, tm), _mod_spec(mod2, sh_slot, n, tm)]
        args += [g2.reshape(1, n), mod2, mod2]
        out_shape.append(jax.ShapeDtypeStruct((g, r, n), BF16))
        out_specs.append(row(n))
    res = pl.pallas_call(
        functools.partial(_mm_resid_kernel, with_next=nxt is not None, cast_inside=cast_inside),
        out_shape=out_shape,
        grid=(g, r // tm),
        in_specs=in_specs,
        out_specs=out_specs,
        scratch_shapes=scratch,
        compiler_params=_cp(2),
        name=name,
    )(*args)
    return res if nxt is not None else res[0]


def _rwkv_prep_kernel(x_ref, g_ref, sc_ref, sh_ref, mix_ref, shift_ref,
                      w1_ref, w2_ref, w0_ref, a1_ref, a2_ref, a0_ref, g1_ref, g2_ref,
                      xr_ref, xk_ref, xv_ref, wpre_ref, a_ref, gate_ref, shift_out, *scratch, seq):
    h = _rms(x_ref[0], g_ref[...]) * (1.0 + sc_ref[0]) + sh_ref[0]
    tr = h.shape[0]
    if seq:
        carry = scratch[0]
        t = pl.program_id(1)

        @pl.when(t == 0)
        def _():
            carry[...] = jnp.broadcast_to(shift_ref[0], carry.shape)

        hprev = _shift_rows(h, carry[...], 1)
        carry[...] = h[tr - SUBLANE:tr]

        @pl.when(t == pl.num_programs(1) - 1)
        def _():
            shift_out[0] = h[tr - 1:tr]
    else:
        hprev = shift_ref[0]
        shift_out[0] = h
    xx = hprev - h

    def mixed(j):
        return (h + xx * mix_ref[j:j + 1, :]).astype(BF16)

    xr_ref[0] = mixed(0)
    xk_ref[0] = mixed(2)
    xv_ref[0] = mixed(3)
    wpre_ref[0] = w0_ref[...] + _dot(jnp.tanh(_dot(mixed(1), w1_ref[...])).astype(BF16), w2_ref[...])
    a = _sigmoid(a0_ref[...] + _dot(_dot(mixed(4), a1_ref[...]).astype(BF16), a2_ref[...]))
    a_ref[0] = a.astype(a_ref.dtype)
    gate_ref[0] = _dot(_sigmoid(_dot(mixed(5), g1_ref[...])).astype(BF16), g2_ref[...]).astype(gate_ref.dtype)


def _lora_pair(w1, w2):
    r = w1.shape[1]
    rp = -(-r // LANE) * LANE
    if rp != r:
        w1 = jnp.pad(w1, ((0, 0), (0, rp - r)))
        w2 = jnp.pad(w2, ((0, rp - r), (0, 0)))
    return w1.astype(BF16), w2.astype(BF16)


def _rwkv_prep(x, g, mod, mix, shift, seq, lora):
    gg, r, d = x.shape
    tr = _pick(r, 256, 16)
    w1, w2, w0, a1, a2, a0, g1, g2 = lora
    w1, w2 = _lora_pair(w1, w2)
    a1, a2 = _lora_pair(a1, a2)
    g1, g2 = _lora_pair(g1, g2)
    row = pl.BlockSpec((1, tr, d), lambda gi, t: (gi, t, 0))
    vec = pl.BlockSpec((1, d), lambda gi, t: (0, 0))
    full = lambda w: pl.BlockSpec(w.shape, lambda gi, t: (0, 0))
    if seq:
        shift_spec = pl.BlockSpec((1, 1, d), lambda gi, t: (gi, 0, 0))
        shift_shape = (gg, 1, d)
        scratch = [pltpu.VMEM((SUBLANE, d), F32)]
    else:
        shift_spec = row
        shift_shape = (gg, r, d)
        scratch = []
    act = lambda dt: jax.ShapeDtypeStruct((gg, r, d), dt)
    outs = pl.pallas_call(
        functools.partial(_rwkv_prep_kernel, seq=seq),
        out_shape=[act(BF16)] * 3 + [act(F32), act(BF16), act(BF16)] + [jax.ShapeDtypeStruct(shift_shape, F32)],
        grid=(gg, r // tr),
        in_specs=[row, vec, _mod_spec(mod, 1, d, tr), _mod_spec(mod, 0, d, tr),
                  pl.BlockSpec((6, d), lambda gi, t: (0, 0)), shift_spec,
                  full(w1), full(w2), vec, full(a1), full(a2), vec, full(g1), full(g2)],
        out_specs=[row] * 6 + [shift_spec],
        scratch_shapes=scratch,
        compiler_params=_cp(2),
        name="rwkv_prep",
    )(x, g.reshape(1, d), mod, mod, mix, shift, w1, w2, w0.reshape(1, d), a1, a2, a0.reshape(1, d), g1, g2)
    return outs[:6], outs[6]


def _wkv_terms(k, wpre, a, kkp, kap):
    logd = -0.6065306597126334 * _sigmoid(wpre)
    kk = k * kkp
    kk = kk * lax.rsqrt(jnp.maximum(_head_sum(kk * kk), 1e-24))
    kmod = k * (1.0 + (a - 1.0) * kap)
    return logd, kk, kmod


def _wkv_readout(y, bonus, gate, lnx_g, lnx_b):
    inv_n = 1.0 / HEAD
    yc = y - _head_sum(y) * inv_n
    var = _head_sum(yc * yc) * inv_n
    return ((yc * lax.rsqrt(var + LNX_EPS) * lnx_g + lnx_b + bonus) * gate).astype(BF16)


def _wkv_chunk_kernel(r_ref, k_ref, v_ref, w_ref, a_ref, gate_ref, kkp_ref, kap_ref, rk_ref, lg_ref, lb_ref,
                      xo_ref, sout_ref, S, Rt, At, Bt, Kt, Vs, Gc, Ys, Bonus):
    c = pl.program_id(1)
    n_pairs = S.shape[0]
    C = r_ref.shape[1]

    @pl.when(c == 0)
    def _():
        S[...] = jnp.zeros(S.shape, F32)

    a = a_ref[0].astype(F32)
    r = r_ref[0].astype(F32)
    logd, kk, kmod = _wkv_terms(k_ref[0].astype(F32), w_ref[0], a, kkp_ref[...], kap_ref[...])
    ri = lax.broadcasted_iota(jnp.int32, (C, C), 0)
    ci = lax.broadcasted_iota(jnp.int32, (C, C), 1)
    tri = (ri >= ci).astype(BF16)
    cum = _split_dot_left(tri, logd, 3)
    gam = jnp.exp(cum)
    rt = r * gam
    at = -kk * jnp.exp(cum - logd)
    ginv = jnp.exp(-cum)
    bt = kk * a * ginv
    kt = kmod * ginv
    v = v_ref[0].astype(F32)
    Bonus[...] = _head_sum(r * kmod * rk_ref[...]) * v
    for p in range(n_pairs):
        sl = slice(p * LANE, (p + 1) * LANE)
        Rt[p] = rt[:, sl]
        At[p] = at[:, sl]
        Bt[p] = bt[:, sl]
        Kt[p] = kt[:, sl]
        Vs[p] = v[:, sl]
        Gc[p] = jnp.broadcast_to(gam[C - 1:C, sl], (SUBLANE, LANE))

    lane = lax.broadcasted_iota(jnp.int32, (1, LANE), 1)
    first = lane < HEAD
    r2 = lax.broadcasted_iota(jnp.int32, (2 * C, 2 * C), 0)
    c2 = lax.broadcasted_iota(jnp.int32, (2 * C, 2 * C), 1)
    same_blk = (r2 < C) == (c2 < C)
    strict = jnp.logical_and(r2 > c2, same_blk)
    incl = jnp.logical_and(r2 >= c2, same_blk)
    rs = lax.broadcasted_iota(jnp.int32, (LANE, LANE), 0)
    cs = lax.broadcasted_iota(jnp.int32, (LANE, LANE), 1)
    same_head = (rs < HEAD) == (cs < HEAD)
    zero_blk = jnp.zeros((2 * C, LANE), BF16)

    def stack(x):
        return jnp.concatenate([jnp.where(first, x, 0.0), jnp.where(first, 0.0, x)], axis=0).astype(BF16)

    def twice(x):
        xb = x.astype(BF16)
        return jnp.concatenate([xb, xb], axis=0)

    n_fac = max(1, (C - 1).bit_length())

    def pairs_step(ps):
        n = range(len(ps))
        rp, ap, bp = [Rt[p] for p in ps], [At[p] for p in ps], [Bt[p] for p in ps]
        kp, vp = [Kt[p] for p in ps], [Vs[p] for p in ps]
        sp, gc = [S[p] for p in ps], [Gc[p] for p in ps]
        am, rm = [stack(t) for t in ap], [stack(t) for t in rp]
        bm, km = [twice(t) for t in bp], [twice(t) for t in kp]
        ar = [jnp.concatenate([am[i], rm[i]], axis=0) for i in n]
        gb = [_dot_nt(ar[i], bm[i]) for i in n]
        gk = [_dot_nt(ar[i], km[i]) for i in n]
        lp = [jnp.where(strict, t[:2 * C], 0.0) for t in gb]
        lak = [jnp.where(strict, t[:2 * C], 0.0).astype(BF16) for t in gk]
        grbk = [jnp.concatenate([jnp.where(incl, gb[i][2 * C:], 0.0), jnp.where(incl, gk[i][2 * C:], 0.0)],
                                axis=1).astype(BF16) for i in n]
        spb = [t.astype(BF16) for t in sp]
        ars = [_dot_nt(jnp.concatenate([ap[i], rp[i]], axis=0).astype(BF16), spb[i]) for i in n]
        a_s, r_s = [t[:C] for t in ars], [t[C:] for t in ars]
        vst = [jnp.concatenate([t, t], axis=0).astype(BF16) for t in vp]
        x = [jnp.concatenate([a_s[i], a_s[i]], axis=0) + _dot(lak[i], vst[i]) for i in n]
        for f in range(n_fac):
            lh = [t.astype(BF16) for t in lp]
            xh = [t.astype(BF16) for t in x]
            xl = [(x[i] - xh[i].astype(F32)).astype(BF16) for i in n]
            lhs = [jnp.concatenate([t, t], axis=1) for t in lh]
            if f + 1 < n_fac:
                rhs = [jnp.concatenate([jnp.concatenate([xh[i], lh[i]], axis=1),
                                        jnp.concatenate([xl[i], zero_blk], axis=1)], axis=0) for i in n]
                prod = [_dot(lhs[i], rhs[i]) for i in n]
                x = [x[i] + prod[i][:, :LANE] for i in n]
                lp = [t[:, LANE:] for t in prod]
            else:
                x = [x[i] + _dot(lhs[i], jnp.concatenate([xh[i], xl[i]], axis=0)) for i in n]
        yst = [jnp.concatenate([r_s[i], r_s[i]], axis=0)
               + _dot(grbk[i], jnp.concatenate([x[i].astype(BF16), vst[i]], axis=0)) for i in n]
        u = [jnp.where(first, t[:C], t[C:]) for t in x]
        ds = [_dot_tn(jnp.concatenate([u[i], vp[i]], axis=0).astype(BF16),
                      jnp.concatenate([bp[i], kp[i]], axis=0).astype(BF16)) for i in n]
        for i, p in enumerate(ps):
            Ys[p] = jnp.where(first, yst[i][:C], yst[i][C:])
            S[p] = jnp.where(same_head, (sp[i] + ds[i]) * gc[i][0:1, :], 0.0)

    for p0 in range(0, n_pairs, PAIR_UNROLL):
        pairs_step(list(range(p0, min(p0 + PAIR_UNROLL, n_pairs))))

    y = jnp.concatenate([Ys[p] for p in range(n_pairs)], axis=-1) if n_pairs > 1 else Ys[0]
    xo_ref[0] = _wkv_readout(y, Bonus[...], gate_ref[0].astype(F32), lg_ref[...], lb_ref[...])

    @pl.when(c == pl.num_programs(1) - 1)
    def _():
        sout_ref[0] = S[...]


def _wkv_chunked(r, k, v, wpre, a, gate, kkp, kap, rk, lnx_g, lnx_b):
    g, t, d = r.shape
    C = CHUNK
    np_ = d // LANE
    row = pl.BlockSpec((1, C, d), lambda gi, ci: (gi, ci, 0))
    vec = pl.BlockSpec((1, d), lambda gi, ci: (0, 0))
    pair_buf = pltpu.VMEM((np_, C, LANE), F32)
    return pl.pallas_call(
        _wkv_chunk_kernel,
        out_shape=[jax.ShapeDtypeStruct((g, t, d), BF16),
                   jax.ShapeDtypeStruct((g, np_, LANE, LANE), F32)],
        grid=(g, t // C),
        in_specs=[row] * 6 + [vec] * 5,
        out_specs=[row, pl.BlockSpec((1, np_, LANE, LANE), lambda gi, ci: (gi, 0, 0, 0))],
        scratch_shapes=[pltpu.VMEM((np_, LANE, LANE), F32), pair_buf, pair_buf, pair_buf, pair_buf, pair_buf,
                        pltpu.VMEM((np_, SUBLANE, LANE), F32), pair_buf, pltpu.VMEM((C, d), F32)],
        compiler_params=_cp(2),
        name="wkv_chunk",
    )(r, k, v, wpre, a, gate, *[p.reshape(1, d) for p in (kkp, kap, rk, lnx_g, lnx_b)])


def _wkv_vec_kernel(k_ref, w_ref, a_ref, r_ref, v_ref, kkp_ref, kap_ref,
                    d_ref, an_ref, b_ref, km_ref, rt_ref, vt_ref):
    a = a_ref[...].astype(F32)
    logd, kk, kmod = _wkv_terms(k_ref[...].astype(F32), w_ref[...], a, kkp_ref[...], kap_ref[...])
    d_ref[...] = jnp.exp(logd).T
    an_ref[...] = (-kk).T
    b_ref[...] = (kk * a).T
    km_ref[...] = kmod.T
    rt_ref[...] = r_ref[...].astype(F32).T
    vt_ref[...] = v_ref[...].astype(F32).T


def _wkv_vec(k, wpre, a, r, v, kkp, kap):
    b, d = k.shape
    full = pl.BlockSpec((b, d), lambda i: (0, 0))
    fullt = pl.BlockSpec((d, b), lambda i: (0, 0))
    vec = pl.BlockSpec((1, d), lambda i: (0, 0))
    return pl.pallas_call(
        _wkv_vec_kernel,
        out_shape=[jax.ShapeDtypeStruct((d, b), F32)] * 6,
        grid=(1,),
        in_specs=[full] * 5 + [vec, vec],
        out_specs=[fullt] * 6,
        compiler_params=_cp(1),
        name="wkv_vec",
    )(k, wpre, a, r, v, kkp.reshape(1, d), kap.reshape(1, d))


def _wkv_step_kernel(s_ref, d_ref, a_ref, b_ref, k_ref, r_ref, v_ref, sn_ref, y_ref):
    s = s_ref[...]
    key = lambda ref: ref[...][:, None]
    sa = jnp.sum(s * key(a_ref), axis=2, keepdims=True)
    sn = s * key(d_ref) + sa * key(b_ref) + v_ref[...] * key(k_ref)
    sn_ref[...] = sn
    y_ref[...] = jnp.sum(sn * key(r_ref), axis=2, keepdims=True)


def _wkv_step(s, dvec, an, bvec, kmod, r, v):
    h, n, _, b = s.shape
    hb = _pick(h, 2, 1)
    sblk = pl.BlockSpec((hb, n, n, b), lambda i: (i, 0, 0, 0))
    kblk = pl.BlockSpec((hb, n, b), lambda i: (i, 0, 0))
    vblk = pl.BlockSpec((hb, n, 1, b), lambda i: (i, 0, 0, 0))
    kv = lambda x: x.reshape(h, n, b)
    sn, y = pl.pallas_call(
        _wkv_step_kernel,
        out_shape=[jax.ShapeDtypeStruct((h, n, n, b), F32), jax.ShapeDtypeStruct((h, n, 1, b), F32)],
        grid=(h // hb,),
        in_specs=[sblk] + [kblk] * 5 + [vblk],
        out_specs=[sblk, vblk],
        compiler_params=_cp(1),
        name="wkv_step",
    )(s, kv(dvec), kv(an), kv(bvec), kv(kmod), kv(r), v.reshape(h, n, 1, b))
    return sn, y.reshape(h * n, b)


def _wkv_post_kernel(y_ref, r_ref, k_ref, v_ref, a_ref, g_ref, kap_ref, rk_ref, lg_ref, lb_ref, o_ref):
    f32 = lambda ref: ref[0].astype(F32)
    kmod = f32(k_ref) * (1.0 + (f32(a_ref) - 1.0) * kap_ref[...])
    bonus = _head_sum(f32(r_ref) * kmod * rk_ref[...]) * f32(v_ref)
    o_ref[0] = _wkv_readout(y_ref[0], bonus, f32(g_ref), lg_ref[...], lb_ref[...])


def _wkv_post(y, r, k, v, a, g, kap, rk, lnx_g, lnx_b):
    gg, t, d = y.shape
    tr = _pick(t, 256, 16)
    row = pl.BlockSpec((1, tr, d), lambda gi, ti: (gi, ti, 0))
    vec = pl.BlockSpec((1, d), lambda gi, ti: (0, 0))
    return pl.pallas_call(
        _wkv_post_kernel,
        out_shape=jax.ShapeDtypeStruct((gg, t, d), BF16),
        grid=(gg, t // tr),
        in_specs=[row] * 6 + [vec] * 4,
        out_specs=row,
        compiler_params=_cp(2),
        name="wkv_post",
    )(y, r, k, v, a, g, kap.reshape(1, d), rk.reshape(1, d), lnx_g.reshape(1, d), lnx_b.reshape(1, d))


def _lru_conv_kernel(x_ref, w_ref, b_ref, buf_ref, xc_ref, *, width):
    acc = b_ref[...] + x_ref[0].astype(F32) * w_ref[width - 1:width, :]
    for s in range(1, width):
        acc = acc + buf_ref[0, :, width - 1 - s, :] * w_ref[width - 1 - s:width - s, :]
    xc_ref[0] = acc


def _lru_conv(proj, w, b, buf):
    g, r, d2 = proj.shape
    d = d2 // 2
    width = w.shape[0]
    tr = _pick(r, 256, 16)
    row = pl.BlockSpec((1, tr, d), lambda gi, t: (gi, t, 0))
    return pl.pallas_call(
        functools.partial(_lru_conv_kernel, width=width),
        out_shape=jax.ShapeDtypeStruct((g, r, d), F32),
        grid=(g, r // tr),
        in_specs=[pl.BlockSpec((1, tr, d), lambda gi, t: (gi, t, 1)),
                  pl.BlockSpec((width, d), lambda gi, t: (0, 0)),
                  pl.BlockSpec((1, d), lambda gi, t: (0, 0)),
                  pl.BlockSpec((1, tr, width - 1, d), lambda gi, t: (gi, t, 0, 0))],
        out_specs=row,
        compiler_params=_cp(2),
        name="lru_conv",
    )(proj, w, b.reshape(1, d), buf)


def _lru_gates_kernel(x_ref, wa_ref, ba_ref, wx_ref, bx_ref, ga_ref, gx_ref):
    x = x_ref[...].astype(BF16)
    ga_ref[...] = _sigmoid(_dot(x, wa_ref[0].astype(BF16)) + ba_ref[...])
    gx_ref[...] = _sigmoid(_dot(x, wx_ref[0].astype(BF16)) + bx_ref[...])


def _lru_gates(xc, wa, ba, wx, bx):
    m, d = xc.shape
    nb, bw, _ = wa.shape
    tm = _pick(m, 1024, 8)
    xs = pl.BlockSpec((tm, bw), lambda n, i: (i, n))
    ws = pl.BlockSpec((1, bw, bw), lambda n, i: (n, 0, 0))
    bs = pl.BlockSpec((1, bw), lambda n, i: (0, n))
    return pl.pallas_call(
        _lru_gates_kernel,
        out_shape=[jax.ShapeDtypeStruct((m, d), F32)] * 2,
        grid=(nb, m // tm),
        in_specs=[xs, ws, bs, ws, bs],
        out_specs=[xs, xs],
        compiler_params=_cp(2),
        name="lru_gates",
    )(xc, wa, ba.reshape(1, d), wx, bx.reshape(1, d))


def _lru_coeffs(ga, gx, xc, lam):
    log_a = -LRU_C * ga * _softplus(-lam)
    a = jnp.exp(log_a)
    mult = jnp.sqrt(1.0 - a * a)
    return a, mult


def _lru_seq_kernel(xin_ref, y_ref, cw_ref, cb_ref, wa_ref, ba_ref, wx_ref, bx_ref, lam_ref, h0_ref, buf_ref,
                    o_ref, hl_ref, nbuf_ref, cconv, carry, abuf, bbuf, hbuf, *, width):
    t = pl.program_id(2)
    tr = xin_ref.shape[1]

    @pl.when(t == 0)
    def _():
        cconv[...] = buf_ref[0]
        carry[...] = jnp.broadcast_to(h0_ref[0], carry.shape)

    x = xin_ref[0].astype(F32)
    c8 = cconv[...]
    xc = cb_ref[...] + x * cw_ref[width - 1:width, :]
    for s in range(1, width):
        xc = xc + _shift_rows(x, c8, s) * cw_ref[width - 1 - s:width - s, :]
    cconv[...] = x[tr - SUBLANE:tr]
    nbuf_ref[0] = x[tr - SUBLANE:tr]
    xb = xc.astype(BF16)
    ga = _sigmoid(_dot(xb, wa_ref[...].astype(BF16)) + ba_ref[...])
    gx = _sigmoid(_dot(xb, wx_ref[...].astype(BF16)) + bx_ref[...])
    a, mult = _lru_coeffs(ga, gx, xc, lam_ref[...])
    row = lax.broadcasted_iota(jnp.int32, a.shape, 0)
    mult = jnp.where(jnp.logical_and(t == 0, row == 0), 1.0, mult)
    abuf[...] = a
    bbuf[...] = mult * gx * xc
    r8 = lax.broadcasted_iota(jnp.int32, (SUBLANE, a.shape[1]), 0)

    def body(i, c):
        off = pl.multiple_of(i * SUBLANE, SUBLANE)
        aa = abuf[pl.ds(off, SUBLANE), :]
        bb = bbuf[pl.ds(off, SUBLANE), :]
        for s in (1, 2, 4):
            a_s = jnp.where(r8 >= s, pltpu.roll(aa, s, 0), 1.0)
            b_s = jnp.where(r8 >= s, pltpu.roll(bb, s, 0), 0.0)
            bb = aa * b_s + bb
            aa = aa * a_s
        h = aa * c + bb
        hbuf[pl.ds(off, SUBLANE), :] = h
        return h[SUBLANE - 1:SUBLANE, :]

    c = lax.fori_loop(0, tr // SUBLANE, body, carry[0:1, :], unroll=LRU_SCAN_UNROLL)
    carry[...] = jnp.broadcast_to(c, carry.shape)
    o_ref[0] = (hbuf[...] * _gelu(y_ref[0].astype(F32))).astype(BF16)

    @pl.when(t == pl.num_programs(2) - 1)
    def _():
        hl_ref[0] = c


def _lru_seq(proj, cw, cb, wa, ba, wx, bx, lam, h0, buf):
    g, r, d2 = proj.shape
    d = d2 // 2
    nb, bw, _ = wa.shape
    width = cw.shape[0]
    tr = _pick(r, 512, 16)
    row = lambda off: pl.BlockSpec((1, tr, bw), lambda gi, n, t: (gi, t, n + off))
    vec = pl.BlockSpec((1, bw), lambda gi, n, t: (0, n))
    wsp = pl.BlockSpec((None, bw, bw), lambda gi, n, t: (n, 0, 0))
    st = pl.BlockSpec((1, 1, bw), lambda gi, n, t: (gi, 0, n))
    st8 = pl.BlockSpec((1, SUBLANE, bw), lambda gi, n, t: (gi, 0, n))
    tile = pltpu.VMEM((tr, bw), F32)
    small = pltpu.VMEM((SUBLANE, bw), F32)
    return pl.pallas_call(
        functools.partial(_lru_seq_kernel, width=width),
        out_shape=[jax.ShapeDtypeStruct((g, r, d), BF16), jax.ShapeDtypeStruct((g, 1, d), F32),
                   jax.ShapeDtypeStruct((g, SUBLANE, d), F32)],
        grid=(g, nb, r // tr),
        in_specs=[row(nb), row(0), pl.BlockSpec((width, bw), lambda gi, n, t: (0, n)), vec,
                  wsp, vec, wsp, vec, vec, st, st8],
        out_specs=[row(0), st, st8],
        scratch_shapes=[small, small, tile, tile, tile],
        compiler_params=_cp(3),
        name="lru_seq",
    )(proj, proj, cw, cb.reshape(1, d), wa, ba.reshape(1, d), wx, bx.reshape(1, d), lam.reshape(1, d), h0, buf)


def _lru_step_kernel(ga_ref, gx_ref, xc_ref, y_ref, lam_ref, h0_ref, o_ref, h_ref):
    xc = xc_ref[0]
    a, mult = _lru_coeffs(ga_ref[0], gx_ref[0], xc, lam_ref[...])
    h = a * h0_ref[0] + mult * gx_ref[0] * xc
    h_ref[0] = h
    o_ref[0] = (h * _gelu(y_ref[0].astype(F32))).astype(BF16)


def _lru_step(ga, gx, xc, proj, lam, h0):
    g, r, d = ga.shape
    row = pl.BlockSpec((1, r, d), lambda i: (0, 0, 0))
    return pl.pallas_call(
        _lru_step_kernel,
        out_shape=[jax.ShapeDtypeStruct((g, r, d), BF16), jax.ShapeDtypeStruct((g, r, d), F32)],
        grid=(1,),
        in_specs=[row, row, row, row, pl.BlockSpec((1, d), lambda i: (0, 0)), row],
        out_specs=[row, row],
        compiler_params=_cp(1),
        name="lru_step",
    )(ga, gx, xc, proj, lam.reshape(1, d), h0)


def _ffn_up_kernel(x_ref, wg_ref, wu_ref, cw_ref, cb_ref, buf_ref, x2_ref, hist2_ref,
                   o_ref, nb_ref, o2_ref, nhist2_ref, wgb, wub, carry, *, tiles_per_seq):
    i = pl.program_id(1)

    @pl.when(i == 0)
    def _():
        wgb[...] = wg_ref[...].astype(BF16)
        wub[...] = wu_ref[...].astype(BF16)

    def gated(gt, p1, p2, u):
        gc = cb_ref[...] + p2 * cw_ref[0:1, :] + p1 * cw_ref[1:2, :] + gt * cw_ref[2:3, :]
        return (_gelu(gc) * u).astype(BF16)

    tm = x_ref.shape[0]
    ts = _pick(tm, FFN_SUB_ROWS, 16)

    @pl.when(i % tiles_per_seq == 0)
    def _():
        carry[...] = buf_ref[0]

    c8 = carry[...]
    for s0 in range(0, tm, ts):
        rows = slice(s0, s0 + ts)
        x = x_ref[rows, :]
        gt = _dot(x, wgb[...])
        u = _dot(x, wub[...])
        o_ref[rows, :] = gated(gt, _shift_rows(gt, c8, 1), _shift_rows(gt, c8, 2), u)
        c8 = gt[ts - SUBLANE:ts]
    carry[...] = c8
    nb_ref[0] = c8

    @pl.when(i == pl.num_programs(1) - 1)
    def _():
        x2 = x2_ref[...]
        gt = _dot(x2, wgb[...])
        p1 = hist2_ref[:, 1, :]
        o2_ref[...] = gated(gt, p1, hist2_ref[:, 0, :], _dot(x2, wub[...]))
        nhist2_ref[:, 0, :] = p1
        nhist2_ref[:, 1, :] = gt


def _ffn_up(x, wg, wu, layer, cw, cb, buf, rows_per_seq, x2, hist2):
    m, k = x.shape
    m2 = x2.shape[0]
    f = wg.shape[2]
    tn = _pick(f, 512, LANE)
    tm = _pick(rows_per_seq, 2048, 16)
    tps = rows_per_seq // tm
    g = m // rows_per_seq
    ws = _wspec(wg, layer, k, tn, lambda j, i: j)
    bspec = pl.BlockSpec((1, SUBLANE, tn), lambda j, i: (i // tps, 0, j))
    hspec = pl.BlockSpec((m2, 2, tn), lambda j, i: (0, 0, j))
    o2spec = pl.BlockSpec((m2, tn), lambda j, i: (0, j))
    return pl.pallas_call(
        functools.partial(_ffn_up_kernel, tiles_per_seq=tps),
        out_shape=[jax.ShapeDtypeStruct((m, f), BF16), jax.ShapeDtypeStruct((g, SUBLANE, f), F32),
                   jax.ShapeDtypeStruct((m2, f), BF16), jax.ShapeDtypeStruct((m2, 2, f), F32)],
        grid=(f // tn, m // tm),
        in_specs=[pl.BlockSpec((tm, k), lambda j, i: (i, 0)), ws, ws,
                  pl.BlockSpec((3, tn), lambda j, i: (0, j)), pl.BlockSpec((1, tn), lambda j, i: (0, j)),
                  bspec, pl.BlockSpec((m2, k), lambda j, i: (0, 0)), hspec],
        out_specs=[pl.BlockSpec((tm, tn), lambda j, i: (i, j)), bspec, o2spec, hspec],
        scratch_shapes=[pltpu.VMEM((k, tn), BF16), pltpu.VMEM((k, tn), BF16), pltpu.VMEM((SUBLANE, tn), F32)],
        compiler_params=_cp(2),
        name="ffn_up",
    )(x, wg, wu, cw, cb.reshape(1, f), buf, x2, hist2)


def _trunk(xp, xs, mods_p, mods_s, st, P):
    gp, rp, d = xp.shape
    rs = xs.shape[1]
    depth = P["w_mod"].shape[0]
    keys = ("wkv", "shift", "h", "lconv", "fconv")
    out_p = {k: [] for k in keys}
    out_s = {k: [] for k in keys}
    flat = lambda t: t.reshape(-1, t.shape[-1])
    unp = lambda t: t.reshape(gp, rp, t.shape[-1])
    uns = lambda t: t.reshape(1, rs, t.shape[-1])
    hp = hs = None
    for i in range(depth):
        mp, ms = mods_p[i], mods_s[i]
        j = i // 2
        if i % 2 == 0:
            lora = tuple(P[nm][j] for nm in ("rw_w1", "rw_w2", "rw_w0", "rw_a1", "rw_a2", "rw_a0", "rw_g1", "rw_g2"))
            g0, mix = P["norm_g"][i, 0], P["rw_mix"][j]
            pre_p, shift_p = _rwkv_prep(xp, g0, mp, mix, jnp.zeros((gp, 1, d), F32), True, lora)
            pre_s, shift_s = _rwkv_prep(xs, g0, ms, mix, st["shift"][j].reshape(1, rs, d), False, lora)
            xr_p, xk_p, xv_p, w_p, a_p, g_p = pre_p
            xr_s, xk_s, xv_s, w_s, a_s, g_s = pre_s
            rr_p, rr_s = _mm(flat(xr_p), P["rw_wr"], j, x2=flat(xr_s), out_dtype=ACT_DTYPE, name="rw_r")
            kk_p, kk_s = _mm(flat(xk_p), P["rw_wk"], j, x2=flat(xk_s), out_dtype=ACT_DTYPE, name="rw_k")
            vv_p, vv_s = _mm(flat(xv_p), P["rw_wv"], j, x2=flat(xv_s), out_dtype=ACT_DTYPE, name="rw_v")
            heads = (P["rw_kk"][j], P["rw_ka"][j], P["rw_rk"][j], P["rw_lnx_g"][j], P["rw_lnx_b"][j])
            xo_p, s_bd = _wkv_chunked(unp(rr_p), unp(kk_p), unp(vv_p), w_p, a_p, g_p, *heads)
            npairs = d // LANE
            s_new = jnp.stack([s_bd[:, :, :HEAD, :HEAD], s_bd[:, :, HEAD:, HEAD:]], axis=2)
            out_p["wkv"].append(s_new.reshape(gp, 2 * npairs, HEAD, HEAD))
            out_p["shift"].append(shift_p.reshape(gp, d))
            s_t = jnp.transpose(st["wkv"][j], (1, 2, 3, 0))
            vecs = _wkv_vec(kk_s, flat(w_s), flat(a_s), rr_s, vv_s, heads[0], heads[1])
            s_t, y_s = _wkv_step(s_t, *vecs)
            out_s["wkv"].append(jnp.transpose(s_t, (3, 0, 1, 2)))
            out_s["shift"].append(shift_s.reshape(rs, d))
            xo_s = _wkv_post(y_s.T.reshape(1, rs, d), uns(rr_s), uns(kk_s), uns(vv_s), a_s, g_s, *heads[1:])
            mo_p, mo_s, w_out = xo_p, xo_s, P["rw_wo"]
        else:
            proj_p, proj_s = _mm(flat(hp), P["lru_w_in"], j, P["lru_b_in"][j], x2=flat(hs), out_dtype=ACT_DTYPE,
                                 name="lru_in")
            dr = proj_p.shape[1] // 2
            width = P["lru_conv_w"].shape[1]
            cw, cb, lam = P["lru_conv_w"][j], P["lru_conv_b"][j], P["lru_lambda"][j]
            gates = (P["lru_wa"][j], P["lru_ba"][j], P["lru_wx"][j], P["lru_bx"][j])
            hy_p, hl_p, nb8 = _lru_seq(unp(proj_p), cw, cb, *gates, lam, jnp.zeros((gp, 1, dr), F32),
                                       jnp.zeros((gp, SUBLANE, dr), F32))
            out_p["lconv"].append(nb8[:, SUBLANE - (width - 1):, :])
            out_p["h"].append(hl_p.reshape(gp, dr))
            hist = st["lru_conv"][j]
            xc = _lru_conv(uns(proj_s), cw, cb, hist.reshape(1, rs, width - 1, dr))
            out_s["lconv"].append(jnp.concatenate([hist[:, 1:], proj_s[:, None, dr:]], axis=1))
            ga, gx = _lru_gates(flat(xc), *gates)
            hy_s, hl_s = _lru_step(uns(ga), uns(gx), xc, uns(proj_s), lam, st["lru_h"][j].reshape(1, rs, dr))
            out_s["h"].append(hl_s.reshape(rs, dr))
            mo_p, mo_s, w_out = hy_p, hy_s, P["lru_w_out"]
        g1, g2 = P["norm_g"][i, 1], P["norm_g"][i, 2]
        xp, hf_p = _mm_resid(mo_p, w_out, j, xp, g1, mp, 2, (g2, mp, 4, 3), name="mix_out")
        xs, hf_s = _mm_resid(mo_s, w_out, j, xs, g1, ms, 2, (g2, ms, 4, 3), name="mix_out")
        fdim = P["ffn_w_gate"].shape[2]
        hact_p, nb8, hact_s, nhist = _ffn_up(
            flat(hf_p), P["ffn_w_gate"], P["ffn_w_up"], i, P["ffn_conv_w"][i], P["ffn_conv_b"][i],
            jnp.zeros((gp, SUBLANE, fdim), F32), rp, flat(hf_s), st["ffn_conv"][i])
        out_p["fconv"].append(nb8[:, SUBLANE - 2:, :])
        out_s["fconv"].append(nhist)
        wd = P["ffn_w_down"][i].astype(BF16)
        g3 = P["norm_g"][i, 3]
        if i + 1 < depth and (i + 1) % 2 == 1:
            nxt = lambda mod: (P["norm_g"][i + 1, 0], mod, 1, 0)
            xp, hp = _mm_resid(unp(hact_p), wd, 0, xp, g3, mp, 5, nxt(mods_p[i + 1]), name="ffn_down",
                               tm_target=256)
            xs, hs = _mm_resid(uns(hact_s), wd, 0, xs, g3, ms, 5, nxt(mods_s[i + 1]), name="ffn_down")
        else:
            xp = _mm_resid(unp(hact_p), wd, 0, xp, g3, mp, 5, None, name="ffn_down", tm_target=256)
            xs = _mm_resid(uns(hact_s), wd, 0, xs, g3, ms, 5, None, name="ffn_down")
    stack = lambda o: tuple(jnp.stack(o[k]) for k in keys)
    return (xp,) + stack(out_p), (xs,) + stack(out_s)


def kernel(x_prompt, x_sample, c_prompt, c_sample, state_rwkv_wkv, state_rwkv_shift, state_lru_h,
           state_lru_conv, state_ffn_conv, w_mod, b_mod, norm_g, rw_mix, rw_wr, rw_wk, rw_wv, rw_wo,
           rw_w0, rw_w1, rw_w2, rw_a0, rw_a1, rw_a2, rw_g1, rw_g2, rw_kk, rw_ka, rw_rk, rw_lnx_g,
           rw_lnx_b, lru_w_in, lru_b_in, lru_conv_w, lru_conv_b, lru_wa, lru_ba, lru_wx, lru_bx,
           lru_lambda, lru_w_out, ffn_w_gate, ffn_w_up, ffn_w_down, ffn_conv_w, ffn_conv_b):
    P = dict(w_mod=w_mod, b_mod=b_mod, norm_g=norm_g, rw_mix=rw_mix, rw_wr=rw_wr, rw_wk=rw_wk,
             rw_wv=rw_wv, rw_wo=rw_wo, rw_w0=rw_w0, rw_w1=rw_w1, rw_w2=rw_w2, rw_a0=rw_a0, rw_a1=rw_a1,
             rw_a2=rw_a2, rw_g1=rw_g1, rw_g2=rw_g2, rw_kk=rw_kk, rw_ka=rw_ka, rw_rk=rw_rk,
             rw_lnx_g=rw_lnx_g, rw_lnx_b=rw_lnx_b, lru_w_in=lru_w_in, lru_b_in=lru_b_in,
             lru_conv_w=lru_conv_w, lru_conv_b=lru_conv_b, lru_wa=lru_wa, lru_ba=lru_ba, lru_wx=lru_wx,
             lru_bx=lru_bx, lru_lambda=lru_lambda, lru_w_out=lru_w_out, ffn_w_gate=ffn_w_gate,
             ffn_w_up=ffn_w_up, ffn_w_down=ffn_w_down, ffn_conv_w=ffn_conv_w, ffn_conv_b=ffn_conv_b)
    bp, t, d = x_prompt.shape
    bs = x_sample.shape[0]
    depth = w_mod.shape[0]
    c_p = jnp.pad(c_prompt, ((0, -bp % 16), (0, 0)))
    mods_p, mods_s = [], []
    for i in range(depth):
        mod_p, mod_s = _mm(c_p, w_mod, i, b_mod[i], x2=c_sample, pre_act="silu", name="mod")
        mods_p.append(mod_p[:bp].reshape(bp, 1, N_MOD * d))
        mods_s.append(mod_s.reshape(1, bs, N_MOD * d))
    st = dict(wkv=state_rwkv_wkv, shift=state_rwkv_shift, lru_h=state_lru_h,
              lru_conv=state_lru_conv, ffn_conv=state_ffn_conv)
    (y_p, *state_p), (y_s, *state_s) = _trunk(x_prompt, x_sample.reshape(1, bs, d), mods_p, mods_s, st, P)
    return (y_p, y_s.reshape(bs, 1, d), *state_p, *state_s)
```

```python
import functools

import jax
import jax.numpy as jnp
from jax import lax
from jax.experimental import pallas as pl
from jax.experimental.pallas import tpu as pltpu

F32 = jnp.float32
BF16 = jnp.bfloat16
ACT_DTYPE = F32

NORM_EPS = 1e-6
LNX_EPS = 64e-5
LRU_C = 8.0
N_MOD = 6
HEAD = 64
LANE = 128
SUBLANE = 8
CHUNK = 64
PAIR_UNROLL = 16
LRU_SCAN_UNROLL = 4
MM_RESID_SUB_ROWS = 128
FFN_SUB_ROWS = 128
RESIDENT_F32_BLOCK_MAX = 24 * 1024 * 1024
VMEM_LIMIT = 52 * 1024 * 1024


def _cp(n_axes):
    return pltpu.CompilerParams(dimension_semantics=("arbitrary",) * n_axes,
                                vmem_limit_bytes=VMEM_LIMIT)


def _pick(n, target, mult):
    best = None
    for t in range(mult, min(n, target) + 1, mult):
        if n % t == 0:
            best = t
    return best if best is not None else n


def _softplus(z):
    return jnp.maximum(z, 0.0) + jnp.log(1.0 + jnp.exp(-jnp.abs(z)))


def _sigmoid(z):
    return 1.0 / (1.0 + jnp.exp(-z))


def _gelu(x):
    return 0.5 * x * (1.0 + jnp.tanh(0.7978845608028654 * (x + 0.044715 * (x * x * x))))


def _act(x, name):
    if name is None:
        return x
    if name == "tanh":
        return jnp.tanh(x)
    if name == "sigmoid":
        return _sigmoid(x)
    if name == "silu":
        return x * _sigmoid(x)
    raise ValueError(name)


def _rms(x, g):
    ms = jnp.mean(x * x, axis=-1, keepdims=True)
    return x * lax.rsqrt(ms + NORM_EPS) * g


def _dot(x, y):
    return jnp.dot(x, y, preferred_element_type=F32)


def _dot_nt(x, y):
    return lax.dot_general(x, y, (((1,), (1,)), ((), ())), preferred_element_type=F32)


def _split_dot_left(m, x, parts):
    acc = None
    rem = x
    for i in range(parts):
        piece = rem.astype(BF16)
        term = jnp.dot(m, piece, preferred_element_type=F32)
        acc = term if acc is None else acc + term
        if i + 1 < parts:
            rem = rem - piece.astype(F32)
    return acc


def _dot_tn(x, y):
    return lax.dot_general(x, y, (((0,), (0,)), ((), ())), preferred_element_type=F32)


def _head_sum(x):
    d = x.shape[-1]
    first = lax.broadcasted_iota(jnp.int32, (1, LANE), 1) < HEAD
    cols = []
    for s in range(0, d, LANE):
        xs = x[:, s:s + LANE]
        lo = jnp.sum(jnp.where(first, xs, 0.0), axis=-1, keepdims=True)
        hi = jnp.sum(jnp.where(first, 0.0, xs), axis=-1, keepdims=True)
        cols.append(jnp.where(first, lo, hi))
    return cols[0] if len(cols) == 1 else jnp.concatenate(cols, axis=-1)


def _shift_rows(x, carry8, s):
    n = x.shape[0]
    rolled = pltpu.roll(x, s, 0)
    rolled_c = pltpu.roll(carry8, s, 0)
    row8 = lax.broadcasted_iota(jnp.int32, (SUBLANE, x.shape[1]), 0)
    top = jnp.where(row8 < s, rolled_c, rolled[0:SUBLANE])
    if n == SUBLANE:
        return top
    return jnp.concatenate([top, rolled[SUBLANE:]], axis=0)


def _mm_kernel(*refs, has_bias, has_x2, pre_act, act):
    refs = list(refs)
    x_ref, w_ref = refs[:2]
    x2_ref = refs[2] if has_x2 else None
    b_ref = refs[2 + has_x2] if has_bias else None
    o_ref = refs[2 + has_x2 + has_bias]
    o2_ref = refs[3 + has_x2 + has_bias] if has_x2 else None
    wb_ref = refs[-1]

    @pl.when(pl.program_id(1) == 0)
    def _():
        wb_ref[...] = w_ref[...].astype(BF16)

    def project(src, dst):
        x = src[...]
        if pre_act is not None:
            x = _act(x.astype(F32), pre_act)
        acc = _dot(x.astype(BF16), wb_ref[...])
        if has_bias:
            acc = acc + b_ref[...]
        dst[...] = _act(acc, act).astype(dst.dtype)

    project(x_ref, o_ref)
    if has_x2:
        @pl.when(pl.program_id(1) == pl.num_programs(1) - 1)
        def _():
            project(x2_ref, o2_ref)


def _wspec(w, layer, rows, cols, col_of):
    assert w.ndim == 3
    return pl.BlockSpec((None, rows, cols), lambda *ids: (layer, 0, col_of(*ids)))


def _mm(x, w, layer, bias=None, *, x2=None, pre_act=None, act=None, out_dtype=F32, tm_target=1024,
        tn_target=1024, name="mm"):
    m, k = x.shape
    n = w.shape[2]
    tm = _pick(m, tm_target, 16)
    tn = _pick(n, tn_target, LANE)
    in_specs = [pl.BlockSpec((tm, k), lambda j, i: (i, 0)),
                _wspec(w, layer, k, tn, lambda j, i: j)]
    args = [x, w]
    out_shape = [jax.ShapeDtypeStruct((m, n), out_dtype)]
    out_specs = [pl.BlockSpec((tm, tn), lambda j, i: (i, j))]
    if x2 is not None:
        m2 = x2.shape[0]
        in_specs.append(pl.BlockSpec((m2, k), lambda j, i: (0, 0)))
        args.append(x2)
        out_shape.append(jax.ShapeDtypeStruct((m2, n), out_dtype))
        out_specs.append(pl.BlockSpec((m2, tn), lambda j, i: (0, j)))
    if bias is not None:
        in_specs.append(pl.BlockSpec((1, tn), lambda j, i: (0, j)))
        args.append(bias.reshape(1, n))
    res = pl.pallas_call(
        functools.partial(_mm_kernel, has_bias=bias is not None, has_x2=x2 is not None, pre_act=pre_act, act=act),
        out_shape=out_shape,
        grid=(n // tn, m // tm),
        in_specs=in_specs,
        out_specs=out_specs,
        scratch_shapes=[pltpu.VMEM((k, tn), BF16)],
        compiler_params=_cp(2),
        name=name,
    )(*args)
    return res if x2 is not None else res[0]


def _mod_spec(mod, slot, d, tr):
    rm = mod.shape[1]
    if rm == 1:
        return pl.BlockSpec((1, 1, d), lambda g, t: (g, 0, slot))
    return pl.BlockSpec((1, tr, d), lambda g, t: (g, t, slot))


def _mm_resid_kernel(*refs, with_next, cast_inside, layer=0):
    refs = list(refs)
    a_ref, w_ref, x_ref, g1_ref, gate_ref = refs[:5]
    pos = 5
    if with_next:
        g2_ref, sc_ref, sh_ref = refs[pos:pos + 3]
        pos += 3
    xn_ref = refs[pos]
    h_ref = refs[pos + 1] if with_next else None
    if cast_inside == "dma":
        wb_ref, stage, sem = refs[-3:]
        kc = stage.shape[1]
        nck = wb_ref.shape[0] // kc

        @pl.when(jnp.logical_and(pl.program_id(0) == 0, pl.program_id(1) == 0))
        def _():
            def chunk(c):
                return pltpu.make_async_copy(w_ref.at[layer, pl.ds(c * kc, kc), :], stage.at[c % 2], sem.at[c % 2])

            chunk(0).start()
            for c in range(nck):
                if c + 1 < nck:
                    chunk(c + 1).start()
                chunk(c).wait()
                wb_ref[c * kc:(c + 1) * kc, :] = stage[c % 2].astype(BF16)
    elif cast_inside:
        wb_ref = refs[-1]

        @pl.when(jnp.logical_and(pl.program_id(0) == 0, pl.program_id(1) == 0))
        def _():
            wb_ref[...] = w_ref[...].astype(BF16)
    else:
        wb_ref = w_ref

    tm = a_ref.shape[1]
    ts = _pick(tm, MM_RESID_SUB_ROWS, 16)

    def rows_of(ref, rows):
        return ref[0] if ref.shape[1] == 1 else ref[0, rows, :]

    for s0 in range(0, tm, ts):
        rows = slice(s0, s0 + ts)
        o = _dot(a_ref[0, rows, :], wb_ref[...])
        xn = x_ref[0, rows, :] + rows_of(gate_ref, rows) * _rms(o, g1_ref[...])
        xn_ref[0, rows, :] = xn
        if with_next:
            h_ref[0, rows, :] = (_rms(xn, g2_ref[...]) * (1.0 + rows_of(sc_ref, rows))
                                 + rows_of(sh_ref, rows)).astype(BF16)


def _mm_resid(a, w, layer, x, g1, mod, gate_slot, nxt, name, tm_target=512):
    g, r, k = a.shape
    n = w.shape[-1]
    cast_inside = w.ndim == 3
    if cast_inside and k * n * 4 > RESIDENT_F32_BLOCK_MAX:
        cast_inside = "dma"
    tm = _pick(r, tm_target, 16)
    row = lambda c: pl.BlockSpec((1, tm, c), lambda gi, t: (gi, t, 0))
    vec = pl.BlockSpec((1, n), lambda gi, t: (0, 0))
    if cast_inside == "dma":
        kc = _pick(k, 256, 16)
        wspec = pl.BlockSpec(memory_space=pl.ANY)
        scratch = [pltpu.VMEM((k, n), BF16), pltpu.VMEM((2, kc, n), F32), pltpu.SemaphoreType.DMA((2,))]
    elif cast_inside:
        wspec = pl.BlockSpec((None, k, n), lambda gi, t: (layer, 0, 0), pipeline_mode=pl.Buffered(1))
        scratch = [pltpu.VMEM((k, n), BF16)]
    else:
        wspec = pl.BlockSpec((k, n), lambda gi, t: (0, 0), pipeline_mode=pl.Buffered(1))
        scratch = []
    in_specs = [row(k), wspec, row(n), vec, _mod_spec(mod, gate_slot, n, tm)]
    args = [a, w, x, g1.reshape(1, n), mod]
    out_shape = [jax.ShapeDtypeStruct((g, r, n), F32)]
    out_specs = [row(n)]
    if nxt is not None:
        g2, mod2, sc_slot, sh_slot = nxt
        in_specs += [vec, _mod_spec(mod2, sc_slot, n, tm), _mod_spec(mod2, sh_slot, n, tm)]
        args += [g2.reshape(1, n), mod2, mod2]
        out_shape.append(jax.ShapeDtypeStruct((g, r, n), BF16))
        out_specs.append(row(n))
    res = pl.pallas_call(
        functools.partial(_mm_resid_kernel, with_next=nxt is not None, cast_inside=cast_inside, layer=layer),
        out_shape=out_shape,
        grid=(g, r // tm),
        in_specs=in_specs,
        out_specs=out_specs,
        scratch_shapes=scratch,
        compiler_params=_cp(2),
        name=name,
    )(*args)
    return res if nxt is not None else res[0]


def _rwkv_prep_kernel(x_ref, g_ref, sc_ref, sh_ref, mix_ref, shift_ref,
                      w1_ref, w2_ref, w0_ref, a1_ref, a2_ref, a0_ref, g1_ref, g2_ref,
                      xr_ref, xk_ref, xv_ref, wpre_ref, a_ref, gate_ref, shift_out, *scratch, seq):
    h = _rms(x_ref[0], g_ref[...]) * (1.0 + sc_ref[0]) + sh_ref[0]
    tr = h.shape[0]
    if seq:
        carry = scratch[0]
        t = pl.program_id(1)

        @pl.when(t == 0)
        def _():
            carry[...] = jnp.broadcast_to(shift_ref[0], carry.shape)

        hprev = _shift_rows(h, carry[...], 1)
        carry[...] = h[tr - SUBLANE:tr]

        @pl.when(t == pl.num_programs(1) - 1)
        def _():
            shift_out[0] = h[tr - 1:tr]
    else:
        hprev = shift_ref[0]
        shift_out[0] = h
    xx = hprev - h

    def mixed(j):
        return (h + xx * mix_ref[j:j + 1, :]).astype(BF16)

    xr_ref[0] = mixed(0)
    xk_ref[0] = mixed(2)
    xv_ref[0] = mixed(3)
    wpre_ref[0] = w0_ref[...] + _dot(jnp.tanh(_dot(mixed(1), w1_ref[...])).astype(BF16), w2_ref[...])
    a = _sigmoid(a0_ref[...] + _dot(_dot(mixed(4), a1_ref[...]).astype(BF16), a2_ref[...]))
    a_ref[0] = a.astype(a_ref.dtype)
    gate_ref[0] = _dot(_sigmoid(_dot(mixed(5), g1_ref[...])).astype(BF16), g2_ref[...]).astype(gate_ref.dtype)


def _lora_pair(w1, w2):
    r = w1.shape[1]
    rp = -(-r // LANE) * LANE
    if rp != r:
        w1 = jnp.pad(w1, ((0, 0), (0, rp - r)))
        w2 = jnp.pad(w2, ((0, rp - r), (0, 0)))
    return w1.astype(BF16), w2.astype(BF16)


def _rwkv_prep(x, g, mod, mix, shift, seq, lora):
    gg, r, d = x.shape
    tr = _pick(r, 256, 16)
    w1, w2, w0, a1, a2, a0, g1, g2 = lora
    w1, w2 = _lora_pair(w1, w2)
    a1, a2 = _lora_pair(a1, a2)
    g1, g2 = _lora_pair(g1, g2)
    row = pl.BlockSpec((1, tr, d), lambda gi, t: (gi, t, 0))
    vec = pl.BlockSpec((1, d), lambda gi, t: (0, 0))
    full = lambda w: pl.BlockSpec(w.shape, lambda gi, t: (0, 0))
    if seq:
        shift_spec = pl.BlockSpec((1, 1, d), lambda gi, t: (gi, 0, 0))
        shift_shape = (gg, 1, d)
        scratch = [pltpu.VMEM((SUBLANE, d), F32)]
    else:
        shift_spec = row
        shift_shape = (gg, r, d)
        scratch = []
    act = lambda dt: jax.ShapeDtypeStruct((gg, r, d), dt)
    outs = pl.pallas_call(
        functools.partial(_rwkv_prep_kernel, seq=seq),
        out_shape=[act(BF16)] * 3 + [act(F32), act(BF16), act(BF16)] + [jax.ShapeDtypeStruct(shift_shape, F32)],
        grid=(gg, r // tr),
        in_specs=[row, vec, _mod_spec(mod, 1, d, tr), _mod_spec(mod, 0, d, tr),
                  pl.BlockSpec((6, d), lambda gi, t: (0, 0)), shift_spec,
                  full(w1), full(w2), vec, full(a1), full(a2), vec, full(g1), full(g2)],
        out_specs=[row] * 6 + [shift_spec],
        scratch_shapes=scratch,
        compiler_params=_cp(2),
        name="rwkv_prep",
    )(x, g.reshape(1, d), mod, mod, mix, shift, w1, w2, w0.reshape(1, d), a1, a2, a0.reshape(1, d), g1, g2)
    return outs[:6], outs[6]


def _wkv_terms(k, wpre, a, kkp, kap):
    logd = -0.6065306597126334 * _sigmoid(wpre)
    kk = k * kkp
    kk = kk * lax.rsqrt(jnp.maximum(_head_sum(kk * kk), 1e-24))
    kmod = k * (1.0 + (a - 1.0) * kap)
    return logd, kk, kmod


def _wkv_readout(y, bonus, gate, lnx_g, lnx_b):
    inv_n = 1.0 / HEAD
    yc = y - _head_sum(y) * inv_n
    var = _head_sum(yc * yc) * inv_n
    return ((yc * lax.rsqrt(var + LNX_EPS) * lnx_g + lnx_b + bonus) * gate).astype(BF16)


def _wkv_chunk_kernel(r_ref, k_ref, v_ref, w_ref, a_ref, gate_ref, kkp_ref, kap_ref, rk_ref, lg_ref, lb_ref,
                      xo_ref, sout_ref, S, Rt, At, Bt, Kt, Vs, Gc, Ys, Bonus):
    c = pl.program_id(1)
    n_pairs = S.shape[0]
    C = r_ref.shape[1]

    @pl.when(c == 0)
    def _():
        S[...] = jnp.zeros(S.shape, F32)

    a = a_ref[0].astype(F32)
    r = r_ref[0].astype(F32)
    logd, kk, kmod = _wkv_terms(k_ref[0].astype(F32), w_ref[0], a, kkp_ref[...], kap_ref[...])
    ri = lax.broadcasted_iota(jnp.int32, (C, C), 0)
    ci = lax.broadcasted_iota(jnp.int32, (C, C), 1)
    tri = (ri >= ci).astype(BF16)
    cum = _split_dot_left(tri, logd, 3)
    gam = jnp.exp(cum)
    rt = r * gam
    at = -kk * jnp.exp(cum - logd)
    ginv = jnp.exp(-cum)
    bt = kk * a * ginv
    kt = kmod * ginv
    v = v_ref[0].astype(F32)
    Bonus[...] = _head_sum(r * kmod * rk_ref[...]) * v
    for p in range(n_pairs):
        sl = slice(p * LANE, (p + 1) * LANE)
        Rt[p] = rt[:, sl]
        At[p] = at[:, sl]
        Bt[p] = bt[:, sl]
        Kt[p] = kt[:, sl]
        Vs[p] = v[:, sl]
        Gc[p] = jnp.broadcast_to(gam[C - 1:C, sl], (SUBLANE, LANE))

    lane = lax.broadcasted_iota(jnp.int32, (1, LANE), 1)
    first = lane < HEAD
    r2 = lax.broadcasted_iota(jnp.int32, (2 * C, 2 * C), 0)
    c2 = lax.broadcasted_iota(jnp.int32, (2 * C, 2 * C), 1)
    same_blk = (r2 < C) == (c2 < C)
    strict = jnp.logical_and(r2 > c2, same_blk)
    incl = jnp.logical_and(r2 >= c2, same_blk)
    rs = lax.broadcasted_iota(jnp.int32, (LANE, LANE), 0)
    cs = lax.broadcasted_iota(jnp.int32, (LANE, LANE), 1)
    same_head = (rs < HEAD) == (cs < HEAD)
    zero_blk = jnp.zeros((2 * C, LANE), BF16)

    def stack(x):
        return jnp.concatenate([jnp.where(first, x, 0.0), jnp.where(first, 0.0, x)], axis=0).astype(BF16)

    def twice(x):
        xb = x.astype(BF16)
        return jnp.concatenate([xb, xb], axis=0)

    n_fac = max(1, (C - 1).bit_length())

    def pairs_step(ps):
        n = range(len(ps))
        rp, ap, bp = [Rt[p] for p in ps], [At[p] for p in ps], [Bt[p] for p in ps]
        kp, vp = [Kt[p] for p in ps], [Vs[p] for p in ps]
        sp, gc = [S[p] for p in ps], [Gc[p] for p in ps]
        am, rm = [stack(t) for t in ap], [stack(t) for t in rp]
        bm, km = [twice(t) for t in bp], [twice(t) for t in kp]
        ar = [jnp.concatenate([am[i], rm[i]], axis=0) for i in n]
        gb = [_dot_nt(ar[i], bm[i]) for i in n]
        gk = [_dot_nt(ar[i], km[i]) for i in n]
        lp = [jnp.where(strict, t[:2 * C], 0.0) for t in gb]
        lak = [jnp.where(strict, t[:2 * C], 0.0).astype(BF16) for t in gk]
        grbk = [jnp.concatenate([jnp.where(incl, gb[i][2 * C:], 0.0), jnp.where(incl, gk[i][2 * C:], 0.0)],
                                axis=1).astype(BF16) for i in n]
        spb = [t.astype(BF16) for t in sp]
        ars = [_dot_nt(jnp.concatenate([ap[i], rp[i]], axis=0).astype(BF16), spb[i]) for i in n]
        a_s, r_s = [t[:C] for t in ars], [t[C:] for t in ars]
        vst = [jnp.concatenate([t, t], axis=0).astype(BF16) for t in vp]
        x = [jnp.concatenate([a_s[i], a_s[i]], axis=0) + _dot(lak[i], vst[i]) for i in n]
        for f in range(n_fac):
            lh = [t.astype(BF16) for t in lp]
            xh = [t.astype(BF16) for t in x]
            xl = [(x[i] - xh[i].astype(F32)).astype(BF16) for i in n]
            lhs = [jnp.concatenate([t, t], axis=1) for t in lh]
            if f + 1 < n_fac:
                rhs = [jnp.concatenate([jnp.concatenate([xh[i], lh[i]], axis=1),
                                        jnp.concatenate([xl[i], zero_blk], axis=1)], axis=0) for i in n]
                prod = [_dot(lhs[i], rhs[i]) for i in n]
                x = [x[i] + prod[i][:, :LANE] for i in n]
                lp = [t[:, LANE:] for t in prod]
            else:
                x = [x[i] + _dot(lhs[i], jnp.concatenate([xh[i], xl[i]], axis=0)) for i in n]
        yst = [jnp.concatenate([r_s[i], r_s[i]], axis=0)
               + _dot(grbk[i], jnp.concatenate([x[i].astype(BF16), vst[i]], axis=0)) for i in n]
        u = [jnp.where(first, t[:C], t[C:]) for t in x]
        ds = [_dot_tn(jnp.concatenate([u[i], vp[i]], axis=0).astype(BF16),
                      jnp.concatenate([bp[i], kp[i]], axis=0).astype(BF16)) for i in n]
        for i, p in enumerate(ps):
            Ys[p] = jnp.where(first, yst[i][:C], yst[i][C:])
            S[p] = jnp.where(same_head, (sp[i] + ds[i]) * gc[i][0:1, :], 0.0)

    for p0 in range(0, n_pairs, PAIR_UNROLL):
        pairs_step(list(range(p0, min(p0 + PAIR_UNROLL, n_pairs))))

    y = jnp.concatenate([Ys[p] for p in range(n_pairs)], axis=-1) if n_pairs > 1 else Ys[0]
    xo_ref[0] = _wkv_readout(y, Bonus[...], gate_ref[0].astype(F32), lg_ref[...], lb_ref[...])

    @pl.when(c == pl.num_programs(1) - 1)
    def _():
        sout_ref[0] = S[...]


def _wkv_chunked(r, k, v, wpre, a, gate, kkp, kap, rk, lnx_g, lnx_b):
    g, t, d = r.shape
    C = CHUNK
    np_ = d // LANE
    row = pl.BlockSpec((1, C, d), lambda gi, ci: (gi, ci, 0))
    vec = pl.BlockSpec((1, d), lambda gi, ci: (0, 0))
    pair_buf = pltpu.VMEM((np_, C, LANE), F32)
    return pl.pallas_call(
        _wkv_chunk_kernel,
        out_shape=[jax.ShapeDtypeStruct((g, t, d), BF16),
                   jax.ShapeDtypeStruct((g, np_, LANE, LANE), F32)],
        grid=(g, t // C),
        in_specs=[row] * 6 + [vec] * 5,
        out_specs=[row, pl.BlockSpec((1, np_, LANE, LANE), lambda gi, ci: (gi, 0, 0, 0))],
        scratch_shapes=[pltpu.VMEM((np_, LANE, LANE), F32), pair_buf, pair_buf, pair_buf, pair_buf, pair_buf,
                        pltpu.VMEM((np_, SUBLANE, LANE), F32), pair_buf, pltpu.VMEM((C, d), F32)],
        compiler_params=_cp(2),
        name="wkv_chunk",
    )(r, k, v, wpre, a, gate, *[p.reshape(1, d) for p in (kkp, kap, rk, lnx_g, lnx_b)])


def _wkv_vec_kernel(k_ref, w_ref, a_ref, r_ref, v_ref, kkp_ref, kap_ref,
                    d_ref, an_ref, b_ref, km_ref, rt_ref, vt_ref):
    a = a_ref[...].astype(F32)
    logd, kk, kmod = _wkv_terms(k_ref[...].astype(F32), w_ref[...], a, kkp_ref[...], kap_ref[...])
    d_ref[...] = jnp.exp(logd).T
    an_ref[...] = (-kk).T
    b_ref[...] = (kk * a).T
    km_ref[...] = kmod.T
    rt_ref[...] = r_ref[...].astype(F32).T
    vt_ref[...] = v_ref[...].astype(F32).T


def _wkv_vec(k, wpre, a, r, v, kkp, kap):
    b, d = k.shape
    full = pl.BlockSpec((b, d), lambda i: (0, 0))
    fullt = pl.BlockSpec((d, b), lambda i: (0, 0))
    vec = pl.BlockSpec((1, d), lambda i: (0, 0))
    return pl.pallas_call(
        _wkv_vec_kernel,
        out_shape=[jax.ShapeDtypeStruct((d, b), F32)] * 6,
        grid=(1,),
        in_specs=[full] * 5 + [vec, vec],
        out_specs=[fullt] * 6,
        compiler_params=_cp(1),
        name="wkv_vec",
    )(k, wpre, a, r, v, kkp.reshape(1, d), kap.reshape(1, d))


def _wkv_step_kernel(s_ref, d_ref, a_ref, b_ref, k_ref, r_ref, v_ref, sn_ref, y_ref):
    s = s_ref[...]
    key = lambda ref: ref[...][:, None]
    sa = jnp.sum(s * key(a_ref), axis=2, keepdims=True)
    sn = s * key(d_ref) + sa * key(b_ref) + v_ref[...] * key(k_ref)
    sn_ref[...] = sn
    y_ref[...] = jnp.sum(sn * key(r_ref), axis=2, keepdims=True)


def _wkv_step(s, dvec, an, bvec, kmod, r, v):
    h, n, _, b = s.shape
    hb = _pick(h, 2, 1)
    sblk = pl.BlockSpec((hb, n, n, b), lambda i: (i, 0, 0, 0))
    kblk = pl.BlockSpec((hb, n, b), lambda i: (i, 0, 0))
    vblk = pl.BlockSpec((hb, n, 1, b), lambda i: (i, 0, 0, 0))
    kv = lambda x: x.reshape(h, n, b)
    sn, y = pl.pallas_call(
        _wkv_step_kernel,
        out_shape=[jax.ShapeDtypeStruct((h, n, n, b), F32), jax.ShapeDtypeStruct((h, n, 1, b), F32)],
        grid=(h // hb,),
        in_specs=[sblk] + [kblk] * 5 + [vblk],
        out_specs=[sblk, vblk],
        compiler_params=_cp(1),
        name="wkv_step",
    )(s, kv(dvec), kv(an), kv(bvec), kv(kmod), kv(r), v.reshape(h, n, 1, b))
    return sn, y.reshape(h * n, b)


def _wkv_post_kernel(y_ref, r_ref, k_ref, v_ref, a_ref, g_ref, kap_ref, rk_ref, lg_ref, lb_ref, o_ref):
    f32 = lambda ref: ref[0].astype(F32)
    kmod = f32(k_ref) * (1.0 + (f32(a_ref) - 1.0) * kap_ref[...])
    bonus = _head_sum(f32(r_ref) * kmod * rk_ref[...]) * f32(v_ref)
    o_ref[0] = _wkv_readout(y_ref[0], bonus, f32(g_ref), lg_ref[...], lb_ref[...])


def _wkv_post(y, r, k, v, a, g, kap, rk, lnx_g, lnx_b):
    gg, t, d = y.shape
    tr = _pick(t, 256, 16)
    row = pl.BlockSpec((1, tr, d), lambda gi, ti: (gi, ti, 0))
    vec = pl.BlockSpec((1, d), lambda gi, ti: (0, 0))
    return pl.pallas_call(
        _wkv_post_kernel,
        out_shape=jax.ShapeDtypeStruct((gg, t, d), BF16),
        grid=(gg, t // tr),
        in_specs=[row] * 6 + [vec] * 4,
        out_specs=row,
        compiler_params=_cp(2),
        name="wkv_post",
    )(y, r, k, v, a, g, kap.reshape(1, d), rk.reshape(1, d), lnx_g.reshape(1, d), lnx_b.reshape(1, d))


def _lru_conv_kernel(x_ref, w_ref, b_ref, buf_ref, xc_ref, *, width):
    acc = b_ref[...] + x_ref[0].astype(F32) * w_ref[width - 1:width, :]
    for s in range(1, width):
        acc = acc + buf_ref[0, :, width - 1 - s, :] * w_ref[width - 1 - s:width - s, :]
    xc_ref[0] = acc


def _lru_conv(proj, w, b, buf):
    g, r, d2 = proj.shape
    d = d2 // 2
    width = w.shape[0]
    tr = _pick(r, 256, 16)
    row = pl.BlockSpec((1, tr, d), lambda gi, t: (gi, t, 0))
    return pl.pallas_call(
        functools.partial(_lru_conv_kernel, width=width),
        out_shape=jax.ShapeDtypeStruct((g, r, d), F32),
        grid=(g, r // tr),
        in_specs=[pl.BlockSpec((1, tr, d), lambda gi, t: (gi, t, 1)),
                  pl.BlockSpec((width, d), lambda gi, t: (0, 0)),
                  pl.BlockSpec((1, d), lambda gi, t: (0, 0)),
                  pl.BlockSpec((1, tr, width - 1, d), lambda gi, t: (gi, t, 0, 0))],
        out_specs=row,
        compiler_params=_cp(2),
        name="lru_conv",
    )(proj, w, b.reshape(1, d), buf)


def _lru_gates_kernel(x_ref, wa_ref, ba_ref, wx_ref, bx_ref, ga_ref, gx_ref):
    x = x_ref[...].astype(BF16)
    ga_ref[...] = _sigmoid(_dot(x, wa_ref[0].astype(BF16)) + ba_ref[...])
    gx_ref[...] = _sigmoid(_dot(x, wx_ref[0].astype(BF16)) + bx_ref[...])


def _lru_gates(xc, wa, ba, wx, bx):
    m, d = xc.shape
    nb, bw, _ = wa.shape
    tm = _pick(m, 1024, 8)
    xs = pl.BlockSpec((tm, bw), lambda n, i: (i, n))
    ws = pl.BlockSpec((1, bw, bw), lambda n, i: (n, 0, 0))
    bs = pl.BlockSpec((1, bw), lambda n, i: (0, n))
    return pl.pallas_call(
        _lru_gates_kernel,
        out_shape=[jax.ShapeDtypeStruct((m, d), F32)] * 2,
        grid=(nb, m // tm),
        in_specs=[xs, ws, bs, ws, bs],
        out_specs=[xs, xs],
        compiler_params=_cp(2),
        name="lru_gates",
    )(xc, wa, ba.reshape(1, d), wx, bx.reshape(1, d))


def _lru_coeffs(ga, gx, xc, lam):
    log_a = -LRU_C * ga * _softplus(-lam)
    a = jnp.exp(log_a)
    mult = jnp.sqrt(1.0 - a * a)
    return a, mult


def _lru_seq_kernel(xin_ref, y_ref, cw_ref, cb_ref, wa_ref, ba_ref, wx_ref, bx_ref, lam_ref, h0_ref, buf_ref,
                    o_ref, hl_ref, nbuf_ref, cconv, carry, abuf, bbuf, hbuf, *, width):
    t = pl.program_id(2)
    tr = xin_ref.shape[1]

    @pl.when(t == 0)
    def _():
        cconv[...] = buf_ref[0]
        carry[...] = jnp.broadcast_to(h0_ref[0], carry.shape)

    x = xin_ref[0].astype(F32)
    c8 = cconv[...]
    xc = cb_ref[...] + x * cw_ref[width - 1:width, :]
    for s in range(1, width):
        xc = xc + _shift_rows(x, c8, s) * cw_ref[width - 1 - s:width - s, :]
    cconv[...] = x[tr - SUBLANE:tr]
    nbuf_ref[0] = x[tr - SUBLANE:tr]
    xb = xc.astype(BF16)
    ga = _sigmoid(_dot(xb, wa_ref[...].astype(BF16)) + ba_ref[...])
    gx = _sigmoid(_dot(xb, wx_ref[...].astype(BF16)) + bx_ref[...])
    a, mult = _lru_coeffs(ga, gx, xc, lam_ref[...])
    row = lax.broadcasted_iota(jnp.int32, a.shape, 0)
    mult = jnp.where(jnp.logical_and(t == 0, row == 0), 1.0, mult)
    abuf[...] = a
    bbuf[...] = mult * gx * xc
    r8 = lax.broadcasted_iota(jnp.int32, (SUBLANE, a.shape[1]), 0)

    def body(i, c):
        off = pl.multiple_of(i * SUBLANE, SUBLANE)
        aa = abuf[pl.ds(off, SUBLANE), :]
        bb = bbuf[pl.ds(off, SUBLANE), :]
        for s in (1, 2, 4):
            a_s = jnp.where(r8 >= s, pltpu.roll(aa, s, 0), 1.0)
            b_s = jnp.where(r8 >= s, pltpu.roll(bb, s, 0), 0.0)
            bb = aa * b_s + bb
            aa = aa * a_s
        h = aa * c + bb
        hbuf[pl.ds(off, SUBLANE), :] = h
        return h[SUBLANE - 1:SUBLANE, :]

    c = lax.fori_loop(0, tr // SUBLANE, body, carry[0:1, :], unroll=LRU_SCAN_UNROLL)
    carry[...] = jnp.broadcast_to(c, carry.shape)
    o_ref[0] = (hbuf[...] * _gelu(y_ref[0].astype(F32))).astype(BF16)

    @pl.when(t == pl.num_programs(2) - 1)
    def _():
        hl_ref[0] = c


def _lru_seq(proj, cw, cb, wa, ba, wx, bx, lam, h0, buf):
    g, r, d2 = proj.shape
    d = d2 // 2
    nb, bw, _ = wa.shape
    width = cw.shape[0]
    tr = _pick(r, 512, 16)
    row = lambda off: pl.BlockSpec((1, tr, bw), lambda gi, n, t: (gi, t, n + off))
    vec = pl.BlockSpec((1, bw), lambda gi, n, t: (0, n))
    wsp = pl.BlockSpec((None, bw, bw), lambda gi, n, t: (n, 0, 0))
    st = pl.BlockSpec((1, 1, bw), lambda gi, n, t: (gi, 0, n))
    st8 = pl.BlockSpec((1, SUBLANE, bw), lambda gi, n, t: (gi, 0, n))
    tile = pltpu.VMEM((tr, bw), F32)
    small = pltpu.VMEM((SUBLANE, bw), F32)
    return pl.pallas_call(
        functools.partial(_lru_seq_kernel, width=width),
        out_shape=[jax.ShapeDtypeStruct((g, r, d), BF16), jax.ShapeDtypeStruct((g, 1, d), F32),
                   jax.ShapeDtypeStruct((g, SUBLANE, d), F32)],
        grid=(g, nb, r // tr),
        in_specs=[row(nb), row(0), pl.BlockSpec((width, bw), lambda gi, n, t: (0, n)), vec,
                  wsp, vec, wsp, vec, vec, st, st8],
        out_specs=[row(0), st, st8],
        scratch_shapes=[small, small, tile, tile, tile],
        compiler_params=_cp(3),
        name="lru_seq",
    )(proj, proj, cw, cb.reshape(1, d), wa, ba.reshape(1, d), wx, bx.reshape(1, d), lam.reshape(1, d), h0, buf)


def _lru_step_kernel(ga_ref, gx_ref, xc_ref, y_ref, lam_ref, h0_ref, o_ref, h_ref):
    xc = xc_ref[0]
    a, mult = _lru_coeffs(ga_ref[0], gx_ref[0], xc, lam_ref[...])
    h = a * h0_ref[0] + mult * gx_ref[0] * xc
    h_ref[0] = h
    o_ref[0] = (h * _gelu(y_ref[0].astype(F32))).astype(BF16)


def _lru_step(ga, gx, xc, proj, lam, h0):
    g, r, d = ga.shape
    row = pl.BlockSpec((1, r, d), lambda i: (0, 0, 0))
    return pl.pallas_call(
        _lru_step_kernel,
        out_shape=[jax.ShapeDtypeStruct((g, r, d), BF16), jax.ShapeDtypeStruct((g, r, d), F32)],
        grid=(1,),
        in_specs=[row, row, row, row, pl.BlockSpec((1, d), lambda i: (0, 0)), row],
        out_specs=[row, row],
        compiler_params=_cp(1),
        name="lru_step",
    )(ga, gx, xc, proj, lam.reshape(1, d), h0)


def _ffn_up_kernel(x_ref, wg_ref, wu_ref, cw_ref, cb_ref, buf_ref, x2_ref, hist2_ref,
                   o_ref, nb_ref, o2_ref, nhist2_ref, wgb, wub, carry, *, tiles_per_seq):
    i = pl.program_id(1)

    @pl.when(i == 0)
    def _():
        wgb[...] = wg_ref[...].astype(BF16)
        wub[...] = wu_ref[...].astype(BF16)

    def gated(gt, p1, p2, u):
        gc = cb_ref[...] + p2 * cw_ref[0:1, :] + p1 * cw_ref[1:2, :] + gt * cw_ref[2:3, :]
        return (_gelu(gc) * u).astype(BF16)

    tm = x_ref.shape[0]
    ts = _pick(tm, FFN_SUB_ROWS, 16)

    @pl.when(i % tiles_per_seq == 0)
    def _():
        carry[...] = buf_ref[0]

    c8 = carry[...]
    for s0 in range(0, tm, ts):
        rows = slice(s0, s0 + ts)
        x = x_ref[rows, :]
        gt = _dot(x, wgb[...])
        u = _dot(x, wub[...])
        o_ref[rows, :] = gated(gt, _shift_rows(gt, c8, 1), _shift_rows(gt, c8, 2), u)
        c8 = gt[ts - SUBLANE:ts]
    carry[...] = c8
    nb_ref[0] = c8

    @pl.when(i == pl.num_programs(1) - 1)
    def _():
        x2 = x2_ref[...]
        gt = _dot(x2, wgb[...])
        p1 = hist2_ref[:, 1, :]
        o2_ref[...] = gated(gt, p1, hist2_ref[:, 0, :], _dot(x2, wub[...]))
        nhist2_ref[:, 0, :] = p1
        nhist2_ref[:, 1, :] = gt


def _ffn_up(x, wg, wu, layer, cw, cb, buf, rows_per_seq, x2, hist2):
    m, k = x.shape
    m2 = x2.shape[0]
    f = wg.shape[2]
    tn = _pick(f, 512, LANE)
    tm = _pick(rows_per_seq, 2048, 16)
    tps = rows_per_seq // tm
    g = m // rows_per_seq
    ws = _wspec(wg, layer, k, tn, lambda j, i: j)
    bspec = pl.BlockSpec((1, SUBLANE, tn), lambda j, i: (i // tps, 0, j))
    hspec = pl.BlockSpec((m2, 2, tn), lambda j, i: (0, 0, j))
    o2spec = pl.BlockSpec((m2, tn), lambda j, i: (0, j))
    return pl.pallas_call(
        functools.partial(_ffn_up_kernel, tiles_per_seq=tps),
        out_shape=[jax.ShapeDtypeStruct((m, f), BF16), jax.ShapeDtypeStruct((g, SUBLANE, f), F32),
                   jax.ShapeDtypeStruct((m2, f), BF16), jax.ShapeDtypeStruct((m2, 2, f), F32)],
        grid=(f // tn, m // tm),
        in_specs=[pl.BlockSpec((tm, k), lambda j, i: (i, 0)), ws, ws,
                  pl.BlockSpec((3, tn), lambda j, i: (0, j)), pl.BlockSpec((1, tn), lambda j, i: (0, j)),
                  bspec, pl.BlockSpec((m2, k), lambda j, i: (0, 0)), hspec],
        out_specs=[pl.BlockSpec((tm, tn), lambda j, i: (i, j)), bspec, o2spec, hspec],
        scratch_shapes=[pltpu.VMEM((k, tn), BF16), pltpu.VMEM((k, tn), BF16), pltpu.VMEM((SUBLANE, tn), F32)],
        compiler_params=_cp(2),
        name="ffn_up",
    )(x, wg, wu, cw, cb.reshape(1, f), buf, x2, hist2)


def _trunk(xp, xs, mods_p, mods_s, st, P):
    gp, rp, d = xp.shape
    rs = xs.shape[1]
    depth = P["w_mod"].shape[0]
    keys = ("wkv", "shift", "h", "lconv", "fconv")
    out_p = {k: [] for k in keys}
    out_s = {k: [] for k in keys}
    flat = lambda t: t.reshape(-1, t.shape[-1])
    unp = lambda t: t.reshape(gp, rp, t.shape[-1])
    uns = lambda t: t.reshape(1, rs, t.shape[-1])
    hp = hs = None
    for i in range(depth):
        mp, ms = mods_p[i], mods_s[i]
        j = i // 2
        if i % 2 == 0:
            lora = tuple(P[nm][j] for nm in ("rw_w1", "rw_w2", "rw_w0", "rw_a1", "rw_a2", "rw_a0", "rw_g1", "rw_g2"))
            g0, mix = P["norm_g"][i, 0], P["rw_mix"][j]
            pre_p, shift_p = _rwkv_prep(xp, g0, mp, mix, jnp.zeros((gp, 1, d), F32), True, lora)
            pre_s, shift_s = _rwkv_prep(xs, g0, ms, mix, st["shift"][j].reshape(1, rs, d), False, lora)
            xr_p, xk_p, xv_p, w_p, a_p, g_p = pre_p
            xr_s, xk_s, xv_s, w_s, a_s, g_s = pre_s
            rr_p, rr_s = _mm(flat(xr_p), P["rw_wr"], j, x2=flat(xr_s), out_dtype=ACT_DTYPE, name="rw_r")
            kk_p, kk_s = _mm(flat(xk_p), P["rw_wk"], j, x2=flat(xk_s), out_dtype=ACT_DTYPE, name="rw_k")
            vv_p, vv_s = _mm(flat(xv_p), P["rw_wv"], j, x2=flat(xv_s), out_dtype=ACT_DTYPE, name="rw_v")
            heads = (P["rw_kk"][j], P["rw_ka"][j], P["rw_rk"][j], P["rw_lnx_g"][j], P["rw_lnx_b"][j])
            xo_p, s_bd = _wkv_chunked(unp(rr_p), unp(kk_p), unp(vv_p), w_p, a_p, g_p, *heads)
            npairs = d // LANE
            s_new = jnp.stack([s_bd[:, :, :HEAD, :HEAD], s_bd[:, :, HEAD:, HEAD:]], axis=2)
            out_p["wkv"].append(s_new.reshape(gp, 2 * npairs, HEAD, HEAD))
            out_p["shift"].append(shift_p.reshape(gp, d))
            s_t = jnp.transpose(st["wkv"][j], (1, 2, 3, 0))
            vecs = _wkv_vec(kk_s, flat(w_s), flat(a_s), rr_s, vv_s, heads[0], heads[1])
            s_t, y_s = _wkv_step(s_t, *vecs)
            out_s["wkv"].append(jnp.transpose(s_t, (3, 0, 1, 2)))
            out_s["shift"].append(shift_s.reshape(rs, d))
            xo_s = _wkv_post(y_s.T.reshape(1, rs, d), uns(rr_s), uns(kk_s), uns(vv_s), a_s, g_s, *heads[1:])
            mo_p, mo_s, w_out = xo_p, xo_s, P["rw_wo"]
        else:
            proj_p, proj_s = _mm(flat(hp), P["lru_w_in"], j, P["lru_b_in"][j], x2=flat(hs), out_dtype=ACT_DTYPE,
                                 name="lru_in")
            dr = proj_p.shape[1] // 2
            width = P["lru_conv_w"].shape[1]
            cw, cb, lam = P["lru_conv_w"][j], P["lru_conv_b"][j], P["lru_lambda"][j]
            gates = (P["lru_wa"][j], P["lru_ba"][j], P["lru_wx"][j], P["lru_bx"][j])
            hy_p, hl_p, nb8 = _lru_seq(unp(proj_p), cw, cb, *gates, lam, jnp.zeros((gp, 1, dr), F32),
                                       jnp.zeros((gp, SUBLANE, dr), F32))
            out_p["lconv"].append(nb8[:, SUBLANE - (width - 1):, :])
            out_p["h"].append(hl_p.reshape(gp, dr))
            hist = st["lru_conv"][j]
            xc = _lru_conv(uns(proj_s), cw, cb, hist.reshape(1, rs, width - 1, dr))
            out_s["lconv"].append(jnp.concatenate([hist[:, 1:], proj_s[:, None, dr:]], axis=1))
            ga, gx = _lru_gates(flat(xc), *gates)
            hy_s, hl_s = _lru_step(uns(ga), uns(gx), xc, uns(proj_s), lam, st["lru_h"][j].reshape(1, rs, dr))
            out_s["h"].append(hl_s.reshape(rs, dr))
            mo_p, mo_s, w_out = hy_p, hy_s, P["lru_w_out"]
        g1, g2 = P["norm_g"][i, 1], P["norm_g"][i, 2]
        xp, hf_p = _mm_resid(mo_p, w_out, j, xp, g1, mp, 2, (g2, mp, 4, 3), name="mix_out")
        xs, hf_s = _mm_resid(mo_s, w_out, j, xs, g1, ms, 2, (g2, ms, 4, 3), name="mix_out")
        fdim = P["ffn_w_gate"].shape[2]
        hact_p, nb8, hact_s, nhist = _ffn_up(
            flat(hf_p), P["ffn_w_gate"], P["ffn_w_up"], i, P["ffn_conv_w"][i], P["ffn_conv_b"][i],
            jnp.zeros((gp, SUBLANE, fdim), F32), rp, flat(hf_s), st["ffn_conv"][i])
        out_p["fconv"].append(nb8[:, SUBLANE - 2:, :])
        out_s["fconv"].append(nhist)
        wd = P["ffn_w_down"]
        g3 = P["norm_g"][i, 3]
        if i + 1 < depth and (i + 1) % 2 == 1:
            nxt = lambda mod: (P["norm_g"][i + 1, 0], mod, 1, 0)
            xp, hp = _mm_resid(unp(hact_p), wd, i, xp, g3, mp, 5, nxt(mods_p[i + 1]), name="ffn_down",
                               tm_target=256)
            xs, hs = _mm_resid(uns(hact_s), wd, i, xs, g3, ms, 5, nxt(mods_s[i + 1]), name="ffn_down")
        else:
            xp = _mm_resid(unp(hact_p), wd, i, xp, g3, mp, 5, None, name="ffn_down", tm_target=256)
            xs = _mm_resid(uns(hact_s), wd, i, xs, g3, ms, 5, None, name="ffn_down")
    stack = lambda o: tuple(jnp.stack(o[k]) for k in keys)
    return (xp,) + stack(out_p), (xs,) + stack(out_s)


def kernel(x_prompt, x_sample, c_prompt, c_sample, state_rwkv_wkv, state_rwkv_shift, state_lru_h,
           state_lru_conv, state_ffn_conv, w_mod, b_mod, norm_g, rw_mix, rw_wr, rw_wk, rw_wv, rw_wo,
           rw_w0, rw_w1, rw_w2, rw_a0, rw_a1, rw_a2, rw_g1, rw_g2, rw_kk, rw_ka, rw_rk, rw_lnx_g,
           rw_lnx_b, lru_w_in, lru_b_in, lru_conv_w, lru_conv_b, lru_wa, lru_ba, lru_wx, lru_bx,
           lru_lambda, lru_w_out, ffn_w_gate, ffn_w_up, ffn_w_down, ffn_conv_w, ffn_conv_b):
    P = dict(w_mod=w_mod, b_mod=b_mod, norm_g=norm_g, rw_mix=rw_mix, rw_wr=rw_wr, rw_wk=rw_wk,
             rw_wv=rw_wv, rw_wo=rw_wo, rw_w0=rw_w0, rw_w1=rw_w1, rw_w2=rw_w2, rw_a0=rw_a0, rw_a1=rw_a1,
             rw_a2=rw_a2, rw_g1=rw_g1, rw_g2=rw_g2, rw_kk=rw_kk, rw_ka=rw_ka, rw_rk=rw_rk,
             rw_lnx_g=rw_lnx_g, rw_lnx_b=rw_lnx_b, lru_w_in=lru_w_in, lru_b_in=lru_b_in,
             lru_conv_w=lru_conv_w, lru_conv_b=lru_conv_b, lru_wa=lru_wa, lru_ba=lru_ba, lru_wx=lru_wx,
             lru_bx=lru_bx, lru_lambda=lru_lambda, lru_w_out=lru_w_out, ffn_w_gate=ffn_w_gate,
             ffn_w_up=ffn_w_up, ffn_w_down=ffn_w_down, ffn_conv_w=ffn_conv_w, ffn_conv_b=ffn_conv_b)
    bp, t, d = x_prompt.shape
    bs = x_sample.shape[0]
    depth = w_mod.shape[0]
    c_p = jnp.pad(c_prompt, ((0, -bp % 16), (0, 0)))
    mods_p, mods_s = [], []
    for i in range(depth):
        mod_p, mod_s = _mm(c_p, w_mod, i, b_mod[i], x2=c_sample, pre_act="silu", name="mod")
        mods_p.append(mod_p[:bp].reshape(bp, 1, N_MOD * d))
        mods_s.append(mod_s.reshape(1, bs, N_MOD * d))
    st = dict(wkv=state_rwkv_wkv, shift=state_rwkv_shift, lru_h=state_lru_h,
              lru_conv=state_lru_conv, ffn_conv=state_ffn_conv)
    (y_p, *state_p), (y_s, *state_s) = _trunk(x_prompt, x_sample.reshape(1, bs, d), mods_p, mods_s, st, P)
    return (y_p, y_s.reshape(bs, 1, d), *state_p, *state_s)
```
